```python
import math
import jax, jax.numpy as jnp
from jax import lax
import numpy as np


D_MODEL = 1024
BATCH = 8
SEQ = 2048
DEPTH = 2

CHUNK = 64
EPS = 1e-6

GM_WIDTH = 1024
GM_GROUPS = 4
GM_BLOCK = 128

MLA_HEADS = 8
MLA_NOPE = 128
MLA_ROPE = 64
MLA_VDIM = 128
MLA_QK_DIM = MLA_NOPE + MLA_ROPE
MLA_Q_RANK = 384
MLA_KV_RANK = 256
MLA_WIDTH = MLA_HEADS * MLA_VDIM
ROPE_THETA = 10000.0
Q_BLOCK = 128

LRU_WIDTH = 1280
LRU_BLOCKS = 16
LRU_BW = LRU_WIDTH // LRU_BLOCKS
LRU_C = 8.0
CONV_W = 4

IN_SIZES = (GM_WIDTH, GM_WIDTH, GM_WIDTH, MLA_Q_RANK, MLA_KV_RANK, MLA_ROPE, MLA_WIDTH,
            LRU_WIDTH, LRU_WIDTH, D_MODEL, D_MODEL, D_MODEL)
N_IN = sum(IN_SIZES)

kernel_name = "hybrid_gmlp_mla_rglru_chunk_causal"


def _split_points():
    return [int(s) for s in np.cumsum(IN_SIZES)[:-1]]


def rmsnorm(x, g):
    xf = x.astype(jnp.float32)
    y = xf * lax.rsqrt(jnp.mean(xf * xf, axis=-1, keepdims=True) + EPS)
    return (y * g.astype(jnp.float32)).astype(x.dtype)


def layernorm(x, g, b):
    xf = x.astype(jnp.float32)
    mu = jnp.mean(xf, axis=-1, keepdims=True)
    var = jnp.mean(jnp.square(xf - mu), axis=-1, keepdims=True)
    y = (xf - mu) * lax.rsqrt(var + EPS)
    return (y * g.astype(jnp.float32) + b.astype(jnp.float32)).astype(x.dtype)


def rope(t, cos, sin):
    half = t.shape[-1] // 2
    t1, t2 = t[..., :half], t[..., half:]
    return jnp.concatenate([t1 * cos - t2 * sin, t2 * cos + t1 * sin], axis=-1)


def gmlp_spatial(u, v, ln_g, ln_b, ws, bs):
    B, S, _ = v.shape
    v = layernorm(v, ln_g, ln_b)
    vb = v.reshape(B, S // GM_BLOCK, GM_BLOCK, GM_GROUPS, GM_WIDTH // GM_GROUPS)
    idx = jnp.arange(GM_BLOCK)
    mask = (idx[None, :] // CHUNK) <= (idx[:, None] // CHUNK)
    ws_m = jnp.where(mask[None], ws, jnp.zeros_like(ws))
    sv = jnp.einsum('gij,bnjgc->bnigc', ws_m, vb) + bs.T[None, None, :, :, None]
    return u * sv.reshape(B, S, GM_WIDTH)


def mla_attention(c_q, c_kv, k_rope_in, q_norm_g, w_uq, kv_norm_g, w_ukv):
    B, S, _ = c_q.shape
    H = MLA_HEADS
    q = (rmsnorm(c_q, q_norm_g) @ w_uq).reshape(B, S, H, MLA_QK_DIM)
    kv = (rmsnorm(c_kv, kv_norm_g) @ w_ukv).reshape(B, S, H, MLA_NOPE + MLA_VDIM)
    q_nope, q_rope = q[..., :MLA_NOPE], q[..., MLA_NOPE:]
    k_nope, v = kv[..., :MLA_NOPE], kv[..., MLA_NOPE:]

    pos = jnp.arange(S, dtype=jnp.float32)
    inv_freq = ROPE_THETA ** (-jnp.arange(0, MLA_ROPE, 2, dtype=jnp.float32) / MLA_ROPE)
    ang = pos[:, None] * inv_freq[None, :]
    cos = jnp.cos(ang).astype(q.dtype)
    sin = jnp.sin(ang).astype(q.dtype)
    q_rope = rope(q_rope, cos[None, :, None, :], sin[None, :, None, :])
    k_rope = rope(k_rope_in, cos[None], sin[None])

    q = jnp.concatenate([q_nope, q_rope], axis=-1)
    k = jnp.concatenate([k_nope, jnp.broadcast_to(k_rope[:, :, None, :], (B, S, H, MLA_ROPE))], axis=-1)
    scale = 1.0 / math.sqrt(MLA_QK_DIM)
    nb = S // Q_BLOCK
    qb = q.reshape(B, nb, Q_BLOCK, H, MLA_QK_DIM).transpose(1, 0, 2, 3, 4)
    key_chunk = jnp.arange(S) // CHUNK

    def one_block(args):
        qi, bi = args
        s = jnp.einsum('bqhd,bkhd->bhqk', qi, k).astype(jnp.float32) * scale
        q_chunk = (bi * Q_BLOCK + jnp.arange(Q_BLOCK)) // CHUNK
        mask = key_chunk[None, :] <= q_chunk[:, None]
        s = jnp.where(mask[None, None], s, -1e30)
        p = jax.nn.softmax(s, axis=-1).astype(v.dtype)
        return jnp.einsum('bhqk,bkhd->bqhd', p, v)

    o = lax.map(one_block, (qb, jnp.arange(nb)))
    return o.transpose(1, 0, 2, 3, 4).reshape(B, S, MLA_WIDTH)


def rg_lru(x_c, conv_w, conv_b, w_a, b_a, w_x, b_x, lam):
    B, S, C = x_c.shape
    xc = lax.conv_general_dilated(
        x_c, conv_w[:, None, :].astype(x_c.dtype), window_strides=(1,), padding=[(CONV_W - 1, 0)],
        dimension_numbers=('NWC', 'WIO', 'NWC'), feature_group_count=C) + conv_b
    xb = xc.reshape(B, S, LRU_BLOCKS, LRU_BW)
    r = jax.nn.sigmoid(jnp.einsum('bshi,hij->bshj', xb, w_a).reshape(B, S, C) + b_a)
    i = jax.nn.sigmoid(jnp.einsum('bshi,hij->bshj', xb, w_x).reshape(B, S, C) + b_x)
    rf = r.astype(jnp.float32)
    log_a = -LRU_C * rf * jax.nn.softplus(-lam.astype(jnp.float32))
    a = jnp.exp(log_a)
    mult = jnp.sqrt(jnp.maximum(1.0 - jnp.exp(2.0 * log_a), 0.0))
    bterm = mult * (i.astype(jnp.float32) * xc.astype(jnp.float32))

    def combine(e1, e2):
        a1, b1 = e1
        a2, b2 = e2
        return a1 * a2, a2 * b1 + b2

    _, h = lax.associative_scan(combine, (a, bterm), axis=1)
    return h.astype(x_c.dtype)


def setup_inputs(seed: int = 0) -> dict:
    key = jax.random.key(seed)
    ks = iter(jax.random.split(key, 32))
    L, D = DEPTH, D_MODEL

    def nrm(shape, scale):
        return jax.random.normal(next(ks), shape, jnp.float32) * scale

    def gain(shape):
        return 1.0 + nrm(shape, 0.02)

    a_init = jax.random.uniform(next(ks), (L, LRU_WIDTH), jnp.float32, 0.9, 0.999)
    s = a_init ** (1.0 / LRU_C)
    lam = jnp.log(s) - jnp.log1p(-s)

    return {
        "x": nrm((BATCH, SEQ, D), 1.0),
        "pre_norm_g": gain((L, D)),
        "w_in": nrm((L, D, N_IN), D ** -0.5),
        "gm_ln_g": gain((L, GM_WIDTH)),
        "gm_ln_b": nrm((L, GM_WIDTH), 0.02),
        "gm_ws": nrm((L, GM_GROUPS, GM_BLOCK, GM_BLOCK), GM_BLOCK ** -0.5),
        "gm_bs": 1.0 + nrm((L, GM_GROUPS, GM_BLOCK), 0.02),
        "mla_q_norm_g": gain((L, MLA_Q_RANK)),
        "mla_w_uq": nrm((L, MLA_Q_RANK, MLA_HEADS * MLA_QK_DIM), MLA_Q_RANK ** -0.5),
        "mla_kv_norm_g": gain((L, MLA_KV_RANK)),
        "mla_w_ukv": nrm((L, MLA_KV_RANK, MLA_HEADS * (MLA_NOPE + MLA_VDIM)), MLA_KV_RANK ** -0.5),
        "lru_conv_w": nrm((L, CONV_W, LRU_WIDTH), CONV_W ** -0.5),
        "lru_conv_b": nrm((L, LRU_WIDTH), 0.01),
        "lru_w_a": nrm((L, LRU_BLOCKS, LRU_BW, LRU_BW), LRU_BW ** -0.5),
        "lru_b_a": nrm((L, LRU_WIDTH), 0.01),
        "lru_w_x": nrm((L, LRU_BLOCKS, LRU_BW, LRU_BW), LRU_BW ** -0.5),
        "lru_b_x": nrm((L, LRU_WIDTH), 0.01),
        "lru_lambda": lam,
        "w_proj_a": nrm((L, GM_WIDTH, D), GM_WIDTH ** -0.5),
        "w_proj_b": nrm((L, MLA_WIDTH, D), MLA_WIDTH ** -0.5),
        "w_proj_c": nrm((L, LRU_WIDTH, D), LRU_WIDTH ** -0.5),
        "w_out": nrm((L, D, D), D ** -0.5),
        "post_norm_g": gain((L, D)),
    }


def reference(x, pre_norm_g, w_in, gm_ln_g, gm_ln_b, gm_ws, gm_bs, mla_q_norm_g, mla_w_uq,
              mla_kv_norm_g, mla_w_ukv, lru_conv_w, lru_conv_b, lru_w_a, lru_b_a, lru_w_x,
              lru_b_x, lru_lambda, w_proj_a, w_proj_b, w_proj_c, w_out, post_norm_g):
    cuts = _split_points()
    for l in range(DEPTH):
        h = rmsnorm(x, pre_norm_g[l])
        proj = h @ w_in[l]
        (u, v, z_a, c_q, c_kv, k_rope, z_b, x_c, z_c,
         g_a, g_b, g_c) = jnp.split(proj, cuts, axis=-1)

        y_a = gmlp_spatial(u, v, gm_ln_g[l], gm_ln_b[l], gm_ws[l], gm_bs[l]) * jax.nn.silu(z_a)
        y_b = mla_attention(c_q, c_kv, k_rope, mla_q_norm_g[l], mla_w_uq[l],
                            mla_kv_norm_g[l], mla_w_ukv[l]) * jax.nn.silu(z_b)
        y_c = rg_lru(x_c, lru_conv_w[l], lru_conv_b[l], lru_w_a[l], lru_b_a[l],
                     lru_w_x[l], lru_b_x[l], lru_lambda[l]) * jax.nn.silu(z_c)

        merged = (jax.nn.sigmoid(g_a) * (y_a @ w_proj_a[l])
                  + jax.nn.sigmoid(g_b) * (y_b @ w_proj_b[l])
                  + jax.nn.sigmoid(g_c) * (y_c @ w_proj_c[l]))
        x = x + rmsnorm(merged @ w_out[l], post_norm_g[l])
    return x
```

```python
import functools
import math

import jax
import jax.numpy as jnp
from jax import lax
from jax.experimental import pallas as pl
from jax.experimental.pallas import tpu as pltpu

D_MODEL = 1024
BATCH = 8
SEQ = 2048
DEPTH = 2
TOKENS = BATCH * SEQ
CHUNK = 64
EPS = 1e-6

GM_WIDTH = 1024
GM_GROUPS = 4
GM_BLOCK = 128
GM_GW = GM_WIDTH // GM_GROUPS

MLA_HEADS = 8
MLA_NOPE = 128
MLA_ROPE = 64
MLA_VDIM = 128
MLA_QK_DIM = MLA_NOPE + MLA_ROPE
MLA_Q_RANK = 384
MLA_KV_RANK = 256
MLA_WIDTH = MLA_HEADS * MLA_VDIM
ROPE_THETA = 10000.0

LRU_WIDTH = 1280
LRU_BLOCKS = 16
LRU_BW = LRU_WIDTH // LRU_BLOCKS
LRU_C = 8.0
CONV_W = 4

IN_SIZES = (GM_WIDTH, GM_WIDTH, GM_WIDTH, MLA_Q_RANK, MLA_KV_RANK, MLA_ROPE, MLA_WIDTH,
            LRU_WIDTH, LRU_WIDTH, D_MODEL, D_MODEL, D_MODEL)

BM_SEG = 1024
(COL_U, COL_V, COL_ZA, COL_ZB, COL_GA, COL_GB) = range(6)
LATENT_W = MLA_Q_RANK + MLA_KV_RANK + 2 * MLA_ROPE
N_BM = 6 * BM_SEG + LATENT_W
COL_LATENT = (6 * BM_SEG) // LATENT_W
N_TM = 2 * LRU_WIDTH + D_MODEL

VMEM_LIMIT_BYTES = 56 * 1024 * 1024

PROJ_TM = 1024
PROJ_BM_TN = 768
PROJ_TM_TN = 512
LRU_TS = 64
PREP_TM = 512
ATT_TQ = 256
ATT_TK = 256
FINAL_TM = 512

_F32 = jnp.float32
_BF16 = jnp.bfloat16


def _params(*sem):
    return pltpu.CompilerParams(dimension_semantics=sem, vmem_limit_bytes=VMEM_LIMIT_BYTES)


def _sigmoid(x):
    return 1.0 / (1.0 + jnp.exp(-x))


def _silu(x):
    return x * _sigmoid(x)


def _proj_kernel(x_ref, g_ref, w_ref, o_ref, h_scr):
    @pl.when(pl.program_id(1) == 0)
    def _():
        x = x_ref[...]
        ms = jnp.mean(x * x, axis=-1, keepdims=True)
        h_scr[...] = (x * lax.rsqrt(ms + EPS) * g_ref[...]).astype(_BF16)

    o_ref[...] = jnp.dot(h_scr[...], w_ref[...], preferred_element_type=_F32).astype(o_ref.dtype)


def _proj(x2d, g, w, tn, time_major):
    n = w.shape[1]
    nj = n // tn
    tiles_per_seq = SEQ // PROJ_TM
    if time_major:
        out_shape = jax.ShapeDtypeStruct((SEQ, BATCH * n), _BF16)
        out_map = lambda i, j: (i % tiles_per_seq, (i // tiles_per_seq) * nj + j)
    else:
        out_shape = jax.ShapeDtypeStruct((TOKENS, n), _BF16)
        out_map = lambda i, j: (i, j)
    return pl.pallas_call(
        _proj_kernel,
        grid=(TOKENS // PROJ_TM, nj),
        in_specs=[
            pl.BlockSpec((PROJ_TM, D_MODEL), lambda i, j: (i, 0)),
            pl.BlockSpec((1, D_MODEL), lambda i, j: (0, 0)),
            pl.BlockSpec((D_MODEL, tn), lambda i, j: (0, j)),
        ],
        out_specs=pl.BlockSpec((PROJ_TM, tn), out_map),
        out_shape=out_shape,
        scratch_shapes=[pltpu.VMEM((PROJ_TM, D_MODEL), _BF16)],
        compiler_params=_params("arbitrary", "arbitrary"),
        name="proj_tm" if time_major else "proj_bm",
    )(x2d, g, w)


def _lru_kernel(p_ref, cw_ref, cb_ref, wa_ref, ba_ref, wx_ref, bx_ref, lam_ref, wp_ref,
                o_ref, xext, a_scr, b_scr, h_scr):
    rows = LRU_TS * BATCH
    halo = (CONV_W - 1) * BATCH

    @pl.when(pl.program_id(0) == 0)
    def _():
        xext[0:halo, :] = jnp.zeros((halo, LRU_WIDTH), _F32)
        h_scr[...] = jnp.zeros_like(h_scr)

    xext[halo:halo + rows, :] = p_ref[:, 0:LRU_WIDTH].astype(_F32)
    xc = cb_ref[...] + cw_ref[CONV_W - 1:CONV_W, :] * xext[halo:halo + rows, :]
    for k in range(CONV_W - 1):
        xc = xc + cw_ref[k:k + 1, :] * xext[k * BATCH:k * BATCH + rows, :]
    xext[0:halo, :] = xext[rows:rows + halo, :]

    xcb = xc.astype(_BF16)
    r = _sigmoid(jnp.dot(xcb, wa_ref[...], preferred_element_type=_F32) + ba_ref[...])
    i = _sigmoid(jnp.dot(xcb, wx_ref[...], preferred_element_type=_F32) + bx_ref[...])
    lam = lam_ref[...]
    softplus_neg_lam = jnp.maximum(-lam, 0.0) + jnp.log(1.0 + jnp.exp(-jnp.abs(lam)))
    a = jnp.exp((-LRU_C) * r * softplus_neg_lam)
    mult = jnp.sqrt(jnp.maximum(1.0 - a * a, 0.0))
    a_scr[...] = a
    b_scr[...] = mult * (i * xc)

    def step(s, h):
        off = pl.multiple_of(s * BATCH, BATCH)
        h = a_scr[pl.ds(off, BATCH), :] * h + b_scr[pl.ds(off, BATCH), :]
        b_scr[pl.ds(off, BATCH), :] = h
        return h

    h_scr[...] = lax.fori_loop(0, LRU_TS, step, h_scr[...], unroll=8)

    z = p_ref[:, LRU_WIDTH:2 * LRU_WIDTH].astype(_F32)
    y = (b_scr[...] * _silu(z)).astype(_BF16)
    g = p_ref[:, 2 * LRU_WIDTH:N_TM].astype(_F32)
    o_ref[...] = (_sigmoid(g) * jnp.dot(y, wp_ref[...], preferred_element_type=_F32)).astype(o_ref.dtype)


def _lru_mixer(p_tm, conv_w, conv_b, wa, ba, wx, bx, lam, wp):
    rows = LRU_TS * BATCH
    halo = (CONV_W - 1) * BATCH
    const = lambda t: (0, 0)
    vec = pl.BlockSpec((1, LRU_WIDTH), const)
    return pl.pallas_call(
        _lru_kernel,
        grid=(SEQ // LRU_TS,),
        in_specs=[
            pl.BlockSpec((rows, N_TM), lambda t: (t, 0)),
            pl.BlockSpec((CONV_W, LRU_WIDTH), const), vec,
            pl.BlockSpec((LRU_WIDTH, LRU_WIDTH), const), vec,
            pl.BlockSpec((LRU_WIDTH, LRU_WIDTH), const), vec,
            vec,
            pl.BlockSpec((LRU_WIDTH, D_MODEL), const),
        ],
        out_specs=pl.BlockSpec((rows, D_MODEL), lambda t: (t, 0)),
        out_shape=jax.ShapeDtypeStruct((TOKENS, D_MODEL), _BF16),
        scratch_shapes=[
            pltpu.VMEM((rows + halo, LRU_WIDTH), _F32),
            pltpu.VMEM((rows, LRU_WIDTH), _F32),
            pltpu.VMEM((rows, LRU_WIDTH), _F32),
            pltpu.VMEM((BATCH, LRU_WIDTH), _F32),
        ],
        compiler_params=_params("arbitrary"),
        name="lru_mixer",
    )(p_tm, conv_w, conv_b, wa, ba, wx, bx, lam, wp)


def _mla_prep_kernel(c_ref, cos_ref, sin_ref, csk_ref, qg_ref, kvg_ref, wq_ref, wkv_ref,
                     qn_ref, qr_ref, kn_ref, v_ref, kr_ref):
    scale = 1.0 / math.sqrt(MLA_QK_DIM)
    c = c_ref[...].astype(_F32)
    cq = c[:, 0:MLA_Q_RANK]
    ckv = c[:, MLA_Q_RANK:MLA_Q_RANK + MLA_KV_RANK]
    krk = c[:, MLA_Q_RANK + MLA_KV_RANK:LATENT_W]

    hq = cq * lax.rsqrt(jnp.mean(cq * cq, axis=-1, keepdims=True) + EPS) * qg_ref[...]
    q = jnp.dot(hq.astype(_BF16), wq_ref[...], preferred_element_type=_F32)
    nope_w = MLA_HEADS * MLA_NOPE
    rope_w = MLA_HEADS * MLA_ROPE
    qn_ref[...] = (q[:, 0:nope_w] * scale).astype(_BF16)
    cos = cos_ref[...]
    sin = sin_ref[...]
    for p in range(rope_w // 128):
        qr = q[:, nope_w + 128 * p:nope_w + 128 * (p + 1)]
        qrr = q[:, nope_w + rope_w + 128 * p:nope_w + rope_w + 128 * (p + 1)]
        qr_ref[:, 128 * p:128 * (p + 1)] = ((qr * cos + qrr * sin) * scale).astype(_BF16)

    hkv = ckv * lax.rsqrt(jnp.mean(ckv * ckv, axis=-1, keepdims=True) + EPS) * kvg_ref[...]
    kv = jnp.dot(hkv.astype(_BF16), wkv_ref[...], preferred_element_type=_F32)
    kn_ref[...] = kv[:, 0:nope_w].astype(_BF16)
    v_ref[...] = kv[:, nope_w:].astype(_BF16)

    t = krk * csk_ref[...]
    kf2 = t + pltpu.roll(t, MLA_ROPE, 1)
    lane = lax.broadcasted_iota(jnp.int32, kf2.shape, 1)
    kr_ref[:, 0:128] = jnp.where(lane < MLA_ROPE, kf2, 0.0).astype(_BF16)
    kr_ref[:, 128:256] = jnp.where(lane >= MLA_ROPE, kf2, 0.0).astype(_BF16)


def _mla_prep(p_bm, cos128, sin128, csk, qg, kvg, wq, wkv):
    tiles_per_seq = SEQ // PREP_TM
    const = lambda i: (0, 0)
    pos = lambda i: (i % tiles_per_seq, 0)
    row = lambda i: (i, 0)
    wide = MLA_HEADS * MLA_NOPE
    return pl.pallas_call(
        _mla_prep_kernel,
        grid=(TOKENS // PREP_TM,),
        in_specs=[
            pl.BlockSpec((PREP_TM, LATENT_W), lambda i: (i, COL_LATENT)),
            pl.BlockSpec((PREP_TM, 128), pos),
            pl.BlockSpec((PREP_TM, 128), pos),
            pl.BlockSpec((PREP_TM, 128), pos),
            pl.BlockSpec((1, MLA_Q_RANK), const),
            pl.BlockSpec((1, MLA_KV_RANK), const),
            pl.BlockSpec(wq.shape, const),
            pl.BlockSpec(wkv.shape, const),
        ],
        out_specs=[
            pl.BlockSpec((PREP_TM, wide), row),
            pl.BlockSpec((PREP_TM, MLA_HEADS * MLA_ROPE), row),
            pl.BlockSpec((PREP_TM, wide), row),
            pl.BlockSpec((PREP_TM, wide), row),
            pl.BlockSpec((PREP_TM, 256), row),
        ],
        out_shape=[
            jax.ShapeDtypeStruct((TOKENS, wide), _BF16),
            jax.ShapeDtypeStruct((TOKENS, MLA_HEADS * MLA_ROPE), _BF16),
            jax.ShapeDtypeStruct((TOKENS, wide), _BF16),
            jax.ShapeDtypeStruct((TOKENS, wide), _BF16),
            jax.ShapeDtypeStruct((TOKENS, 256), _BF16),
        ],
        compiler_params=_params("arbitrary"),
        name="mla_prep",
    )(p_bm, cos128, sin128, csk, qg, kvg, wq, wkv)


def _attn_kernel(qn_ref, qr_ref, kn_ref, v_ref, kr_ref, zb_ref, gb_ref, wp_ref, o_ref, y_scr):
    qi = pl.program_id(1)
    row_chunk = lax.broadcasted_iota(jnp.int32, (ATT_TQ, ATT_TK), 0) // CHUNK
    col_chunk = lax.broadcasted_iota(jnp.int32, (ATT_TQ, ATT_TK), 1) // CHUNK
    diag_mask = col_chunk <= row_chunk

    for h in range(MLA_HEADS):
        hs = slice(MLA_NOPE * h, MLA_NOPE * (h + 1))
        q = jnp.concatenate([qn_ref[:, hs], qr_ref[:, 128 * (h // 2):128 * (h // 2 + 1)]], axis=1)

        def tile(j, carry, masked, hs=hs, q=q, h=h):
            m, l, acc = carry
            off = pl.multiple_of(j * ATT_TK, ATT_TK)
            k = jnp.concatenate([kn_ref[pl.ds(off, ATT_TK), hs],
                                 kr_ref[pl.ds(off, ATT_TK), 128 * (h % 2):128 * (h % 2 + 1)]], axis=1)
            s = lax.dot_general(q, k, (((1,), (1,)), ((), ())), preferred_element_type=_F32)
            if masked:
                s = jnp.where(diag_mask, s, -1e30)
            m_new = jnp.maximum(m, jnp.max(s, axis=1, keepdims=True))
            alpha = jnp.exp(m - m_new)
            p = jnp.exp(s - m_new)
            l = alpha * l + jnp.sum(p, axis=1, keepdims=True)
            acc = alpha * acc + jnp.dot(p.astype(_BF16), v_ref[pl.ds(off, ATT_TK), hs],
                                        preferred_element_type=_F32)
            return m_new, l, acc

        init = (jnp.full((ATT_TQ, 1), -1e30, _F32), jnp.zeros((ATT_TQ, 1), _F32),
                jnp.zeros((ATT_TQ, MLA_VDIM), _F32))
        carry = lax.fori_loop(0, qi, functools.partial(tile, masked=False), init)
        _, l, acc = tile(qi, carry, True)
        o = acc * (1.0 / l)
        y_scr[:, hs] = (o * _silu(zb_ref[:, hs].astype(_F32))).astype(_BF16)

    cb = jnp.dot(y_scr[...], wp_ref[...], preferred_element_type=_F32)
    o_ref[...] = (_sigmoid(gb_ref[...].astype(_F32)) * cb).astype(o_ref.dtype)


def _attention(qn, qr, kn, v, kr, p_bm, wp):
    nq = SEQ // ATT_TQ
    wide = MLA_HEADS * MLA_NOPE
    qmap = lambda b, i: (b * nq + i, 0)
    kmap = lambda b, i: (b, 0)
    return pl.pallas_call(
        _attn_kernel,
        grid=(BATCH, nq),
        in_specs=[
            pl.BlockSpec((ATT_TQ, wide), qmap),
            pl.BlockSpec((ATT_TQ, MLA_HEADS * MLA_ROPE), qmap),
            pl.BlockSpec((SEQ, wide), kmap),
            pl.BlockSpec((SEQ, wide), kmap),
            pl.BlockSpec((SEQ, 256), kmap),
            pl.BlockSpec((ATT_TQ, BM_SEG), lambda b, i: (b * nq + i, COL_ZB)),
            pl.BlockSpec((ATT_TQ, BM_SEG), lambda b, i: (b * nq + i, COL_GB)),
            pl.BlockSpec((MLA_WIDTH, D_MODEL), lambda b, i: (0, 0)),
        ],
        out_specs=pl.BlockSpec((ATT_TQ, D_MODEL), qmap),
        out_shape=jax.ShapeDtypeStruct((TOKENS, D_MODEL), _BF16),
        scratch_shapes=[pltpu.VMEM((ATT_TQ, MLA_WIDTH), _BF16)],
        compiler_params=_params("arbitrary", "arbitrary"),
        name="mla_attn",
    )(qn, qr, kn, v, kr, p_bm, p_bm, wp)


def _final_kernel(x_ref, u_ref, v_ref, za_ref, ga_ref, cb_ref, cc_ref, lng_ref, lnb_ref,
                  ws_ref, bs_ref, wpa_ref, wo_ref, pg_ref, o_ref, y_scr):
    idx_r = lax.broadcasted_iota(jnp.int32, (GM_BLOCK, GM_BLOCK), 0) // CHUNK
    idx_c = lax.broadcasted_iota(jnp.int32, (GM_BLOCK, GM_BLOCK), 1) // CHUNK
    causal = idx_c <= idx_r

    v = v_ref[...].astype(_F32)
    mu = jnp.mean(v, axis=-1, keepdims=True)
    vc = v - mu
    var = jnp.mean(vc * vc, axis=-1, keepdims=True)
    vln = (vc * lax.rsqrt(var + EPS) * lng_ref[...] + lnb_ref[...]).astype(_BF16)

    for g in range(GM_GROUPS):
        ws = jnp.where(causal, ws_ref[g], 0.0).astype(_BF16)
        cs = slice(GM_GW * g, GM_GW * (g + 1))
        for r in range(FINAL_TM // GM_BLOCK):
            rs = slice(GM_BLOCK * r, GM_BLOCK * (r + 1))
            sv = jnp.dot(ws, vln[rs, cs], preferred_element_type=_F32) + bs_ref[g]
            y = u_ref[rs, cs].astype(_F32) * sv * _silu(za_ref[rs, cs].astype(_F32))
            y_scr[rs, cs] = y.astype(_BF16)

    ca = _sigmoid(ga_ref[...].astype(_F32)) * jnp.dot(y_scr[...], wpa_ref[...],
                                                      preferred_element_type=_F32)
    merged = ca + cb_ref[...].astype(_F32) + cc_ref[...].astype(_F32)
    o = jnp.dot(merged.astype(_BF16), wo_ref[...], preferred_element_type=_F32)
    o = o * lax.rsqrt(jnp.mean(o * o, axis=-1, keepdims=True) + EPS) * pg_ref[...]
    o_ref[...] = x_ref[...] + o


def _final(x2d, p_bm, cb, cc_tm, lng, lnb, ws, bs_col, wpa, wo, pg):
    tiles_per_seq = SEQ // FINAL_TM
    const2 = lambda i: (0, 0)
    const3 = lambda i: (0, 0, 0)
    seg = lambda c: pl.BlockSpec((FINAL_TM, BM_SEG), lambda i: (i, c))
    vec = pl.BlockSpec((1, D_MODEL), const2)
    return pl.pallas_call(
        _final_kernel,
        grid=(TOKENS // FINAL_TM,),
        in_specs=[
            pl.BlockSpec((FINAL_TM, D_MODEL), lambda i: (i, 0)),
            seg(COL_U), seg(COL_V), seg(COL_ZA), seg(COL_GA),
            pl.BlockSpec((FINAL_TM, D_MODEL), lambda i: (i, 0)),
            pl.BlockSpec((FINAL_TM, D_MODEL), lambda i: (i % tiles_per_seq, i // tiles_per_seq)),
            vec, vec,
            pl.BlockSpec((GM_GROUPS, GM_BLOCK, GM_BLOCK), const3),
            pl.BlockSpec((GM_GROUPS, GM_BLOCK, 1), const3),
            pl.BlockSpec((GM_WIDTH, D_MODEL), const2),
            pl.BlockSpec((D_MODEL, D_MODEL), const2),
            vec,
        ],
        out_specs=pl.BlockSpec((FINAL_TM, D_MODEL), lambda i: (i, 0)),
        out_shape=jax.ShapeDtypeStruct((TOKENS, D_MODEL), _F32),
        scratch_shapes=[pltpu.VMEM((FINAL_TM, GM_WIDTH), _BF16)],
        compiler_params=_params("arbitrary"),
        name="gmlp_merge_out",
    )(x2d, p_bm, p_bm, p_bm, p_bm, cb, cc_tm, lng, lnb, ws, bs_col, wpa, wo, pg)


def _layer_weights(l, w_in, mla_w_uq, mla_w_ukv, lru_w_a, lru_w_x, w_proj_a, w_proj_b, w_proj_c,
                   w_out):
    cuts = [0]
    for s in IN_SIZES:
        cuts.append(cuts[-1] + s)
    seg = lambda k: w_in[l][:, cuts[k]:cuts[k + 1]]
    (u, v, z_a, c_q, c_kv, k_rope, z_b, x_c, z_c, g_a, g_b, g_c) = [seg(k) for k in range(12)]
    half = MLA_ROPE // 2
    k_rope_rot = jnp.concatenate([-k_rope[:, half:], k_rope[:, :half]], axis=1)
    w_bm = jnp.concatenate([u, v, z_a, z_b, g_a, g_b, c_q, c_kv, k_rope, k_rope_rot],
                           axis=1).astype(_BF16)
    w_tm = jnp.concatenate([x_c, z_c, g_c], axis=1).astype(_BF16)

    wq = mla_w_uq[l].reshape(MLA_Q_RANK, MLA_HEADS, MLA_QK_DIM)
    wq_nope = wq[:, :, :MLA_NOPE].reshape(MLA_Q_RANK, -1)
    wq_rope = wq[:, :, MLA_NOPE:].reshape(MLA_Q_RANK, -1)
    wq_rope_rot = jnp.concatenate([-wq[:, :, MLA_NOPE + half:], wq[:, :, MLA_NOPE:MLA_NOPE + half]],
                                  axis=2).reshape(MLA_Q_RANK, -1)
    wq_all = jnp.concatenate([wq_nope, wq_rope, wq_rope_rot], axis=1).astype(_BF16)

    wkv = mla_w_ukv[l].reshape(MLA_KV_RANK, MLA_HEADS, MLA_NOPE + MLA_VDIM)
    wkv_all = jnp.concatenate([wkv[:, :, :MLA_NOPE].reshape(MLA_KV_RANK, -1),
                               wkv[:, :, MLA_NOPE:].reshape(MLA_KV_RANK, -1)], axis=1).astype(_BF16)

    def block_diag(w):
        eye = jnp.eye(LRU_BLOCKS, dtype=w.dtype)
        dense = jnp.einsum('hij,hk->hikj', w, eye)
        return dense.reshape(LRU_WIDTH, LRU_WIDTH).astype(_BF16)

    return dict(w_bm=w_bm, w_tm=w_tm, wq=wq_all, wkv=wkv_all,
                wa=block_diag(lru_w_a[l]), wx=block_diag(lru_w_x[l]),
                wpa=w_proj_a[l].astype(_BF16), wpb=w_proj_b[l].astype(_BF16),
                wpc=w_proj_c[l].astype(_BF16), wo=w_out[l].astype(_BF16))


def _rope_tables():
    pos = jnp.arange(SEQ, dtype=_F32)
    inv_freq = ROPE_THETA ** (-jnp.arange(0, MLA_ROPE, 2, dtype=_F32) / MLA_ROPE)
    ang = pos[:, None] * inv_freq[None, :]
    cos = jnp.cos(ang)
    sin = jnp.sin(ang)
    cos128 = jnp.tile(cos, (1, 4))
    sin128 = jnp.tile(sin, (1, 4))
    csk = jnp.concatenate([cos, cos, sin, sin], axis=1)
    return cos128, sin128, csk


def kernel(x, pre_norm_g, w_in, gm_ln_g, gm_ln_b, gm_ws, gm_bs, mla_q_norm_g, mla_w_uq,
           mla_kv_norm_g, mla_w_ukv, lru_conv_w, lru_conv_b, lru_w_a, lru_b_a, lru_w_x,
           lru_b_x, lru_lambda, w_proj_a, w_proj_b, w_proj_c, w_out, post_norm_g):
    cos128, sin128, csk = _rope_tables()
    x2d = x.reshape(TOKENS, D_MODEL)
    row = lambda a: a.reshape(1, -1)
    for l in range(DEPTH):
        w = _layer_weights(l, w_in, mla_w_uq, mla_w_ukv, lru_w_a, lru_w_x, w_proj_a, w_proj_b,
                           w_proj_c, w_out)
        g_pre = row(pre_norm_g[l])
        p_bm = _proj(x2d, g_pre, w["w_bm"], PROJ_BM_TN, time_major=False)
        p_tm = _proj(x2d, g_pre, w["w_tm"], PROJ_TM_TN, time_major=True)

        cc = _lru_mixer(p_tm.reshape(TOKENS, N_TM), lru_conv_w[l], row(lru_conv_b[l]),
                        w["wa"], row(lru_b_a[l]), w["wx"], row(lru_b_x[l]), row(lru_lambda[l]),
                        w["wpc"])
        cc_tm = cc.reshape(SEQ, BATCH * D_MODEL)

        qn, qr, kn, v, kr = _mla_prep(p_bm, cos128, sin128, csk, row(mla_q_norm_g[l]),
                                      row(mla_kv_norm_g[l]), w["wq"], w["wkv"])
        cb = _attention(qn, qr, kn, v, kr, p_bm, w["wpb"])

        x2d = _final(x2d, p_bm, cb, cc_tm, row(gm_ln_g[l]), row(gm_ln_b[l]), gm_ws[l],
                     gm_bs[l][:, :, None], w["wpa"], w["wo"], row(post_norm_g[l]))
    return x2d.reshape(BATCH, SEQ, D_MODEL)
```

```python
import functools
import math

import jax
import jax.numpy as jnp
from jax import lax
from jax.experimental import pallas as pl
from jax.experimental.pallas import tpu as pltpu

D_MODEL = 1024
BATCH = 8
SEQ = 2048
DEPTH = 2
TOKENS = BATCH * SEQ
CHUNK = 64
EPS = 1e-6

GM_WIDTH = 1024
GM_GROUPS = 4
GM_BLOCK = 128
GM_GW = GM_WIDTH // GM_GROUPS

MLA_HEADS = 8
MLA_NOPE = 128
MLA_ROPE = 64
MLA_VDIM = 128
MLA_QK_DIM = MLA_NOPE + MLA_ROPE
MLA_Q_RANK = 384
MLA_KV_RANK = 256
MLA_WIDTH = MLA_HEADS * MLA_VDIM
ROPE_THETA = 10000.0

LRU_WIDTH = 1280
LRU_BLOCKS = 16
LRU_BW = LRU_WIDTH // LRU_BLOCKS
LRU_C = 8.0
CONV_W = 4

IN_SIZES = (GM_WIDTH, GM_WIDTH, GM_WIDTH, MLA_Q_RANK, MLA_KV_RANK, MLA_ROPE, MLA_WIDTH,
            LRU_WIDTH, LRU_WIDTH, D_MODEL, D_MODEL, D_MODEL)

BM_SEG = 1024
(COL_U, COL_V, COL_ZA, COL_ZB, COL_GA, COL_GB) = range(6)
LATENT_W = MLA_Q_RANK + MLA_KV_RANK + 2 * MLA_ROPE
N_BM = 6 * BM_SEG + LATENT_W
COL_LATENT = (6 * BM_SEG) // LATENT_W
N_TM = 2 * LRU_WIDTH + D_MODEL

VMEM_LIMIT_BYTES = 56 * 1024 * 1024

PROJ_TM = 1024
PROJ_BM_TN = 768
PROJ_TM_TN = 512
LRU_TS = 64
PREP_TM = 512
ATT_TQ = 256
ATT_TK = 256
ATT_D = 256
FINAL_TM = 512

_F32 = jnp.float32
_BF16 = jnp.bfloat16


def _params(*sem):
    return pltpu.CompilerParams(dimension_semantics=sem, vmem_limit_bytes=VMEM_LIMIT_BYTES)


def _sigmoid(x):
    return 1.0 / (1.0 + jnp.exp(-x))


def _silu(x):
    return x * _sigmoid(x)


def _proj_kernel(x_ref, g_ref, w_ref, o_ref, h_scr):
    @pl.when(pl.program_id(1) == 0)
    def _():
        x = x_ref[...]
        ms = jnp.mean(x * x, axis=-1, keepdims=True)
        h_scr[...] = (x * lax.rsqrt(ms + EPS) * g_ref[...]).astype(_BF16)

    o_ref[...] = jnp.dot(h_scr[...], w_ref[...], preferred_element_type=_F32).astype(o_ref.dtype)


def _proj(x2d, g, w, tn, time_major):
    n = w.shape[1]
    nj = n // tn
    tiles_per_seq = SEQ // PROJ_TM
    if time_major:
        out_shape = jax.ShapeDtypeStruct((SEQ, BATCH * n), _BF16)
        out_map = lambda i, j: (i % tiles_per_seq, (i // tiles_per_seq) * nj + j)
    else:
        out_shape = jax.ShapeDtypeStruct((TOKENS, n), _BF16)
        out_map = lambda i, j: (i, j)
    return pl.pallas_call(
        _proj_kernel,
        grid=(TOKENS // PROJ_TM, nj),
        in_specs=[
            pl.BlockSpec((PROJ_TM, D_MODEL), lambda i, j: (i, 0)),
            pl.BlockSpec((1, D_MODEL), lambda i, j: (0, 0)),
            pl.BlockSpec((D_MODEL, tn), lambda i, j: (0, j)),
        ],
        out_specs=pl.BlockSpec((PROJ_TM, tn), out_map),
        out_shape=out_shape,
        scratch_shapes=[pltpu.VMEM((PROJ_TM, D_MODEL), _BF16)],
        compiler_params=_params("arbitrary", "arbitrary"),
        name="proj_tm" if time_major else "proj_bm",
    )(x2d, g, w)


def _lru_kernel(p_ref, cw_ref, cb_ref, wa_ref, ba_ref, wx_ref, bx_ref, lam_ref, wp_ref,
                o_ref, xext, a_scr, b_scr, h_scr):
    rows = LRU_TS * BATCH
    halo = (CONV_W - 1) * BATCH

    @pl.when(pl.program_id(0) == 0)
    def _():
        xext[0:halo, :] = jnp.zeros((halo, LRU_WIDTH), _F32)
        h_scr[...] = jnp.zeros_like(h_scr)

    xext[halo:halo + rows, :] = p_ref[:, 0:LRU_WIDTH].astype(_F32)
    xc = cb_ref[...] + cw_ref[CONV_W - 1:CONV_W, :] * xext[halo:halo + rows, :]
    for k in range(CONV_W - 1):
        xc = xc + cw_ref[k:k + 1, :] * xext[k * BATCH:k * BATCH + rows, :]
    xext[0:halo, :] = xext[rows:rows + halo, :]

    xcb = xc.astype(_BF16)
    r = _sigmoid(jnp.dot(xcb, wa_ref[...], preferred_element_type=_F32) + ba_ref[...])
    i = _sigmoid(jnp.dot(xcb, wx_ref[...], preferred_element_type=_F32) + bx_ref[...])
    lam = lam_ref[...]
    softplus_neg_lam = jnp.maximum(-lam, 0.0) + jnp.log(1.0 + jnp.exp(-jnp.abs(lam)))
    a = jnp.exp((-LRU_C) * r * softplus_neg_lam)
    mult = jnp.sqrt(jnp.maximum(1.0 - a * a, 0.0))
    a_scr[...] = a
    b_scr[...] = mult * (i * xc)

    def step(s, h):
        off = pl.multiple_of(s * BATCH, BATCH)
        h = a_scr[pl.ds(off, BATCH), :] * h + b_scr[pl.ds(off, BATCH), :]
        b_scr[pl.ds(off, BATCH), :] = h
        return h

    h_scr[...] = lax.fori_loop(0, LRU_TS, step, h_scr[...], unroll=8)

    z = p_ref[:, LRU_WIDTH:2 * LRU_WIDTH].astype(_F32)
    y = (b_scr[...] * _silu(z)).astype(_BF16)
    g = p_ref[:, 2 * LRU_WIDTH:N_TM].astype(_F32)
    o_ref[...] = (_sigmoid(g) * jnp.dot(y, wp_ref[...], preferred_element_type=_F32)).astype(o_ref.dtype)


def _lru_mixer(p_tm, conv_w, conv_b, wa, ba, wx, bx, lam, wp):
    rows = LRU_TS * BATCH
    halo = (CONV_W - 1) * BATCH
    const = lambda t: (0, 0)
    vec = pl.BlockSpec((1, LRU_WIDTH), const)
    return pl.pallas_call(
        _lru_kernel,
        grid=(SEQ // LRU_TS,),
        in_specs=[
            pl.BlockSpec((rows, N_TM), lambda t: (t, 0)),
            pl.BlockSpec((CONV_W, LRU_WIDTH), const), vec,
            pl.BlockSpec((LRU_WIDTH, LRU_WIDTH), const), vec,
            pl.BlockSpec((LRU_WIDTH, LRU_WIDTH), const), vec,
            vec,
            pl.BlockSpec((LRU_WIDTH, D_MODEL), const),
        ],
        out_specs=pl.BlockSpec((rows, D_MODEL), lambda t: (t, 0)),
        out_shape=jax.ShapeDtypeStruct((TOKENS, D_MODEL), _BF16),
        scratch_shapes=[
            pltpu.VMEM((rows + halo, LRU_WIDTH), _F32),
            pltpu.VMEM((rows, LRU_WIDTH), _F32),
            pltpu.VMEM((rows, LRU_WIDTH), _F32),
            pltpu.VMEM((BATCH, LRU_WIDTH), _F32),
        ],
        compiler_params=_params("arbitrary"),
        name="lru_mixer",
    )(p_tm, conv_w, conv_b, wa, ba, wx, bx, lam, wp)


def _mla_prep_kernel(c_ref, cost_ref, sint_ref, csk_ref, qg_ref, kvg_ref, wqt_ref, wkn_ref, wvt_ref,
                     qt_ref, k_ref, vt_ref):
    qscale = math.log2(math.e) / math.sqrt(MLA_QK_DIM)
    c = c_ref[...].astype(_F32)
    cq = c[:, 0:MLA_Q_RANK]
    ckv = c[:, MLA_Q_RANK:MLA_Q_RANK + MLA_KV_RANK]
    krk = c[:, MLA_Q_RANK + MLA_KV_RANK:LATENT_W]

    hq = (cq * lax.rsqrt(jnp.mean(cq * cq, axis=-1, keepdims=True) + EPS)
          * qg_ref[...]).astype(_BF16)
    qt = lax.dot_general(wqt_ref[...], hq, (((1,), (1,)), ((), ())), preferred_element_type=_F32)
    nope_w = MLA_HEADS * MLA_NOPE
    rope_w = MLA_HEADS * MLA_ROPE
    cost = cost_ref[...]
    sint = sint_ref[...]
    for p in range(rope_w // 128):
        qr = qt[nope_w + 128 * p:nope_w + 128 * (p + 1), :]
        qrr = qt[nope_w + rope_w + 128 * p:nope_w + rope_w + 128 * (p + 1), :]
        q_pair = ((qr * cost + qrr * sint) * qscale).astype(_BF16)
        for h in (2 * p, 2 * p + 1):
            qt_ref[ATT_D * h:ATT_D * h + MLA_NOPE, :] = (
                qt[MLA_NOPE * h:MLA_NOPE * (h + 1), :] * qscale).astype(_BF16)
            qt_ref[ATT_D * h + MLA_NOPE:ATT_D * (h + 1), :] = q_pair

    hkv = (ckv * lax.rsqrt(jnp.mean(ckv * ckv, axis=-1, keepdims=True) + EPS)
           * kvg_ref[...]).astype(_BF16)
    kn = jnp.dot(hkv, wkn_ref[...], preferred_element_type=_F32)
    vt_ref[...] = lax.dot_general(wvt_ref[...], hkv, (((1,), (1,)), ((), ())),
                                  preferred_element_type=_F32).astype(_BF16)

    t = krk * csk_ref[...]
    kf2 = t + pltpu.roll(t, MLA_ROPE, 1)
    lane = lax.broadcasted_iota(jnp.int32, kf2.shape, 1)
    kr_even = jnp.where(lane < MLA_ROPE, kf2, 0.0).astype(_BF16)
    kr_odd = jnp.where(lane >= MLA_ROPE, kf2, 0.0).astype(_BF16)
    for h in range(MLA_HEADS):
        k_ref[:, ATT_D * h:ATT_D * h + MLA_NOPE] = kn[:, MLA_NOPE * h:MLA_NOPE * (h + 1)].astype(_BF16)
        k_ref[:, ATT_D * h + MLA_NOPE:ATT_D * (h + 1)] = kr_even if h % 2 == 0 else kr_odd


def _mla_prep(p_bm, cos128t, sin128t, csk, qg, kvg, wqt, wkn, wvt):
    tiles_per_seq = SEQ // PREP_TM
    const = lambda i: (0, 0)
    pos = lambda i: (i % tiles_per_seq, 0)
    pos_t = lambda i: (0, i % tiles_per_seq)
    return pl.pallas_call(
        _mla_prep_kernel,
        grid=(TOKENS // PREP_TM,),
        in_specs=[
            pl.BlockSpec((PREP_TM, LATENT_W), lambda i: (i, COL_LATENT)),
            pl.BlockSpec((128, PREP_TM), pos_t),
            pl.BlockSpec((128, PREP_TM), pos_t),
            pl.BlockSpec((PREP_TM, 128), pos),
            pl.BlockSpec((1, MLA_Q_RANK), const),
            pl.BlockSpec((1, MLA_KV_RANK), const),
            pl.BlockSpec(wqt.shape, const),
            pl.BlockSpec(wkn.shape, const),
            pl.BlockSpec(wvt.shape, const),
        ],
        out_specs=[
            pl.BlockSpec((MLA_HEADS * ATT_D, PREP_TM), lambda i: (0, i)),
            pl.BlockSpec((PREP_TM, MLA_HEADS * ATT_D), lambda i: (i, 0)),
            pl.BlockSpec((MLA_WIDTH, PREP_TM), lambda i: (0, i)),
        ],
        out_shape=[
            jax.ShapeDtypeStruct((MLA_HEADS * ATT_D, TOKENS), _BF16),
            jax.ShapeDtypeStruct((TOKENS, MLA_HEADS * ATT_D), _BF16),
            jax.ShapeDtypeStruct((MLA_WIDTH, TOKENS), _BF16),
        ],
        compiler_params=_params("arbitrary"),
        name="mla_prep",
    )(p_bm, cos128t, sin128t, csk, qg, kvg, wqt, wkn, wvt)


def _attn_kernel(qt_ref, k_ref, vt_ref, zb_ref, gb_ref, wp_ref, o_ref,
                 st_scr, mt_scr, m_scr, l_scr, acc_scr, y_scr):
    qi = pl.program_id(1)
    key_chunk = lax.broadcasted_iota(jnp.int32, (ATT_TK, ATT_TQ), 0) // CHUNK
    query_chunk = lax.broadcasted_iota(jnp.int32, (ATT_TK, ATT_TQ), 1) // CHUNK
    diag_mask = key_chunk <= query_chunk

    m_scr[...] = jnp.full(m_scr.shape, -1e30, _F32)
    l_scr[...] = jnp.zeros(l_scr.shape, _F32)
    acc_scr[...] = jnp.zeros(acc_scr.shape, _F32)

    def scores(j, h):
        off = pl.multiple_of(j * ATT_TK, ATT_TK)
        ds_ = slice(ATT_D * h, ATT_D * (h + 1))
        st = jnp.dot(k_ref[pl.ds(off, ATT_TK), ds_], qt_ref[ds_, :], preferred_element_type=_F32)
        st_scr[h] = st
        mt_scr[h] = jnp.max(st, axis=0, keepdims=True)

    def accumulate(j, h, masked):
        off = pl.multiple_of(j * ATT_TK, ATT_TK)
        st = st_scr[h]
        if masked:
            st = jnp.where(diag_mask, st, -1e30)
            mt = jnp.max(st, axis=0, keepdims=True)
        else:
            mt = mt_scr[h]
        m_prev = m_scr[h]
        m_new = jnp.maximum(m_prev, mt)
        alpha = jnp.exp2(m_prev - m_new)
        p = jnp.exp2(st - m_new)
        l_scr[h] = alpha * l_scr[h] + jnp.sum(p, axis=0, keepdims=True)
        vt = vt_ref[MLA_VDIM * h:MLA_VDIM * (h + 1), pl.ds(off, ATT_TK)]
        acc_scr[h] = alpha * acc_scr[h] + jnp.dot(vt, p.astype(_BF16), preferred_element_type=_F32)
        m_scr[h] = m_new

    for h in range(MLA_HEADS):
        scores(0, h)

    def body(j, carry):
        for h in range(MLA_HEADS):
            accumulate(j, h, False)
            scores(j + 1, h)
        return carry

    lax.fori_loop(0, qi, body, 0)
    for h in range(MLA_HEADS):
        accumulate(qi, h, True)

    for h in range(MLA_HEADS):
        hs = slice(MLA_VDIM * h, MLA_VDIM * (h + 1))
        o = (acc_scr[h] * (1.0 / l_scr[h])).T
        y_scr[:, hs] = (o * _silu(zb_ref[:, hs].astype(_F32))).astype(_BF16)

    cb = jnp.dot(y_scr[...], wp_ref[...], preferred_element_type=_F32)
    o_ref[...] = (_sigmoid(gb_ref[...].astype(_F32)) * cb).astype(o_ref.dtype)


def _attention(qt, k, vt, p_bm, wp):
    nq = SEQ // ATT_TQ
    qmap = lambda b, i: (b * nq + i, 0)
    return pl.pallas_call(
        _attn_kernel,
        grid=(BATCH, nq),
        in_specs=[
            pl.BlockSpec((MLA_HEADS * ATT_D, ATT_TQ), lambda b, i: (0, b * nq + i)),
            pl.BlockSpec((SEQ, MLA_HEADS * ATT_D), lambda b, i: (b, 0)),
            pl.BlockSpec((MLA_WIDTH, SEQ), lambda b, i: (0, b)),
            pl.BlockSpec((ATT_TQ, BM_SEG), lambda b, i: (b * nq + i, COL_ZB)),
            pl.BlockSpec((ATT_TQ, BM_SEG), lambda b, i: (b * nq + i, COL_GB)),
            pl.BlockSpec((MLA_WIDTH, D_MODEL), lambda b, i: (0, 0)),
        ],
        out_specs=pl.BlockSpec((ATT_TQ, D_MODEL), qmap),
        out_shape=jax.ShapeDtypeStruct((TOKENS, D_MODEL), _BF16),
        scratch_shapes=[
            pltpu.VMEM((MLA_HEADS, ATT_TK, ATT_TQ), _F32),
            pltpu.VMEM((MLA_HEADS, 1, ATT_TQ), _F32),
            pltpu.VMEM((MLA_HEADS, 1, ATT_TQ), _F32),
            pltpu.VMEM((MLA_HEADS, 1, ATT_TQ), _F32),
            pltpu.VMEM((MLA_HEADS, MLA_VDIM, ATT_TQ), _F32),
            pltpu.VMEM((ATT_TQ, MLA_WIDTH), _BF16),
        ],
        compiler_params=_params("arbitrary", "arbitrary"),
        name="mla_attn",
    )(qt, k, vt, p_bm, p_bm, wp)


def _final_kernel(x_ref, u_ref, v_ref, za_ref, ga_ref, cb_ref, cc_ref, lng_ref, lnb_ref,
                  ws_ref, bs_ref, wpa_ref, wo_ref, pg_ref, o_ref, y_scr):
    idx_r = lax.broadcasted_iota(jnp.int32, (GM_BLOCK, GM_BLOCK), 0) // CHUNK
    idx_c = lax.broadcasted_iota(jnp.int32, (GM_BLOCK, GM_BLOCK), 1) // CHUNK
    causal = idx_c <= idx_r

    v = v_ref[...].astype(_F32)
    mu = jnp.mean(v, axis=-1, keepdims=True)
    vc = v - mu
    var = jnp.mean(vc * vc, axis=-1, keepdims=True)
    vln = (vc * lax.rsqrt(var + EPS) * lng_ref[...] + lnb_ref[...]).astype(_BF16)

    for g in range(GM_GROUPS):
        ws = jnp.where(causal, ws_ref[g], 0.0).astype(_BF16)
        cs = slice(GM_GW * g, GM_GW * (g + 1))
        for r in range(FINAL_TM // GM_BLOCK):
            rs = slice(GM_BLOCK * r, GM_BLOCK * (r + 1))
            sv = jnp.dot(ws, vln[rs, cs], preferred_element_type=_F32) + bs_ref[g]
            y = u_ref[rs, cs].astype(_F32) * sv * _silu(za_ref[rs, cs].astype(_F32))
            y_scr[rs, cs] = y.astype(_BF16)

    ca = _sigmoid(ga_ref[...].astype(_F32)) * jnp.dot(y_scr[...], wpa_ref[...],
                                                      preferred_element_type=_F32)
    merged = ca + cb_ref[...].astype(_F32) + cc_ref[...].astype(_F32)
    o = jnp.dot(merged.astype(_BF16), wo_ref[...], preferred_element_type=_F32)
    o = o * lax.rsqrt(jnp.mean(o * o, axis=-1, keepdims=True) + EPS) * pg_ref[...]
    o_ref[...] = x_ref[...] + o


def _final(x2d, p_bm, cb, cc_tm, lng, lnb, ws, bs_col, wpa, wo, pg):
    tiles_per_seq = SEQ // FINAL_TM
    const2 = lambda i: (0, 0)
    const3 = lambda i: (0, 0, 0)
    seg = lambda c: pl.BlockSpec((FINAL_TM, BM_SEG), lambda i: (i, c))
    vec = pl.BlockSpec((1, D_MODEL), const2)
    return pl.pallas_call(
        _final_kernel,
        grid=(TOKENS // FINAL_TM,),
        in_specs=[
            pl.BlockSpec((FINAL_TM, D_MODEL), lambda i: (i, 0)),
            seg(COL_U), seg(COL_V), seg(COL_ZA), seg(COL_GA),
            pl.BlockSpec((FINAL_TM, D_MODEL), lambda i: (i, 0)),
            pl.BlockSpec((FINAL_TM, D_MODEL), lambda i: (i % tiles_per_seq, i // tiles_per_seq)),
            vec, vec,
            pl.BlockSpec((GM_GROUPS, GM_BLOCK, GM_BLOCK), const3),
            pl.BlockSpec((GM_GROUPS, GM_BLOCK, 1), const3),
            pl.BlockSpec((GM_WIDTH, D_MODEL), const2),
            pl.BlockSpec((D_MODEL, D_MODEL), const2),
            vec,
        ],
        out_specs=pl.BlockSpec((FINAL_TM, D_MODEL), lambda i: (i, 0)),
        out_shape=jax.ShapeDtypeStruct((TOKENS, D_MODEL), _F32),
        scratch_shapes=[pltpu.VMEM((FINAL_TM, GM_WIDTH), _BF16)],
        compiler_params=_params("arbitrary"),
        name="gmlp_merge_out",
    )(x2d, p_bm, p_bm, p_bm, p_bm, cb, cc_tm, lng, lnb, ws, bs_col, wpa, wo, pg)


def _layer_weights(l, w_in, mla_w_uq, mla_w_ukv, lru_w_a, lru_w_x, w_proj_a, w_proj_b, w_proj_c,
                   w_out):
    cuts = [0]
    for s in IN_SIZES:
        cuts.append(cuts[-1] + s)
    seg = lambda k: w_in[l][:, cuts[k]:cuts[k + 1]]
    (u, v, z_a, c_q, c_kv, k_rope, z_b, x_c, z_c, g_a, g_b, g_c) = [seg(k) for k in range(12)]
    half = MLA_ROPE // 2
    k_rope_rot = jnp.concatenate([-k_rope[:, half:], k_rope[:, :half]], axis=1)
    w_bm = jnp.concatenate([u, v, z_a, z_b, g_a, g_b, c_q, c_kv, k_rope, k_rope_rot],
                           axis=1).astype(_BF16)
    w_tm = jnp.concatenate([x_c, z_c, g_c], axis=1).astype(_BF16)

    wq = mla_w_uq[l].reshape(MLA_Q_RANK, MLA_HEADS, MLA_QK_DIM)
    wq_nope = wq[:, :, :MLA_NOPE].reshape(MLA_Q_RANK, -1)
    wq_rope = wq[:, :, MLA_NOPE:].reshape(MLA_Q_RANK, -1)
    wq_rope_rot = jnp.concatenate([-wq[:, :, MLA_NOPE + half:], wq[:, :, MLA_NOPE:MLA_NOPE + half]],
                                  axis=2).reshape(MLA_Q_RANK, -1)
    wq_all = jnp.concatenate([wq_nope, wq_rope, wq_rope_rot], axis=1).T.astype(_BF16)

    wkv = mla_w_ukv[l].reshape(MLA_KV_RANK, MLA_HEADS, MLA_NOPE + MLA_VDIM)
    wkn = wkv[:, :, :MLA_NOPE].reshape(MLA_KV_RANK, -1).astype(_BF16)
    wvt = wkv[:, :, MLA_NOPE:].reshape(MLA_KV_RANK, -1).T.astype(_BF16)

    def block_diag(w):
        eye = jnp.eye(LRU_BLOCKS, dtype=w.dtype)
        dense = jnp.einsum('hij,hk->hikj', w, eye)
        return dense.reshape(LRU_WIDTH, LRU_WIDTH).astype(_BF16)

    return dict(w_bm=w_bm, w_tm=w_tm, wq=wq_all, wkn=wkn, wvt=wvt,
                wa=block_diag(lru_w_a[l]), wx=block_diag(lru_w_x[l]),
                wpa=w_proj_a[l].astype(_BF16), wpb=w_proj_b[l].astype(_BF16),
                wpc=w_proj_c[l].astype(_BF16), wo=w_out[l].astype(_BF16))


def _rope_tables():
    pos = jnp.arange(SEQ, dtype=_F32)
    inv_freq = ROPE_THETA ** (-jnp.arange(0, MLA_ROPE, 2, dtype=_F32) / MLA_ROPE)
    ang = pos[:, None] * inv_freq[None, :]
    cos = jnp.cos(ang)
    sin = jnp.sin(ang)
    cos128t = jnp.tile(cos, (1, 4)).T
    sin128t = jnp.tile(sin, (1, 4)).T
    csk = jnp.concatenate([cos, cos, sin, sin], axis=1)
    return cos128t, sin128t, csk


def kernel(x, pre_norm_g, w_in, gm_ln_g, gm_ln_b, gm_ws, gm_bs, mla_q_norm_g, mla_w_uq,
           mla_kv_norm_g, mla_w_ukv, lru_conv_w, lru_conv_b, lru_w_a, lru_b_a, lru_w_x,
           lru_b_x, lru_lambda, w_proj_a, w_proj_b, w_proj_c, w_out, post_norm_g):
    cos128t, sin128t, csk = _rope_tables()
    x2d = x.reshape(TOKENS, D_MODEL)
    row = lambda a: a.reshape(1, -1)
    for l in range(DEPTH):
        w = _layer_weights(l, w_in, mla_w_uq, mla_w_ukv, lru_w_a, lru_w_x, w_proj_a, w_proj_b,
                           w_proj_c, w_out)
        g_pre = row(pre_norm_g[l])
        p_bm = _proj(x2d, g_pre, w["w_bm"], PROJ_BM_TN, time_major=False)
        p_tm = _proj(x2d, g_pre, w["w_tm"], PROJ_TM_TN, time_major=True)

        cc = _lru_mixer(p_tm.reshape(TOKENS, N_TM), lru_conv_w[l], row(lru_conv_b[l]),
                        w["wa"], row(lru_b_a[l]), w["wx"], row(lru_b_x[l]), row(lru_lambda[l]),
                        w["wpc"])
        cc_tm = cc.reshape(SEQ, BATCH * D_MODEL)

        qt, k, vt = _mla_prep(p_bm, cos128t, sin128t, csk, row(mla_q_norm_g[l]),
                              row(mla_kv_norm_g[l]), w["wq"], w["wkn"], w["wvt"])
        cb = _attention(qt, k, vt, p_bm, w["wpb"])

        x2d = _final(x2d, p_bm, cb, cc_tm, row(gm_ln_g[l]), row(gm_ln_b[l]), gm_ws[l],
                     gm_bs[l][:, :, None], w["wpa"], w["wo"], row(post_norm_g[l]))
    return x2d.reshape(BATCH, SEQ, D_MODEL)
```

```python
import functools
import math

import jax
import jax.numpy as jnp
from jax import lax
from jax.experimental import pallas as pl
from jax.experimental.pallas import tpu as pltpu

D_MODEL = 1024
BATCH = 8
SEQ = 2048
DEPTH = 2
TOKENS = BATCH * SEQ
CHUNK = 64
EPS = 1e-6

GM_WIDTH = 1024
GM_GROUPS = 4
GM_BLOCK = 128
GM_GW = GM_WIDTH // GM_GROUPS

MLA_HEADS = 8
MLA_NOPE = 128
MLA_ROPE = 64
MLA_VDIM = 128
MLA_QK_DIM = MLA_NOPE + MLA_ROPE
MLA_Q_RANK = 384
MLA_KV_RANK = 256
MLA_WIDTH = MLA_HEADS * MLA_VDIM
ROPE_THETA = 10000.0

LRU_WIDTH = 1280
LRU_BLOCKS = 16
LRU_BW = LRU_WIDTH // LRU_BLOCKS
LRU_C = 8.0
CONV_W = 4

IN_SIZES = (GM_WIDTH, GM_WIDTH, GM_WIDTH, MLA_Q_RANK, MLA_KV_RANK, MLA_ROPE, MLA_WIDTH,
            LRU_WIDTH, LRU_WIDTH, D_MODEL, D_MODEL, D_MODEL)

BM_SEG = 1024
(COL_U, COL_V, COL_ZA, COL_ZB, COL_GA, COL_GB) = range(6)
LATENT_W = MLA_Q_RANK + MLA_KV_RANK + 2 * MLA_ROPE
N_BM = 6 * BM_SEG + LATENT_W
COL_LATENT = (6 * BM_SEG) // LATENT_W

VMEM_LIMIT_BYTES = 56 * 1024 * 1024

PROJ_TM = 1024
PROJ_TN = 768
LRU_TS = 64
GATE_TN = 256
PREP_TM = 512
ATT_TQ = 256
ATT_TK = 256
ATT_D = 256
FINAL_TM = 512

_F32 = jnp.float32
_BF16 = jnp.bfloat16


def _params(*sem):
    return pltpu.CompilerParams(dimension_semantics=sem, vmem_limit_bytes=VMEM_LIMIT_BYTES)


def _sigmoid(x):
    return 1.0 / (1.0 + jnp.exp(-x))


def _silu(x):
    return x * _sigmoid(x)


def _rmsnorm_bf16(x, g):
    ms = jnp.mean(x * x, axis=-1, keepdims=True)
    return (x * lax.rsqrt(ms + EPS) * g).astype(_BF16)


def _proj_kernel(x_ref, g_ref, w_ref, o_ref, h_scr):
    j = pl.program_id(1)

    @pl.when(j == 0)
    def _():
        h_scr[...] = _rmsnorm_bf16(x_ref[...], g_ref[...])

    o_ref[...] = jnp.dot(h_scr[...], w_ref[j], preferred_element_type=_F32).astype(o_ref.dtype)


def _proj(x2d, g, w3):
    nj = w3.shape[0]
    return pl.pallas_call(
        _proj_kernel,
        grid=(TOKENS // PROJ_TM, nj),
        in_specs=[
            pl.BlockSpec((PROJ_TM, D_MODEL), lambda i, j: (i, 0)),
            pl.BlockSpec((1, D_MODEL), lambda i, j: (0, 0)),
            pl.BlockSpec(w3.shape, lambda i, j: (0, 0, 0), pipeline_mode=pl.Buffered(1)),
        ],
        out_specs=pl.BlockSpec((PROJ_TM, PROJ_TN), lambda i, j: (i, j)),
        out_shape=jax.ShapeDtypeStruct((TOKENS, nj * PROJ_TN), _BF16),
        scratch_shapes=[pltpu.VMEM((PROJ_TM, D_MODEL), _BF16)],
        compiler_params=_params("arbitrary", "arbitrary"),
        name="proj_bm",
    )(x2d, g, w3)


def _gate_k_range(c):
    first_block = (c * GATE_TN) // LRU_BW
    last_block = ((c + 1) * GATE_TN - 1) // LRU_BW
    k0 = (first_block * LRU_BW) // GATE_TN * GATE_TN
    k1 = -(-((last_block + 1) * LRU_BW) // GATE_TN) * GATE_TN
    return k0, k1


def _lru_kernel(x_ref, pg_ref, wxc_ref, wz_ref, wg_ref, cw_ref, cb_ref, wax_ref, ba_ref, bx_ref,
                lam_ref, wp_ref, o_ref, xext, a_scr, b_scr, h_scr):
    rows = LRU_TS * BATCH
    halo = (CONV_W - 1) * BATCH

    @pl.when(pl.program_id(0) == 0)
    def _():
        xext[0:halo, :] = jnp.zeros((halo, LRU_WIDTH), _F32)
        h_scr[...] = jnp.zeros_like(h_scr)

    xt = jnp.swapaxes(x_ref[...], 0, 1).reshape(rows, D_MODEL)
    hn = _rmsnorm_bf16(xt, pg_ref[...])

    xext[halo:halo + rows, :] = jnp.dot(hn, wxc_ref[...], preferred_element_type=_F32)
    xc = cb_ref[...] + cw_ref[CONV_W - 1:CONV_W, :] * xext[halo:halo + rows, :]
    for k in range(CONV_W - 1):
        xc = xc + cw_ref[k:k + 1, :] * xext[k * BATCH:k * BATCH + rows, :]
    xext[0:halo, :] = xext[rows:rows + halo, :]

    xcb = xc.astype(_BF16)
    lam = lam_ref[...]
    softplus_neg_lam = jnp.maximum(-lam, 0.0) + jnp.log(1.0 + jnp.exp(-jnp.abs(lam)))
    for c in range(LRU_WIDTH // GATE_TN):
        cs = slice(c * GATE_TN, (c + 1) * GATE_TN)
        k0, k1 = _gate_k_range(c)
        ri = jnp.dot(xcb[:, k0:k1], wax_ref[c, 0:k1 - k0, :], preferred_element_type=_F32)
        r = _sigmoid(ri[:, 0:GATE_TN] + ba_ref[:, cs])
        i = _sigmoid(ri[:, GATE_TN:2 * GATE_TN] + bx_ref[:, cs])
        a = jnp.exp((-LRU_C) * r * softplus_neg_lam[:, cs])
        mult = jnp.sqrt(jnp.maximum(1.0 - a * a, 0.0))
        a_scr[:, cs] = a
        b_scr[:, cs] = mult * (i * xc[:, cs])

    def step(s, h):
        off = pl.multiple_of(s * BATCH, BATCH)
        h = a_scr[pl.ds(off, BATCH), :] * h + b_scr[pl.ds(off, BATCH), :]
        b_scr[pl.ds(off, BATCH), :] = h
        return h

    h_scr[...] = lax.fori_loop(0, LRU_TS, step, h_scr[...], unroll=8)

    z = jnp.dot(hn, wz_ref[...], preferred_element_type=_F32)
    y = (b_scr[...] * _silu(z)).astype(_BF16)
    g = jnp.dot(hn, wg_ref[...], preferred_element_type=_F32)
    cc = _sigmoid(g) * jnp.dot(y, wp_ref[...], preferred_element_type=_F32)
    o_ref[...] = jnp.swapaxes(cc.reshape(LRU_TS, BATCH, D_MODEL), 0, 1).astype(o_ref.dtype)


def _lru_mixer(x3d, pre_g, wxc, wz, wg, conv_w, conv_b, wax, ba, bx, lam, wp):
    rows = LRU_TS * BATCH
    halo = (CONV_W - 1) * BATCH

    def resident(a):
        return pl.BlockSpec(a.shape, lambda t: (0,) * a.ndim, pipeline_mode=pl.Buffered(1))

    return pl.pallas_call(
        _lru_kernel,
        grid=(SEQ // LRU_TS,),
        in_specs=[pl.BlockSpec((BATCH, LRU_TS, D_MODEL), lambda t: (0, t, 0))]
        + [resident(a) for a in (pre_g, wxc, wz, wg, conv_w, conv_b, wax, ba, bx, lam, wp)],
        out_specs=pl.BlockSpec((BATCH, LRU_TS, D_MODEL), lambda t: (0, t, 0)),
        out_shape=jax.ShapeDtypeStruct((BATCH, SEQ, D_MODEL), _BF16),
        scratch_shapes=[
            pltpu.VMEM((rows + halo, LRU_WIDTH), _F32),
            pltpu.VMEM((rows, LRU_WIDTH), _F32),
            pltpu.VMEM((rows, LRU_WIDTH), _F32),
            pltpu.VMEM((BATCH, LRU_WIDTH), _F32),
        ],
        compiler_params=_params("arbitrary"),
        name="lru_mixer",
    )(x3d, pre_g, wxc, wz, wg, conv_w, conv_b, wax, ba, bx, lam, wp)


def _mla_prep_kernel(c_ref, cost_ref, sint_ref, csk_ref, qg_ref, kvg_ref, wqt_ref, wkn_ref, wvt_ref,
                     qt_ref, k_ref, vt_ref):
    qscale = math.log2(math.e) / math.sqrt(MLA_QK_DIM)
    c = c_ref[...].astype(_F32)
    cq = c[:, 0:MLA_Q_RANK]
    ckv = c[:, MLA_Q_RANK:MLA_Q_RANK + MLA_KV_RANK]
    krk = c[:, MLA_Q_RANK + MLA_KV_RANK:LATENT_W]

    hq = (cq * lax.rsqrt(jnp.mean(cq * cq, axis=-1, keepdims=True) + EPS)
          * qg_ref[...]).astype(_BF16)
    qt = lax.dot_general(wqt_ref[...], hq, (((1,), (1,)), ((), ())), preferred_element_type=_F32)
    nope_w = MLA_HEADS * MLA_NOPE
    rope_w = MLA_HEADS * MLA_ROPE
    cost = cost_ref[...]
    sint = sint_ref[...]
    for p in range(rope_w // 128):
        qr = qt[nope_w + 128 * p:nope_w + 128 * (p + 1), :]
        qrr = qt[nope_w + rope_w + 128 * p:nope_w + rope_w + 128 * (p + 1), :]
        q_pair = ((qr * cost + qrr * sint) * qscale).astype(_BF16)
        for h in (2 * p, 2 * p + 1):
            qt_ref[ATT_D * h:ATT_D * h + MLA_NOPE, :] = (
                qt[MLA_NOPE * h:MLA_NOPE * (h + 1), :] * qscale).astype(_BF16)
            qt_ref[ATT_D * h + MLA_NOPE:ATT_D * (h + 1), :] = q_pair

    hkv = (ckv * lax.rsqrt(jnp.mean(ckv * ckv, axis=-1, keepdims=True) + EPS)
           * kvg_ref[...]).astype(_BF16)
    kn = jnp.dot(hkv, wkn_ref[...], preferred_element_type=_F32)
    vt_ref[...] = lax.dot_general(wvt_ref[...], hkv, (((1,), (1,)), ((), ())),
                                  preferred_element_type=_F32).astype(_BF16)

    t = krk * csk_ref[...]
    kf2 = t + pltpu.roll(t, MLA_ROPE, 1)
    lane = lax.broadcasted_iota(jnp.int32, kf2.shape, 1)
    kr_even = jnp.where(lane < MLA_ROPE, kf2, 0.0).astype(_BF16)
    kr_odd = jnp.where(lane >= MLA_ROPE, kf2, 0.0).astype(_BF16)
    for h in range(MLA_HEADS):
        k_ref[:, ATT_D * h:ATT_D * h + MLA_NOPE] = kn[:, MLA_NOPE * h:MLA_NOPE * (h + 1)].astype(_BF16)
        k_ref[:, ATT_D * h + MLA_NOPE:ATT_D * (h + 1)] = kr_even if h % 2 == 0 else kr_odd


def _mla_prep(p_bm, cos128t, sin128t, csk, qg, kvg, wqt, wkn, wvt):
    tiles_per_seq = SEQ // PREP_TM
    const = lambda i: (0, 0)
    pos = lambda i: (i % tiles_per_seq, 0)
    pos_t = lambda i: (0, i % tiles_per_seq)
    return pl.pallas_call(
        _mla_prep_kernel,
        grid=(TOKENS // PREP_TM,),
        in_specs=[
            pl.BlockSpec((PREP_TM, LATENT_W), lambda i: (i, COL_LATENT)),
            pl.BlockSpec((128, PREP_TM), pos_t),
            pl.BlockSpec((128, PREP_TM), pos_t),
            pl.BlockSpec((PREP_TM, 128), pos),
            pl.BlockSpec((1, MLA_Q_RANK), const),
            pl.BlockSpec((1, MLA_KV_RANK), const),
            pl.BlockSpec(wqt.shape, const),
            pl.BlockSpec(wkn.shape, const),
            pl.BlockSpec(wvt.shape, const),
        ],
        out_specs=[
            pl.BlockSpec((MLA_HEADS * ATT_D, PREP_TM), lambda i: (0, i)),
            pl.BlockSpec((PREP_TM, MLA_HEADS * ATT_D), lambda i: (i, 0)),
            pl.BlockSpec((MLA_WIDTH, PREP_TM), lambda i: (0, i)),
        ],
        out_shape=[
            jax.ShapeDtypeStruct((MLA_HEADS * ATT_D, TOKENS), _BF16),
            jax.ShapeDtypeStruct((TOKENS, MLA_HEADS * ATT_D), _BF16),
            jax.ShapeDtypeStruct((MLA_WIDTH, TOKENS), _BF16),
        ],
        compiler_params=_params("arbitrary"),
        name="mla_prep",
    )(p_bm, cos128t, sin128t, csk, qg, kvg, wqt, wkn, wvt)


def _attn_kernel(qt_ref, k_ref, vt_ref, zb_ref, gb_ref, wp_ref, o_ref,
                 st_scr, mt_scr, m_scr, l_scr, acc_scr, y_scr):
    qi = pl.program_id(1)
    key_chunk = lax.broadcasted_iota(jnp.int32, (ATT_TK, ATT_TQ), 0) // CHUNK
    query_chunk = lax.broadcasted_iota(jnp.int32, (ATT_TK, ATT_TQ), 1) // CHUNK
    diag_mask = key_chunk <= query_chunk

    m_scr[...] = jnp.full(m_scr.shape, -1e30, _F32)
    l_scr[...] = jnp.zeros(l_scr.shape, _F32)
    acc_scr[...] = jnp.zeros(acc_scr.shape, _F32)

    def scores(j, h):
        off = pl.multiple_of(j * ATT_TK, ATT_TK)
        ds_ = slice(ATT_D * h, ATT_D * (h + 1))
        st = jnp.dot(k_ref[pl.ds(off, ATT_TK), ds_], qt_ref[ds_, :], preferred_element_type=_F32)
        st_scr[h] = st
        mt_scr[h] = jnp.max(st, axis=0, keepdims=True)

    def accumulate(j, h, masked):
        off = pl.multiple_of(j * ATT_TK, ATT_TK)
        st = st_scr[h]
        if masked:
            st = jnp.where(diag_mask, st, -1e30)
            mt = jnp.max(st, axis=0, keepdims=True)
        else:
            mt = mt_scr[h]
        m_prev = m_scr[h]
        m_new = jnp.maximum(m_prev, mt)
        alpha = jnp.exp2(m_prev - m_new)
        p = jnp.exp2(st - m_new)
        l_scr[h] = alpha * l_scr[h] + jnp.sum(p, axis=0, keepdims=True)
        vt = vt_ref[MLA_VDIM * h:MLA_VDIM * (h + 1), pl.ds(off, ATT_TK)]
        acc_scr[h] = alpha * acc_scr[h] + jnp.dot(vt, p.astype(_BF16), preferred_element_type=_F32)
        m_scr[h] = m_new

    for h in range(MLA_HEADS):
        scores(0, h)

    def body(j, carry):
        for h in range(MLA_HEADS):
            accumulate(j, h, False)
            scores(j + 1, h)
        return carry

    lax.fori_loop(0, qi, body, 0)
    for h in range(MLA_HEADS):
        accumulate(qi, h, True)

    for h in range(MLA_HEADS):
        hs = slice(MLA_VDIM * h, MLA_VDIM * (h + 1))
        o = (acc_scr[h] * (1.0 / l_scr[h])).T
        y_scr[:, hs] = (o * _silu(zb_ref[:, hs].astype(_F32))).astype(_BF16)

    cb = jnp.dot(y_scr[...], wp_ref[...], preferred_element_type=_F32)
    o_ref[...] = (_sigmoid(gb_ref[...].astype(_F32)) * cb).astype(o_ref.dtype)


def _attention(qt, k, vt, p_bm, wp):
    nq = SEQ // ATT_TQ
    qmap = lambda b, i: (b * nq + i, 0)
    return pl.pallas_call(
        _attn_kernel,
        grid=(BATCH, nq),
        in_specs=[
            pl.BlockSpec((MLA_HEADS * ATT_D, ATT_TQ), lambda b, i: (0, b * nq + i)),
            pl.BlockSpec((SEQ, MLA_HEADS * ATT_D), lambda b, i: (b, 0)),
            pl.BlockSpec((MLA_WIDTH, SEQ), lambda b, i: (0, b)),
            pl.BlockSpec((ATT_TQ, BM_SEG), lambda b, i: (b * nq + i, COL_ZB)),
            pl.BlockSpec((ATT_TQ, BM_SEG), lambda b, i: (b * nq + i, COL_GB)),
            pl.BlockSpec((MLA_WIDTH, D_MODEL), lambda b, i: (0, 0)),
        ],
        out_specs=pl.BlockSpec((ATT_TQ, D_MODEL), qmap),
        out_shape=jax.ShapeDtypeStruct((TOKENS, D_MODEL), _BF16),
        scratch_shapes=[
            pltpu.VMEM((MLA_HEADS, ATT_TK, ATT_TQ), _F32),
            pltpu.VMEM((MLA_HEADS, 1, ATT_TQ), _F32),
            pltpu.VMEM((MLA_HEADS, 1, ATT_TQ), _F32),
            pltpu.VMEM((MLA_HEADS, 1, ATT_TQ), _F32),
            pltpu.VMEM((MLA_HEADS, MLA_VDIM, ATT_TQ), _F32),
            pltpu.VMEM((ATT_TQ, MLA_WIDTH), _BF16),
        ],
        compiler_params=_params("arbitrary", "arbitrary"),
        name="mla_attn",
    )(qt, k, vt, p_bm, p_bm, wp)


def _final_kernel(x_ref, u_ref, v_ref, za_ref, ga_ref, cb_ref, cc_ref, lng_ref, lnb_ref,
                  ws_ref, bs_ref, wpa_ref, wo_ref, pg_ref, o_ref, y_scr):
    idx_r = lax.broadcasted_iota(jnp.int32, (GM_BLOCK, GM_BLOCK), 0) // CHUNK
    idx_c = lax.broadcasted_iota(jnp.int32, (GM_BLOCK, GM_BLOCK), 1) // CHUNK
    causal = idx_c <= idx_r

    v = v_ref[...].astype(_F32)
    mu = jnp.mean(v, axis=-1, keepdims=True)
    vc = v - mu
    var = jnp.mean(vc * vc, axis=-1, keepdims=True)
    vln = (vc * lax.rsqrt(var + EPS) * lng_ref[...] + lnb_ref[...]).astype(_BF16)

    for g in range(GM_GROUPS):
        ws = jnp.where(causal, ws_ref[g], 0.0).astype(_BF16)
        cs = slice(GM_GW * g, GM_GW * (g + 1))
        for r in range(FINAL_TM // GM_BLOCK):
            rs = slice(GM_BLOCK * r, GM_BLOCK * (r + 1))
            sv = jnp.dot(ws, vln[rs, cs], preferred_element_type=_F32) + bs_ref[g]
            y = u_ref[rs, cs].astype(_F32) * sv * _silu(za_ref[rs, cs].astype(_F32))
            y_scr[rs, cs] = y.astype(_BF16)

    ca = _sigmoid(ga_ref[...].astype(_F32)) * jnp.dot(y_scr[...], wpa_ref[...],
                                                      preferred_element_type=_F32)
    merged = ca + cb_ref[...].astype(_F32) + cc_ref[...].astype(_F32)
    o = jnp.dot(merged.astype(_BF16), wo_ref[...], preferred_element_type=_F32)
    o = o * lax.rsqrt(jnp.mean(o * o, axis=-1, keepdims=True) + EPS) * pg_ref[...]
    o_ref[...] = x_ref[...] + o


def _final(x2d, p_bm, cb, cc, lng, lnb, ws, bs_col, wpa, wo, pg):
    const2 = lambda i: (0, 0)
    const3 = lambda i: (0, 0, 0)
    seg = lambda c: pl.BlockSpec((FINAL_TM, BM_SEG), lambda i: (i, c))
    vec = pl.BlockSpec((1, D_MODEL), const2)
    return pl.pallas_call(
        _final_kernel,
        grid=(TOKENS // FINAL_TM,),
        in_specs=[
            pl.BlockSpec((FINAL_TM, D_MODEL), lambda i: (i, 0)),
            seg(COL_U), seg(COL_V), seg(COL_ZA), seg(COL_GA),
            pl.BlockSpec((FINAL_TM, D_MODEL), lambda i: (i, 0)),
            pl.BlockSpec((FINAL_TM, D_MODEL), lambda i: (i, 0)),
            vec, vec,
            pl.BlockSpec((GM_GROUPS, GM_BLOCK, GM_BLOCK), const3),
            pl.BlockSpec((GM_GROUPS, GM_BLOCK, 1), const3),
            pl.BlockSpec((GM_WIDTH, D_MODEL), const2),
            pl.BlockSpec((D_MODEL, D_MODEL), const2),
            vec,
        ],
        out_specs=pl.BlockSpec((FINAL_TM, D_MODEL), lambda i: (i, 0)),
        out_shape=jax.ShapeDtypeStruct((TOKENS, D_MODEL), _F32),
        scratch_shapes=[pltpu.VMEM((FINAL_TM, GM_WIDTH), _BF16)],
        compiler_params=_params("arbitrary"),
        name="gmlp_merge_out",
    )(x2d, p_bm, p_bm, p_bm, p_bm, cb, cc, lng, lnb, ws, bs_col, wpa, wo, pg)


def _layer_weights(l, w_in, mla_w_uq, mla_w_ukv, lru_w_a, lru_w_x, w_proj_a, w_proj_b, w_proj_c,
                   w_out):
    cuts = [0]
    for s in IN_SIZES:
        cuts.append(cuts[-1] + s)
    seg = lambda k: w_in[l][:, cuts[k]:cuts[k + 1]]
    (u, v, z_a, c_q, c_kv, k_rope, z_b, x_c, z_c, g_a, g_b, g_c) = [seg(k) for k in range(12)]
    half = MLA_ROPE // 2
    k_rope_rot = jnp.concatenate([-k_rope[:, half:], k_rope[:, :half]], axis=1)
    w_bm = jnp.concatenate([u, v, z_a, z_b, g_a, g_b, c_q, c_kv, k_rope, k_rope_rot],
                           axis=1).astype(_BF16)
    w_bm = w_bm.reshape(D_MODEL, N_BM // PROJ_TN, PROJ_TN).transpose(1, 0, 2)

    wq = mla_w_uq[l].reshape(MLA_Q_RANK, MLA_HEADS, MLA_QK_DIM)
    wq_nope = wq[:, :, :MLA_NOPE].reshape(MLA_Q_RANK, -1)
    wq_rope = wq[:, :, MLA_NOPE:].reshape(MLA_Q_RANK, -1)
    wq_rope_rot = jnp.concatenate([-wq[:, :, MLA_NOPE + half:], wq[:, :, MLA_NOPE:MLA_NOPE + half]],
                                  axis=2).reshape(MLA_Q_RANK, -1)
    wq_all = jnp.concatenate([wq_nope, wq_rope, wq_rope_rot], axis=1).T.astype(_BF16)

    wkv = mla_w_ukv[l].reshape(MLA_KV_RANK, MLA_HEADS, MLA_NOPE + MLA_VDIM)
    wkn = wkv[:, :, :MLA_NOPE].reshape(MLA_KV_RANK, -1).astype(_BF16)
    wvt = wkv[:, :, MLA_NOPE:].reshape(MLA_KV_RANK, -1).T.astype(_BF16)

    def block_diag(w):
        eye = jnp.eye(LRU_BLOCKS, dtype=w.dtype)
        dense = jnp.einsum('hij,hk->hikj', w, eye)
        return dense.reshape(LRU_WIDTH, LRU_WIDTH).astype(_BF16)

    wa_dense = block_diag(lru_w_a[l])
    wx_dense = block_diag(lru_w_x[l])
    k_max = max(k1 - k0 for k0, k1 in map(_gate_k_range, range(LRU_WIDTH // GATE_TN)))
    wax = []
    for c in range(LRU_WIDTH // GATE_TN):
        k0, k1 = _gate_k_range(c)
        cs = slice(c * GATE_TN, (c + 1) * GATE_TN)
        tile = jnp.concatenate([wa_dense[k0:k1, cs], wx_dense[k0:k1, cs]], axis=1)
        wax.append(jnp.pad(tile, ((0, k_max - (k1 - k0)), (0, 0))))
    wax = jnp.stack(wax)

    return dict(w_bm=w_bm, wxc=x_c.astype(_BF16), wz=z_c.astype(_BF16), wg=g_c.astype(_BF16),
                wq=wq_all, wkn=wkn, wvt=wvt, wax=wax,
                wpa=w_proj_a[l].astype(_BF16), wpb=w_proj_b[l].astype(_BF16),
                wpc=w_proj_c[l].astype(_BF16), wo=w_out[l].astype(_BF16))


def _rope_tables():
    pos = jnp.arange(SEQ, dtype=_F32)
    inv_freq = ROPE_THETA ** (-jnp.arange(0, MLA_ROPE, 2, dtype=_F32) / MLA_ROPE)
    ang = pos[:, None] * inv_freq[None, :]
    cos = jnp.cos(ang)
    sin = jnp.sin(ang)
    cos128t = jnp.tile(cos, (1, 4)).T
    sin128t = jnp.tile(sin, (1, 4)).T
    csk = jnp.concatenate([cos, cos, sin, sin], axis=1)
    return cos128t, sin128t, csk


def kernel(x, pre_norm_g, w_in, gm_ln_g, gm_ln_b, gm_ws, gm_bs, mla_q_norm_g, mla_w_uq,
           mla_kv_norm_g, mla_w_ukv, lru_conv_w, lru_conv_b, lru_w_a, lru_b_a, lru_w_x,
           lru_b_x, lru_lambda, w_proj_a, w_proj_b, w_proj_c, w_out, post_norm_g):
    cos128t, sin128t, csk = _rope_tables()
    x2d = x.reshape(TOKENS, D_MODEL)
    row = lambda a: a.reshape(1, -1)
    for l in range(DEPTH):
        w = _layer_weights(l, w_in, mla_w_uq, mla_w_ukv, lru_w_a, lru_w_x, w_proj_a, w_proj_b,
                           w_proj_c, w_out)
        g_pre = row(pre_norm_g[l])
        p_bm = _proj(x2d, g_pre, w["w_bm"])

        cc = _lru_mixer(x2d.reshape(BATCH, SEQ, D_MODEL), g_pre, w["wxc"], w["wz"], w["wg"],
                        lru_conv_w[l], row(lru_conv_b[l]), w["wax"], row(lru_b_a[l]),
                        row(lru_b_x[l]), row(lru_lambda[l]), w["wpc"]).reshape(TOKENS, D_MODEL)

        qt, k, vt = _mla_prep(p_bm, cos128t, sin128t, csk, row(mla_q_norm_g[l]),
                              row(mla_kv_norm_g[l]), w["wq"], w["wkn"], w["wvt"])
        cb = _attention(qt, k, vt, p_bm, w["wpb"])

        x2d = _final(x2d, p_bm, cb, cc, row(gm_ln_g[l]), row(gm_ln_b[l]), gm_ws[l],
                     gm_bs[l][:, :, None], w["wpa"], w["wo"], row(post_norm_g[l]))
    return x2d.reshape(BATCH, SEQ, D_MODEL)
```

```python
import functools
import math

import jax
import jax.numpy as jnp
from jax import lax
from jax.experimental import pallas as pl
from jax.experimental.pallas import tpu as pltpu

D_MODEL = 1024
BATCH = 8
SEQ = 2048
DEPTH = 2
TOKENS = BATCH * SEQ
CHUNK = 64
EPS = 1e-6

GM_WIDTH = 1024
GM_GROUPS = 4
GM_BLOCK = 128
GM_GW = GM_WIDTH // GM_GROUPS

MLA_HEADS = 8
MLA_NOPE = 128
MLA_ROPE = 64
MLA_VDIM = 128
MLA_QK_DIM = MLA_NOPE + MLA_ROPE
MLA_Q_RANK = 384
MLA_KV_RANK = 256
MLA_WIDTH = MLA_HEADS * MLA_VDIM
ROPE_THETA = 10000.0

LRU_WIDTH = 1280
LRU_BLOCKS = 16
LRU_BW = LRU_WIDTH // LRU_BLOCKS
LRU_C = 8.0
CONV_W = 4

IN_SIZES = (GM_WIDTH, GM_WIDTH, GM_WIDTH, MLA_Q_RANK, MLA_KV_RANK, MLA_ROPE, MLA_WIDTH,
            LRU_WIDTH, LRU_WIDTH, D_MODEL, D_MODEL, D_MODEL)

BM_SEG = 1024
(COL_U, COL_V, COL_ZA, COL_ZB, COL_GA, COL_GB) = range(6)
LATENT_W = MLA_Q_RANK + MLA_KV_RANK + 2 * MLA_ROPE
N_BM = 6 * BM_SEG + LATENT_W
COL_LATENT = (6 * BM_SEG) // LATENT_W

VMEM_LIMIT_BYTES = 56 * 1024 * 1024

PROJ_TM = 1024
PROJ_TN = 2304
PROJ_DOT_N = 768
LRU_TS = 64
GATE_TN = 256
PREP_TM = 512
ATT_TQ = 256
ATT_TK = 256
ATT_D = 256
FINAL_TM = 512

_F32 = jnp.float32
_BF16 = jnp.bfloat16


def _params(*sem):
    return pltpu.CompilerParams(dimension_semantics=sem, vmem_limit_bytes=VMEM_LIMIT_BYTES)


def _sigmoid(x):
    return 1.0 / (1.0 + jnp.exp(-x))


def _silu(x):
    return x * _sigmoid(x)


def _rmsnorm_bf16(x, g):
    ms = jnp.mean(x * x, axis=-1, keepdims=True)
    return (x * lax.rsqrt(ms + EPS) * g).astype(_BF16)


def _dot_nt(a, b_t):
    return lax.dot_general(a, b_t, (((1,), (1,)), ((), ())), preferred_element_type=_F32)


def _proj_kernel(x_ref, g_ref, wt_ref, o_ref, h_scr):
    @pl.when(pl.program_id(1) == 0)
    def _():
        h_scr[...] = _rmsnorm_bf16(x_ref[...], g_ref[...])

    for k in range(PROJ_TN // PROJ_DOT_N):
        cs = slice(k * PROJ_DOT_N, (k + 1) * PROJ_DOT_N)
        o_ref[:, cs] = _dot_nt(h_scr[...], wt_ref[cs, :]).astype(o_ref.dtype)


def _proj(x2d, g, wt):
    return pl.pallas_call(
        _proj_kernel,
        grid=(TOKENS // PROJ_TM, N_BM // PROJ_TN),
        in_specs=[
            pl.BlockSpec((PROJ_TM, D_MODEL), lambda i, j: (i, 0)),
            pl.BlockSpec((1, D_MODEL), lambda i, j: (0, 0)),
            pl.BlockSpec((PROJ_TN, D_MODEL), lambda i, j: (j, 0)),
        ],
        out_specs=pl.BlockSpec((PROJ_TM, PROJ_TN), lambda i, j: (i, j)),
        out_shape=jax.ShapeDtypeStruct((TOKENS, N_BM), _BF16),
        scratch_shapes=[pltpu.VMEM((PROJ_TM, D_MODEL), _BF16)],
        compiler_params=_params("arbitrary", "arbitrary"),
        name="proj_bm",
    )(x2d, g, wt)


def _gate_k_range(c):
    first_block = (c * GATE_TN) // LRU_BW
    last_block = ((c + 1) * GATE_TN - 1) // LRU_BW
    k0 = (first_block * LRU_BW) // GATE_TN * GATE_TN
    k1 = -(-((last_block + 1) * LRU_BW) // GATE_TN) * GATE_TN
    return k0, k1


def _lru_kernel(x_ref, pg_ref, wxc_ref, wz_ref, wg_ref, cw_ref, cb_ref, wax_ref, ba_ref, bx_ref,
                lam_ref, wp_ref, o_ref, xext, a_scr, b_scr, h_scr):
    rows = LRU_TS * BATCH
    halo = (CONV_W - 1) * BATCH

    @pl.when(pl.program_id(0) == 0)
    def _():
        xext[0:halo, :] = jnp.zeros((halo, LRU_WIDTH), _F32)
        h_scr[...] = jnp.zeros_like(h_scr)

    xt = jnp.swapaxes(x_ref[...], 0, 1).reshape(rows, D_MODEL)
    hn = _rmsnorm_bf16(xt, pg_ref[...])

    xext[halo:halo + rows, :] = _dot_nt(hn, wxc_ref[...])
    xc = cb_ref[...] + cw_ref[CONV_W - 1:CONV_W, :] * xext[halo:halo + rows, :]
    for k in range(CONV_W - 1):
        xc = xc + cw_ref[k:k + 1, :] * xext[k * BATCH:k * BATCH + rows, :]
    xext[0:halo, :] = xext[rows:rows + halo, :]

    xcb = xc.astype(_BF16)
    lam = lam_ref[...]
    softplus_neg_lam = jnp.maximum(-lam, 0.0) + jnp.log(1.0 + jnp.exp(-jnp.abs(lam)))
    for c in range(LRU_WIDTH // GATE_TN):
        cs = slice(c * GATE_TN, (c + 1) * GATE_TN)
        k0, k1 = _gate_k_range(c)
        ri = jnp.dot(xcb[:, k0:k1], wax_ref[c, 0:k1 - k0, :], preferred_element_type=_F32)
        r = _sigmoid(ri[:, 0:GATE_TN] + ba_ref[:, cs])
        i = _sigmoid(ri[:, GATE_TN:2 * GATE_TN] + bx_ref[:, cs])
        a = jnp.exp((-LRU_C) * r * softplus_neg_lam[:, cs])
        mult = jnp.sqrt(jnp.maximum(1.0 - a * a, 0.0))
        a_scr[:, cs] = a
        b_scr[:, cs] = mult * (i * xc[:, cs])

    def step(s, h):
        off = pl.multiple_of(s * BATCH, BATCH)
        h = a_scr[pl.ds(off, BATCH), :] * h + b_scr[pl.ds(off, BATCH), :]
        b_scr[pl.ds(off, BATCH), :] = h
        return h

    h_scr[...] = lax.fori_loop(0, LRU_TS, step, h_scr[...], unroll=8)

    z = _dot_nt(hn, wz_ref[...])
    y = (b_scr[...] * _silu(z)).astype(_BF16)
    g = _dot_nt(hn, wg_ref[...])
    cc = _sigmoid(g) * jnp.dot(y, wp_ref[...], preferred_element_type=_F32)
    o_ref[...] = jnp.swapaxes(cc.reshape(LRU_TS, BATCH, D_MODEL), 0, 1).astype(o_ref.dtype)


def _lru_mixer(x3d, pre_g, wxc, wz, wg, conv_w, conv_b, wax, ba, bx, lam, wp):
    rows = LRU_TS * BATCH
    halo = (CONV_W - 1) * BATCH

    def resident(a):
        return pl.BlockSpec(a.shape, lambda t: (0,) * a.ndim, pipeline_mode=pl.Buffered(1))

    return pl.pallas_call(
        _lru_kernel,
        grid=(SEQ // LRU_TS,),
        in_specs=[pl.BlockSpec((BATCH, LRU_TS, D_MODEL), lambda t: (0, t, 0))]
        + [resident(a) for a in (pre_g, wxc, wz, wg, conv_w, conv_b, wax, ba, bx, lam, wp)],
        out_specs=pl.BlockSpec((BATCH, LRU_TS, D_MODEL), lambda t: (0, t, 0)),
        out_shape=jax.ShapeDtypeStruct((BATCH, SEQ, D_MODEL), _BF16),
        scratch_shapes=[
            pltpu.VMEM((rows + halo, LRU_WIDTH), _F32),
            pltpu.VMEM((rows, LRU_WIDTH), _F32),
            pltpu.VMEM((rows, LRU_WIDTH), _F32),
            pltpu.VMEM((BATCH, LRU_WIDTH), _F32),
        ],
        compiler_params=_params("arbitrary"),
        name="lru_mixer",
    )(x3d, pre_g, wxc, wz, wg, conv_w, conv_b, wax, ba, bx, lam, wp)


def _mla_prep_kernel(c_ref, cost_ref, sint_ref, csk_ref, qg_ref, kvg_ref, wqt_ref, wkn_ref, wvt_ref,
                     qt_ref, k_ref, vt_ref):
    qscale = math.log2(math.e) / math.sqrt(MLA_QK_DIM)
    c = c_ref[...].astype(_F32)
    cq = c[:, 0:MLA_Q_RANK]
    ckv = c[:, MLA_Q_RANK:MLA_Q_RANK + MLA_KV_RANK]
    krk = c[:, MLA_Q_RANK + MLA_KV_RANK:LATENT_W]

    hq = (cq * lax.rsqrt(jnp.mean(cq * cq, axis=-1, keepdims=True) + EPS)
          * qg_ref[...]).astype(_BF16)
    qt = lax.dot_general(wqt_ref[...], hq, (((1,), (1,)), ((), ())), preferred_element_type=_F32)
    nope_w = MLA_HEADS * MLA_NOPE
    rope_w = MLA_HEADS * MLA_ROPE
    cost = cost_ref[...]
    sint = sint_ref[...]
    for p in range(rope_w // 128):
        qr = qt[nope_w + 128 * p:nope_w + 128 * (p + 1), :]
        qrr = qt[nope_w + rope_w + 128 * p:nope_w + rope_w + 128 * (p + 1), :]
        q_pair = ((qr * cost + qrr * sint) * qscale).astype(_BF16)
        for h in (2 * p, 2 * p + 1):
            qt_ref[ATT_D * h:ATT_D * h + MLA_NOPE, :] = (
                qt[MLA_NOPE * h:MLA_NOPE * (h + 1), :] * qscale).astype(_BF16)
            qt_ref[ATT_D * h + MLA_NOPE:ATT_D * (h + 1), :] = q_pair

    hkv = (ckv * lax.rsqrt(jnp.mean(ckv * ckv, axis=-1, keepdims=True) + EPS)
           * kvg_ref[...]).astype(_BF16)
    kn = jnp.dot(hkv, wkn_ref[...], preferred_element_type=_F32)
    vt_ref[...] = lax.dot_general(wvt_ref[...], hkv, (((1,), (1,)), ((), ())),
                                  preferred_element_type=_F32).astype(_BF16)

    t = krk * csk_ref[...]
    kf2 = t + pltpu.roll(t, MLA_ROPE, 1)
    lane = lax.broadcasted_iota(jnp.int32, kf2.shape, 1)
    kr_even = jnp.where(lane < MLA_ROPE, kf2, 0.0).astype(_BF16)
    kr_odd = jnp.where(lane >= MLA_ROPE, kf2, 0.0).astype(_BF16)
    for h in range(MLA_HEADS):
        k_ref[:, ATT_D * h:ATT_D * h + MLA_NOPE] = kn[:, MLA_NOPE * h:MLA_NOPE * (h + 1)].astype(_BF16)
        k_ref[:, ATT_D * h + MLA_NOPE:ATT_D * (h + 1)] = kr_even if h % 2 == 0 else kr_odd


def _mla_prep(p_bm, cos128t, sin128t, csk, qg, kvg, wqt, wkn, wvt):
    tiles_per_seq = SEQ // PREP_TM
    const = lambda i: (0, 0)
    pos = lambda i: (i % tiles_per_seq, 0)
    pos_t = lambda i: (0, i % tiles_per_seq)
    return pl.pallas_call(
        _mla_prep_kernel,
        grid=(TOKENS // PREP_TM,),
        in_specs=[
            pl.BlockSpec((PREP_TM, LATENT_W), lambda i: (i, COL_LATENT)),
            pl.BlockSpec((128, PREP_TM), pos_t),
            pl.BlockSpec((128, PREP_TM), pos_t),
            pl.BlockSpec((PREP_TM, 128), pos),
            pl.BlockSpec((1, MLA_Q_RANK), const),
            pl.BlockSpec((1, MLA_KV_RANK), const),
            pl.BlockSpec(wqt.shape, const),
            pl.BlockSpec(wkn.shape, const),
            pl.BlockSpec(wvt.shape, const),
        ],
        out_specs=[
            pl.BlockSpec((MLA_HEADS * ATT_D, PREP_TM), lambda i: (0, i)),
            pl.BlockSpec((PREP_TM, MLA_HEADS * ATT_D), lambda i: (i, 0)),
            pl.BlockSpec((MLA_WIDTH, PREP_TM), lambda i: (0, i)),
        ],
        out_shape=[
            jax.ShapeDtypeStruct((MLA_HEADS * ATT_D, TOKENS), _BF16),
            jax.ShapeDtypeStruct((TOKENS, MLA_HEADS * ATT_D), _BF16),
            jax.ShapeDtypeStruct((MLA_WIDTH, TOKENS), _BF16),
        ],
        compiler_params=_params("arbitrary"),
        name="mla_prep",
    )(p_bm, cos128t, sin128t, csk, qg, kvg, wqt, wkn, wvt)


def _attn_kernel(qt_ref, k_ref, vt_ref, zb_ref, gb_ref, wp_ref, o_ref,
                 st_scr, mt_scr, m_scr, l_scr, acc_scr, y_scr):
    qi = pl.program_id(1)
    key_chunk = lax.broadcasted_iota(jnp.int32, (ATT_TK, ATT_TQ), 0) // CHUNK
    query_chunk = lax.broadcasted_iota(jnp.int32, (ATT_TK, ATT_TQ), 1) // CHUNK
    diag_mask = key_chunk <= query_chunk

    m_scr[...] = jnp.full(m_scr.shape, -1e30, _F32)
    l_scr[...] = jnp.zeros(l_scr.shape, _F32)
    acc_scr[...] = jnp.zeros(acc_scr.shape, _F32)

    def scores(j, h):
        off = pl.multiple_of(j * ATT_TK, ATT_TK)
        ds_ = slice(ATT_D * h, ATT_D * (h + 1))
        st = jnp.dot(k_ref[pl.ds(off, ATT_TK), ds_], qt_ref[ds_, :], preferred_element_type=_F32)
        st_scr[h] = st
        mt_scr[h] = jnp.max(st, axis=0, keepdims=True)

    def accumulate(j, h, masked):
        off = pl.multiple_of(j * ATT_TK, ATT_TK)
        st = st_scr[h]
        if masked:
            st = jnp.where(diag_mask, st, -1e30)
            mt = jnp.max(st, axis=0, keepdims=True)
        else:
            mt = mt_scr[h]
        m_prev = m_scr[h]
        m_new = jnp.maximum(m_prev, mt)
        alpha = jnp.exp2(m_prev - m_new)
        p = jnp.exp2(st - m_new)
        l_scr[h] = alpha * l_scr[h] + jnp.sum(p, axis=0, keepdims=True)
        vt = vt_ref[MLA_VDIM * h:MLA_VDIM * (h + 1), pl.ds(off, ATT_TK)]
        acc_scr[h] = alpha * acc_scr[h] + jnp.dot(vt, p.astype(_BF16), preferred_element_type=_F32)
        m_scr[h] = m_new

    for h in range(MLA_HEADS):
        scores(0, h)

    def body(j, carry):
        for h in range(MLA_HEADS):
            accumulate(j, h, False)
            scores(j + 1, h)
        return carry

    lax.fori_loop(0, qi, body, 0)
    for h in range(MLA_HEADS):
        accumulate(qi, h, True)

    for h in range(MLA_HEADS):
        hs = slice(MLA_VDIM * h, MLA_VDIM * (h + 1))
        o = (acc_scr[h] * (1.0 / l_scr[h])).T
        y_scr[:, hs] = (o * _silu(zb_ref[:, hs].astype(_F32))).astype(_BF16)

    cb = jnp.dot(y_scr[...], wp_ref[...], preferred_element_type=_F32)
    o_ref[...] = (_sigmoid(gb_ref[...].astype(_F32)) * cb).astype(o_ref.dtype)


def _attention(qt, k, vt, p_bm, wp):
    nq = SEQ // ATT_TQ
    qmap = lambda b, i: (b * nq + i, 0)
    return pl.pallas_call(
        _attn_kernel,
        grid=(BATCH, nq),
        in_specs=[
            pl.BlockSpec((MLA_HEADS * ATT_D, ATT_TQ), lambda b, i: (0, b * nq + i)),
            pl.BlockSpec((SEQ, MLA_HEADS * ATT_D), lambda b, i: (b, 0)),
            pl.BlockSpec((MLA_WIDTH, SEQ), lambda b, i: (0, b)),
            pl.BlockSpec((ATT_TQ, BM_SEG), lambda b, i: (b * nq + i, COL_ZB)),
            pl.BlockSpec((ATT_TQ, BM_SEG), lambda b, i: (b * nq + i, COL_GB)),
            pl.BlockSpec((MLA_WIDTH, D_MODEL), lambda b, i: (0, 0)),
        ],
        out_specs=pl.BlockSpec((ATT_TQ, D_MODEL), qmap),
        out_shape=jax.ShapeDtypeStruct((TOKENS, D_MODEL), _BF16),
        scratch_shapes=[
            pltpu.VMEM((MLA_HEADS, ATT_TK, ATT_TQ), _F32),
            pltpu.VMEM((MLA_HEADS, 1, ATT_TQ), _F32),
            pltpu.VMEM((MLA_HEADS, 1, ATT_TQ), _F32),
            pltpu.VMEM((MLA_HEADS, 1, ATT_TQ), _F32),
            pltpu.VMEM((MLA_HEADS, MLA_VDIM, ATT_TQ), _F32),
            pltpu.VMEM((ATT_TQ, MLA_WIDTH), _BF16),
        ],
        compiler_params=_params("arbitrary", "arbitrary"),
        name="mla_attn",
    )(qt, k, vt, p_bm, p_bm, wp)


def _final_kernel(x_ref, u_ref, v_ref, za_ref, ga_ref, cb_ref, cc_ref, lng_ref, lnb_ref,
                  ws_ref, bs_ref, wpa_ref, wo_ref, pg_ref, o_ref, y_scr):
    idx_r = lax.broadcasted_iota(jnp.int32, (GM_BLOCK, GM_BLOCK), 0) // CHUNK
    idx_c = lax.broadcasted_iota(jnp.int32, (GM_BLOCK, GM_BLOCK), 1) // CHUNK
    causal = idx_c <= idx_r

    v = v_ref[...].astype(_F32)
    mu = jnp.mean(v, axis=-1, keepdims=True)
    vc = v - mu
    var = jnp.mean(vc * vc, axis=-1, keepdims=True)
    vln = (vc * lax.rsqrt(var + EPS) * lng_ref[...] + lnb_ref[...]).astype(_BF16)

    for g in range(GM_GROUPS):
        ws = jnp.where(causal, ws_ref[g], 0.0).astype(_BF16)
        cs = slice(GM_GW * g, GM_GW * (g + 1))
        for r in range(FINAL_TM // GM_BLOCK):
            rs = slice(GM_BLOCK * r, GM_BLOCK * (r + 1))
            sv = jnp.dot(ws, vln[rs, cs], preferred_element_type=_F32) + bs_ref[g]
            y = u_ref[rs, cs].astype(_F32) * sv * _silu(za_ref[rs, cs].astype(_F32))
            y_scr[rs, cs] = y.astype(_BF16)

    ca = _sigmoid(ga_ref[...].astype(_F32)) * jnp.dot(y_scr[...], wpa_ref[...],
                                                      preferred_element_type=_F32)
    merged = ca + cb_ref[...].astype(_F32) + cc_ref[...].astype(_F32)
    o = jnp.dot(merged.astype(_BF16), wo_ref[...], preferred_element_type=_F32)
    o = o * lax.rsqrt(jnp.mean(o * o, axis=-1, keepdims=True) + EPS) * pg_ref[...]
    o_ref[...] = x_ref[...] + o


def _final(x2d, p_bm, cb, cc, lng, lnb, ws, bs_col, wpa, wo, pg):
    const2 = lambda i: (0, 0)
    const3 = lambda i: (0, 0, 0)
    seg = lambda c: pl.BlockSpec((FINAL_TM, BM_SEG), lambda i: (i, c))
    vec = pl.BlockSpec((1, D_MODEL), const2)
    return pl.pallas_call(
        _final_kernel,
        grid=(TOKENS // FINAL_TM,),
        in_specs=[
            pl.BlockSpec((FINAL_TM, D_MODEL), lambda i: (i, 0)),
            seg(COL_U), seg(COL_V), seg(COL_ZA), seg(COL_GA),
            pl.BlockSpec((FINAL_TM, D_MODEL), lambda i: (i, 0)),
            pl.BlockSpec((FINAL_TM, D_MODEL), lambda i: (i, 0)),
            vec, vec,
            pl.BlockSpec((GM_GROUPS, GM_BLOCK, GM_BLOCK), const3),
            pl.BlockSpec((GM_GROUPS, GM_BLOCK, 1), const3),
            pl.BlockSpec((GM_WIDTH, D_MODEL), const2),
            pl.BlockSpec((D_MODEL, D_MODEL), const2),
            vec,
        ],
        out_specs=pl.BlockSpec((FINAL_TM, D_MODEL), lambda i: (i, 0)),
        out_shape=jax.ShapeDtypeStruct((TOKENS, D_MODEL), _F32),
        scratch_shapes=[pltpu.VMEM((FINAL_TM, GM_WIDTH), _BF16)],
        compiler_params=_params("arbitrary"),
        name="gmlp_merge_out",
    )(x2d, p_bm, p_bm, p_bm, p_bm, cb, cc, lng, lnb, ws, bs_col, wpa, wo, pg)


def _layer_weights(l, w_in, mla_w_uq, mla_w_ukv, lru_w_a, lru_w_x, w_proj_a, w_proj_b, w_proj_c,
                   w_out):
    cuts = [0]
    for s in IN_SIZES:
        cuts.append(cuts[-1] + s)
    w_in_t = jnp.swapaxes(w_in[l], 0, 1)
    seg = lambda k: w_in_t[cuts[k]:cuts[k + 1], :]
    (u, v, z_a, c_q, c_kv, k_rope, z_b, x_c, z_c, g_a, g_b, g_c) = [seg(k) for k in range(12)]
    half = MLA_ROPE // 2
    k_rope_rot = jnp.concatenate([-k_rope[half:, :], k_rope[:half, :]], axis=0)
    w_bm = jnp.concatenate([u, v, z_a, z_b, g_a, g_b, c_q, c_kv, k_rope, k_rope_rot],
                           axis=0).astype(_BF16)

    wq = mla_w_uq[l].reshape(MLA_Q_RANK, MLA_HEADS, MLA_QK_DIM)
    wq_nope = wq[:, :, :MLA_NOPE].reshape(MLA_Q_RANK, -1)
    wq_rope = wq[:, :, MLA_NOPE:].reshape(MLA_Q_RANK, -1)
    wq_rope_rot = jnp.concatenate([-wq[:, :, MLA_NOPE + half:], wq[:, :, MLA_NOPE:MLA_NOPE + half]],
                                  axis=2).reshape(MLA_Q_RANK, -1)
    wq_all = jnp.concatenate([wq_nope, wq_rope, wq_rope_rot], axis=1).T.astype(_BF16)

    wkv = mla_w_ukv[l].reshape(MLA_KV_RANK, MLA_HEADS, MLA_NOPE + MLA_VDIM)
    wkn = wkv[:, :, :MLA_NOPE].reshape(MLA_KV_RANK, -1).astype(_BF16)
    wvt = wkv[:, :, MLA_NOPE:].reshape(MLA_KV_RANK, -1).T.astype(_BF16)

    def block_diag(w):
        eye = jnp.eye(LRU_BLOCKS, dtype=w.dtype)
        dense = jnp.einsum('hij,hk->hikj', w, eye)
        return dense.reshape(LRU_WIDTH, LRU_WIDTH).astype(_BF16)

    wa_dense = block_diag(lru_w_a[l])
    wx_dense = block_diag(lru_w_x[l])
    k_max = max(k1 - k0 for k0, k1 in map(_gate_k_range, range(LRU_WIDTH // GATE_TN)))
    wax = []
    for c in range(LRU_WIDTH // GATE_TN):
        k0, k1 = _gate_k_range(c)
        cs = slice(c * GATE_TN, (c + 1) * GATE_TN)
        tile = jnp.concatenate([wa_dense[k0:k1, cs], wx_dense[k0:k1, cs]], axis=1)
        wax.append(jnp.pad(tile, ((0, k_max - (k1 - k0)), (0, 0))))
    wax = jnp.stack(wax)

    return dict(w_bm=w_bm, wxc=x_c.astype(_BF16), wz=z_c.astype(_BF16), wg=g_c.astype(_BF16),
                wq=wq_all, wkn=wkn, wvt=wvt, wax=wax,
                wpa=w_proj_a[l].astype(_BF16), wpb=w_proj_b[l].astype(_BF16),
                wpc=w_proj_c[l].astype(_BF16), wo=w_out[l].astype(_BF16))


def _rope_tables():
    pos = jnp.arange(SEQ, dtype=_F32)
    inv_freq = ROPE_THETA ** (-jnp.arange(0, MLA_ROPE, 2, dtype=_F32) / MLA_ROPE)
    ang = pos[:, None] * inv_freq[None, :]
    cos = jnp.cos(ang)
    sin = jnp.sin(ang)
    cos128t = jnp.tile(cos, (1, 4)).T
    sin128t = jnp.tile(sin, (1, 4)).T
    csk = jnp.concatenate([cos, cos, sin, sin], axis=1)
    return cos128t, sin128t, csk


def kernel(x, pre_norm_g, w_in, gm_ln_g, gm_ln_b, gm_ws, gm_bs, mla_q_norm_g, mla_w_uq,
           mla_kv_norm_g, mla_w_ukv, lru_conv_w, lru_conv_b, lru_w_a, lru_b_a, lru_w_x,
           lru_b_x, lru_lambda, w_proj_a, w_proj_b, w_proj_c, w_out, post_norm_g):
    cos128t, sin128t, csk = _rope_tables()
    x2d = x.reshape(TOKENS, D_MODEL)
    row = lambda a: a.reshape(1, -1)
    for l in range(DEPTH):
        w = _layer_weights(l, w_in, mla_w_uq, mla_w_ukv, lru_w_a, lru_w_x, w_proj_a, w_proj_b,
                           w_proj_c, w_out)
        g_pre = row(pre_norm_g[l])
        p_bm = _proj(x2d, g_pre, w["w_bm"])

        cc = _lru_mixer(x2d.reshape(BATCH, SEQ, D_MODEL), g_pre, w["wxc"], w["wz"], w["wg"],
                        lru_conv_w[l], row(lru_conv_b[l]), w["wax"], row(lru_b_a[l]),
                        row(lru_b_x[l]), row(lru_lambda[l]), w["wpc"]).reshape(TOKENS, D_MODEL)

        qt, k, vt = _mla_prep(p_bm, cos128t, sin128t, csk, row(mla_q_norm_g[l]),
                              row(mla_kv_norm_g[l]), w["wq"], w["wkn"], w["wvt"])
        cb = _attention(qt, k, vt, p_bm, w["wpb"])

        x2d = _final(x2d, p_bm, cb, cc, row(gm_ln_g[l]), row(gm_ln_b[l]), gm_ws[l],
                     gm_bs[l][:, :, None], w["wpa"], w["wo"], row(post_norm_g[l]))
    return x2d.reshape(BATCH, SEQ, D_MODEL)
```

```python
import functools
import math

import jax
import jax.numpy as jnp
from jax import lax
from jax.experimental import pallas as pl
from jax.experimental.pallas import tpu as pltpu

D_MODEL = 1024
BATCH = 8
SEQ = 2048
DEPTH = 2
TOKENS = BATCH * SEQ
CHUNK = 64
EPS = 1e-6

GM_WIDTH = 1024
GM_GROUPS = 4
GM_BLOCK = 128
GM_GW = GM_WIDTH // GM_GROUPS

MLA_HEADS = 8
MLA_NOPE = 128
MLA_ROPE = 64
MLA_VDIM = 128
MLA_QK_DIM = MLA_NOPE + MLA_ROPE
MLA_Q_RANK = 384
MLA_KV_RANK = 256
MLA_WIDTH = MLA_HEADS * MLA_VDIM
ROPE_THETA = 10000.0

LRU_WIDTH = 1280
LRU_BLOCKS = 16
LRU_BW = LRU_WIDTH // LRU_BLOCKS
LRU_C = 8.0
CONV_W = 4

IN_SIZES = (GM_WIDTH, GM_WIDTH, GM_WIDTH, MLA_Q_RANK, MLA_KV_RANK, MLA_ROPE, MLA_WIDTH,
            LRU_WIDTH, LRU_WIDTH, D_MODEL, D_MODEL, D_MODEL)

BM_SEG = 1024
(COL_U, COL_V, COL_ZA, COL_ZB, COL_GA, COL_GB) = range(6)
LATENT_W = MLA_Q_RANK + MLA_KV_RANK + 2 * MLA_ROPE
N_BM = 6 * BM_SEG + LATENT_W
COL_LATENT = (6 * BM_SEG) // LATENT_W

VMEM_LIMIT_BYTES = 56 * 1024 * 1024

PROJ_TM = 1024
PROJ_TN = 2304
PROJ_DOT_N = 768
LRU_TS = 64
GATE_TN = 256
PREP_TM = 512
ATT_TQ = 256
ATT_TK = 256
ATT_D = 256
FINAL_TM = 512

_F32 = jnp.float32
_BF16 = jnp.bfloat16


def _params(*sem):
    return pltpu.CompilerParams(dimension_semantics=sem, vmem_limit_bytes=VMEM_LIMIT_BYTES)


def _sigmoid(x):
    return 1.0 / (1.0 + jnp.exp(-x))


def _silu(x):
    return x * _sigmoid(x)


def _rmsnorm_bf16(x, g):
    ms = jnp.mean(x * x, axis=-1, keepdims=True)
    return (x * lax.rsqrt(ms + EPS) * g).astype(_BF16)


def _dot_nt(a, b_t):
    return lax.dot_general(a, b_t, (((1,), (1,)), ((), ())), preferred_element_type=_F32)


def _proj_kernel(x_ref, g_ref, wt_ref, o_ref, h_scr):
    @pl.when(pl.program_id(1) == 0)
    def _():
        h_scr[...] = _rmsnorm_bf16(x_ref[...], g_ref[...])

    for k in range(PROJ_TN // PROJ_DOT_N):
        cs = slice(k * PROJ_DOT_N, (k + 1) * PROJ_DOT_N)
        o_ref[:, cs] = _dot_nt(h_scr[...], wt_ref[cs, :]).astype(o_ref.dtype)


def _proj(x2d, g, wt):
    return pl.pallas_call(
        _proj_kernel,
        grid=(TOKENS // PROJ_TM, N_BM // PROJ_TN),
        in_specs=[
            pl.BlockSpec((PROJ_TM, D_MODEL), lambda i, j: (i, 0)),
            pl.BlockSpec((1, D_MODEL), lambda i, j: (0, 0)),
            pl.BlockSpec((PROJ_TN, D_MODEL), lambda i, j: (j, 0)),
        ],
        out_specs=pl.BlockSpec((PROJ_TM, PROJ_TN), lambda i, j: (i, j)),
        out_shape=jax.ShapeDtypeStruct((TOKENS, N_BM), _BF16),
        scratch_shapes=[pltpu.VMEM((PROJ_TM, D_MODEL), _BF16)],
        compiler_params=_params("arbitrary", "arbitrary"),
        name="proj_bm",
    )(x2d, g, wt)


def _gate_k_range(c):
    first_block = (c * GATE_TN) // LRU_BW
    last_block = ((c + 1) * GATE_TN - 1) // LRU_BW
    k0 = (first_block * LRU_BW) // GATE_TN * GATE_TN
    k1 = -(-((last_block + 1) * LRU_BW) // GATE_TN) * GATE_TN
    return k0, k1


def _lru_kernel(x_ref, pg_ref, wxc_ref, wz_ref, wg_ref, cw_ref, cb_ref, wax_ref, ba_ref, bx_ref,
                lam_ref, wp_ref, o_ref, xext, a_scr, b_scr, h_scr, sz_scr, sg_scr):
    rows = LRU_TS * BATCH
    halo = (CONV_W - 1) * BATCH

    @pl.when(pl.program_id(0) == 0)
    def _():
        xext[0:halo, :] = jnp.zeros((halo, LRU_WIDTH), _F32)
        h_scr[...] = jnp.zeros_like(h_scr)

    xt = jnp.swapaxes(x_ref[...], 0, 1).reshape(rows, D_MODEL)
    hn = _rmsnorm_bf16(xt, pg_ref[...])

    xext[halo:halo + rows, :] = _dot_nt(hn, wxc_ref[...])
    xc = cb_ref[...] + cw_ref[CONV_W - 1:CONV_W, :] * xext[halo:halo + rows, :]
    for k in range(CONV_W - 1):
        xc = xc + cw_ref[k:k + 1, :] * xext[k * BATCH:k * BATCH + rows, :]
    xext[0:halo, :] = xext[rows:rows + halo, :]

    xcb = xc.astype(_BF16)
    lam = lam_ref[...]
    softplus_neg_lam = jnp.maximum(-lam, 0.0) + jnp.log(1.0 + jnp.exp(-jnp.abs(lam)))
    for c in range(LRU_WIDTH // GATE_TN):
        cs = slice(c * GATE_TN, (c + 1) * GATE_TN)
        k0, k1 = _gate_k_range(c)
        ri = jnp.dot(xcb[:, k0:k1], wax_ref[c, 0:k1 - k0, :], preferred_element_type=_F32)
        r = _sigmoid(ri[:, 0:GATE_TN] + ba_ref[:, cs])
        i = _sigmoid(ri[:, GATE_TN:2 * GATE_TN] + bx_ref[:, cs])
        a = jnp.exp((-LRU_C) * r * softplus_neg_lam[:, cs])
        mult = jnp.sqrt(jnp.maximum(1.0 - a * a, 0.0))
        a_scr[:, cs] = a
        b_scr[:, cs] = mult * (i * xc[:, cs])

    sz_scr[...] = _silu(_dot_nt(hn, wz_ref[...]))
    sg_scr[...] = _sigmoid(_dot_nt(hn, wg_ref[...]))

    h = h_scr[...]
    for s in range(LRU_TS):
        rs = slice(s * BATCH, (s + 1) * BATCH)
        h = a_scr[rs, :] * h + b_scr[rs, :]
        b_scr[rs, :] = h
    h_scr[...] = h

    y = (b_scr[...] * sz_scr[...]).astype(_BF16)
    cc = sg_scr[...] * jnp.dot(y, wp_ref[...], preferred_element_type=_F32)
    o_ref[...] = jnp.swapaxes(cc.reshape(LRU_TS, BATCH, D_MODEL), 0, 1).astype(o_ref.dtype)


def _lru_mixer(x3d, pre_g, wxc, wz, wg, conv_w, conv_b, wax, ba, bx, lam, wp):
    rows = LRU_TS * BATCH
    halo = (CONV_W - 1) * BATCH

    def resident(a):
        return pl.BlockSpec(a.shape, lambda t: (0,) * a.ndim, pipeline_mode=pl.Buffered(1))

    return pl.pallas_call(
        _lru_kernel,
        grid=(SEQ // LRU_TS,),
        in_specs=[pl.BlockSpec((BATCH, LRU_TS, D_MODEL), lambda t: (0, t, 0))]
        + [resident(a) for a in (pre_g, wxc, wz, wg, conv_w, conv_b, wax, ba, bx, lam, wp)],
        out_specs=pl.BlockSpec((BATCH, LRU_TS, D_MODEL), lambda t: (0, t, 0)),
        out_shape=jax.ShapeDtypeStruct((BATCH, SEQ, D_MODEL), _BF16),
        scratch_shapes=[
            pltpu.VMEM((rows + halo, LRU_WIDTH), _F32),
            pltpu.VMEM((rows, LRU_WIDTH), _F32),
            pltpu.VMEM((rows, LRU_WIDTH), _F32),
            pltpu.VMEM((BATCH, LRU_WIDTH), _F32),
            pltpu.VMEM((rows, LRU_WIDTH), _F32),
            pltpu.VMEM((rows, D_MODEL), _F32),
        ],
        compiler_params=_params("arbitrary"),
        name="lru_mixer",
    )(x3d, pre_g, wxc, wz, wg, conv_w, conv_b, wax, ba, bx, lam, wp)


def _mla_prep_kernel(c_ref, cost_ref, sint_ref, csk_ref, qg_ref, kvg_ref, wqt_ref, wkn_ref, wvt_ref,
                     qt_ref, k_ref, vt_ref):
    qscale = math.log2(math.e) / math.sqrt(MLA_QK_DIM)
    c = c_ref[...].astype(_F32)
    cq = c[:, 0:MLA_Q_RANK]
    ckv = c[:, MLA_Q_RANK:MLA_Q_RANK + MLA_KV_RANK]
    krk = c[:, MLA_Q_RANK + MLA_KV_RANK:LATENT_W]

    hq = (cq * lax.rsqrt(jnp.mean(cq * cq, axis=-1, keepdims=True) + EPS)
          * qg_ref[...]).astype(_BF16)
    qt = lax.dot_general(wqt_ref[...], hq, (((1,), (1,)), ((), ())), preferred_element_type=_F32)
    nope_w = MLA_HEADS * MLA_NOPE
    rope_w = MLA_HEADS * MLA_ROPE
    cost = cost_ref[...]
    sint = sint_ref[...]
    for p in range(rope_w // 128):
        qr = qt[nope_w + 128 * p:nope_w + 128 * (p + 1), :]
        qrr = qt[nope_w + rope_w + 128 * p:nope_w + rope_w + 128 * (p + 1), :]
        q_pair = ((qr * cost + qrr * sint) * qscale).astype(_BF16)
        for h in (2 * p, 2 * p + 1):
            qt_ref[ATT_D * h:ATT_D * h + MLA_NOPE, :] = (
                qt[MLA_NOPE * h:MLA_NOPE * (h + 1), :] * qscale).astype(_BF16)
            qt_ref[ATT_D * h + MLA_NOPE:ATT_D * (h + 1), :] = q_pair

    hkv = (ckv * lax.rsqrt(jnp.mean(ckv * ckv, axis=-1, keepdims=True) + EPS)
           * kvg_ref[...]).astype(_BF16)
    kn = jnp.dot(hkv, wkn_ref[...], preferred_element_type=_F32)
    vt_ref[...] = lax.dot_general(wvt_ref[...], hkv, (((1,), (1,)), ((), ())),
                                  preferred_element_type=_F32).astype(_BF16)

    t = krk * csk_ref[...]
    kf2 = t + pltpu.roll(t, MLA_ROPE, 1)
    lane = lax.broadcasted_iota(jnp.int32, kf2.shape, 1)
    kr_even = jnp.where(lane < MLA_ROPE, kf2, 0.0).astype(_BF16)
    kr_odd = jnp.where(lane >= MLA_ROPE, kf2, 0.0).astype(_BF16)
    for h in range(MLA_HEADS):
        k_ref[:, ATT_D * h:ATT_D * h + MLA_NOPE] = kn[:, MLA_NOPE * h:MLA_NOPE * (h + 1)].astype(_BF16)
        k_ref[:, ATT_D * h + MLA_NOPE:ATT_D * (h + 1)] = kr_even if h % 2 == 0 else kr_odd


def _mla_prep(p_bm, cos128t, sin128t, csk, qg, kvg, wqt, wkn, wvt):
    tiles_per_seq = SEQ // PREP_TM
    const = lambda i: (0, 0)
    pos = lambda i: (i % tiles_per_seq, 0)
    pos_t = lambda i: (0, i % tiles_per_seq)
    return pl.pallas_call(
        _mla_prep_kernel,
        grid=(TOKENS // PREP_TM,),
        in_specs=[
            pl.BlockSpec((PREP_TM, LATENT_W), lambda i: (i, COL_LATENT)),
            pl.BlockSpec((128, PREP_TM), pos_t),
            pl.BlockSpec((128, PREP_TM), pos_t),
            pl.BlockSpec((PREP_TM, 128), pos),
            pl.BlockSpec((1, MLA_Q_RANK), const),
            pl.BlockSpec((1, MLA_KV_RANK), const),
            pl.BlockSpec(wqt.shape, const),
            pl.BlockSpec(wkn.shape, const),
            pl.BlockSpec(wvt.shape, const),
        ],
        out_specs=[
            pl.BlockSpec((MLA_HEADS * ATT_D, PREP_TM), lambda i: (0, i)),
            pl.BlockSpec((PREP_TM, MLA_HEADS * ATT_D), lambda i: (i, 0)),
            pl.BlockSpec((MLA_WIDTH, PREP_TM), lambda i: (0, i)),
        ],
        out_shape=[
            jax.ShapeDtypeStruct((MLA_HEADS * ATT_D, TOKENS), _BF16),
            jax.ShapeDtypeStruct((TOKENS, MLA_HEADS * ATT_D), _BF16),
            jax.ShapeDtypeStruct((MLA_WIDTH, TOKENS), _BF16),
        ],
        compiler_params=_params("arbitrary"),
        name="mla_prep",
    )(p_bm, cos128t, sin128t, csk, qg, kvg, wqt, wkn, wvt)


def _attn_kernel(qt_ref, k_ref, vt_ref, zb_ref, gb_ref, wp_ref, o_ref,
                 st_scr, mt_scr, m_scr, l_scr, acc_scr, y_scr):
    qi = pl.program_id(1)
    key_chunk = lax.broadcasted_iota(jnp.int32, (ATT_TK, ATT_TQ), 0) // CHUNK
    query_chunk = lax.broadcasted_iota(jnp.int32, (ATT_TK, ATT_TQ), 1) // CHUNK
    diag_mask = key_chunk <= query_chunk

    m_scr[...] = jnp.full(m_scr.shape, -1e30, _F32)
    l_scr[...] = jnp.zeros(l_scr.shape, _F32)
    acc_scr[...] = jnp.zeros(acc_scr.shape, _F32)

    def scores(j, h):
        off = pl.multiple_of(j * ATT_TK, ATT_TK)
        ds_ = slice(ATT_D * h, ATT_D * (h + 1))
        st = jnp.dot(k_ref[pl.ds(off, ATT_TK), ds_], qt_ref[ds_, :], preferred_element_type=_F32)
        st_scr[h] = st
        mt_scr[h] = jnp.max(st, axis=0, keepdims=True)

    def accumulate(j, h, masked):
        off = pl.multiple_of(j * ATT_TK, ATT_TK)
        st = st_scr[h]
        if masked:
            st = jnp.where(diag_mask, st, -1e30)
            mt = jnp.max(st, axis=0, keepdims=True)
        else:
            mt = mt_scr[h]
        m_prev = m_scr[h]
        m_new = jnp.maximum(m_prev, mt)
        alpha = jnp.exp2(m_prev - m_new)
        p = jnp.exp2(st - m_new)
        l_scr[h] = alpha * l_scr[h] + jnp.sum(p, axis=0, keepdims=True)
        vt = vt_ref[MLA_VDIM * h:MLA_VDIM * (h + 1), pl.ds(off, ATT_TK)]
        acc_scr[h] = alpha * acc_scr[h] + jnp.dot(vt, p.astype(_BF16), preferred_element_type=_F32)
        m_scr[h] = m_new

    for h in range(MLA_HEADS):
        scores(0, h)

    def body(j, carry):
        for h in range(MLA_HEADS):
            accumulate(j, h, False)
            scores(j + 1, h)
        return carry

    lax.fori_loop(0, qi, body, 0)
    for h in range(MLA_HEADS):
        accumulate(qi, h, True)

    for h in range(MLA_HEADS):
        hs = slice(MLA_VDIM * h, MLA_VDIM * (h + 1))
        o = (acc_scr[h] * (1.0 / l_scr[h])).T
        y_scr[:, hs] = (o * _silu(zb_ref[:, hs].astype(_F32))).astype(_BF16)

    cb = jnp.dot(y_scr[...], wp_ref[...], preferred_element_type=_F32)
    o_ref[...] = (_sigmoid(gb_ref[...].astype(_F32)) * cb).astype(o_ref.dtype)


def _attention(qt, k, vt, p_bm, wp):
    nq = SEQ // ATT_TQ
    qmap = lambda b, i: (b * nq + i, 0)
    return pl.pallas_call(
        _attn_kernel,
        grid=(BATCH, nq),
        in_specs=[
            pl.BlockSpec((MLA_HEADS * ATT_D, ATT_TQ), lambda b, i: (0, b * nq + i)),
            pl.BlockSpec((SEQ, MLA_HEADS * ATT_D), lambda b, i: (b, 0)),
            pl.BlockSpec((MLA_WIDTH, SEQ), lambda b, i: (0, b)),
            pl.BlockSpec((ATT_TQ, BM_SEG), lambda b, i: (b * nq + i, COL_ZB)),
            pl.BlockSpec((ATT_TQ, BM_SEG), lambda b, i: (b * nq + i, COL_GB)),
            pl.BlockSpec((MLA_WIDTH, D_MODEL), lambda b, i: (0, 0)),
        ],
        out_specs=pl.BlockSpec((ATT_TQ, D_MODEL), qmap),
        out_shape=jax.ShapeDtypeStruct((TOKENS, D_MODEL), _BF16),
        scratch_shapes=[
            pltpu.VMEM((MLA_HEADS, ATT_TK, ATT_TQ), _F32),
            pltpu.VMEM((MLA_HEADS, 1, ATT_TQ), _F32),
            pltpu.VMEM((MLA_HEADS, 1, ATT_TQ), _F32),
            pltpu.VMEM((MLA_HEADS, 1, ATT_TQ), _F32),
            pltpu.VMEM((MLA_HEADS, MLA_VDIM, ATT_TQ), _F32),
            pltpu.VMEM((ATT_TQ, MLA_WIDTH), _BF16),
        ],
        compiler_params=_params("arbitrary", "arbitrary"),
        name="mla_attn",
    )(qt, k, vt, p_bm, p_bm, wp)


def _final_kernel(x_ref, u_ref, v_ref, za_ref, ga_ref, cb_ref, cc_ref, lng_ref, lnb_ref,
                  ws_ref, bs_ref, wpa_ref, wo_ref, pg_ref, o_ref, y_scr):
    idx_r = lax.broadcasted_iota(jnp.int32, (GM_BLOCK, GM_BLOCK), 0) // CHUNK
    idx_c = lax.broadcasted_iota(jnp.int32, (GM_BLOCK, GM_BLOCK), 1) // CHUNK
    causal = idx_c <= idx_r

    v = v_ref[...].astype(_F32)
    mu = jnp.mean(v, axis=-1, keepdims=True)
    vc = v - mu
    var = jnp.mean(vc * vc, axis=-1, keepdims=True)
    vln = (vc * lax.rsqrt(var + EPS) * lng_ref[...] + lnb_ref[...]).astype(_BF16)

    for g in range(GM_GROUPS):
        ws = jnp.where(causal, ws_ref[g], 0.0).astype(_BF16)
        cs = slice(GM_GW * g, GM_GW * (g + 1))
        for r in range(FINAL_TM // GM_BLOCK):
            rs = slice(GM_BLOCK * r, GM_BLOCK * (r + 1))
            sv = jnp.dot(ws, vln[rs, cs], preferred_element_type=_F32) + bs_ref[g]
            y = u_ref[rs, cs].astype(_F32) * sv * _silu(za_ref[rs, cs].astype(_F32))
            y_scr[rs, cs] = y.astype(_BF16)

    ca = _sigmoid(ga_ref[...].astype(_F32)) * jnp.dot(y_scr[...], wpa_ref[...],
                                                      preferred_element_type=_F32)
    merged = ca + cb_ref[...].astype(_F32) + cc_ref[...].astype(_F32)
    o = jnp.dot(merged.astype(_BF16), wo_ref[...], preferred_element_type=_F32)
    o = o * lax.rsqrt(jnp.mean(o * o, axis=-1, keepdims=True) + EPS) * pg_ref[...]
    o_ref[...] = x_ref[...] + o


def _final(x2d, p_bm, cb, cc, lng, lnb, ws, bs_col, wpa, wo, pg):
    const2 = lambda i: (0, 0)
    const3 = lambda i: (0, 0, 0)
    seg = lambda c: pl.BlockSpec((FINAL_TM, BM_SEG), lambda i: (i, c))
    vec = pl.BlockSpec((1, D_MODEL), const2)
    return pl.pallas_call(
        _final_kernel,
        grid=(TOKENS // FINAL_TM,),
        in_specs=[
            pl.BlockSpec((FINAL_TM, D_MODEL), lambda i: (i, 0)),
            seg(COL_U), seg(COL_V), seg(COL_ZA), seg(COL_GA),
            pl.BlockSpec((FINAL_TM, D_MODEL), lambda i: (i, 0)),
            pl.BlockSpec((FINAL_TM, D_MODEL), lambda i: (i, 0)),
            vec, vec,
            pl.BlockSpec((GM_GROUPS, GM_BLOCK, GM_BLOCK), const3),
            pl.BlockSpec((GM_GROUPS, GM_BLOCK, 1), const3),
            pl.BlockSpec((GM_WIDTH, D_MODEL), const2),
            pl.BlockSpec((D_MODEL, D_MODEL), const2),
            vec,
        ],
        out_specs=pl.BlockSpec((FINAL_TM, D_MODEL), lambda i: (i, 0)),
        out_shape=jax.ShapeDtypeStruct((TOKENS, D_MODEL), _F32),
        scratch_shapes=[pltpu.VMEM((FINAL_TM, GM_WIDTH), _BF16)],
        compiler_params=_params("arbitrary"),
        name="gmlp_merge_out",
    )(x2d, p_bm, p_bm, p_bm, p_bm, cb, cc, lng, lnb, ws, bs_col, wpa, wo, pg)


def _layer_weights(l, w_in, mla_w_uq, mla_w_ukv, lru_w_a, lru_w_x, w_proj_a, w_proj_b, w_proj_c,
                   w_out):
    cuts = [0]
    for s in IN_SIZES:
        cuts.append(cuts[-1] + s)
    w_in_t = jnp.swapaxes(w_in, 1, 2)
    seg = lambda k: lax.slice(w_in_t, (l, cuts[k], 0), (l + 1, cuts[k + 1], D_MODEL)).reshape(
        cuts[k + 1] - cuts[k], D_MODEL)
    (u, v, z_a, c_q, c_kv, k_rope, z_b, x_c, z_c, g_a, g_b, g_c) = [seg(k) for k in range(12)]
    half = MLA_ROPE // 2
    k_rope_rot = jnp.concatenate([-k_rope[half:, :], k_rope[:half, :]], axis=0)
    w_bm = jnp.concatenate([u, v, z_a, z_b, g_a, g_b, c_q, c_kv, k_rope, k_rope_rot],
                           axis=0).astype(_BF16)

    wq = mla_w_uq[l].reshape(MLA_Q_RANK, MLA_HEADS, MLA_QK_DIM)
    wq_nope = wq[:, :, :MLA_NOPE].reshape(MLA_Q_RANK, -1)
    wq_rope = wq[:, :, MLA_NOPE:].reshape(MLA_Q_RANK, -1)
    wq_rope_rot = jnp.concatenate([-wq[:, :, MLA_NOPE + half:], wq[:, :, MLA_NOPE:MLA_NOPE + half]],
                                  axis=2).reshape(MLA_Q_RANK, -1)
    wq_all = jnp.concatenate([wq_nope, wq_rope, wq_rope_rot], axis=1).T.astype(_BF16)

    wkv = mla_w_ukv[l].reshape(MLA_KV_RANK, MLA_HEADS, MLA_NOPE + MLA_VDIM)
    wkn = wkv[:, :, :MLA_NOPE].reshape(MLA_KV_RANK, -1).astype(_BF16)
    wvt = wkv[:, :, MLA_NOPE:].reshape(MLA_KV_RANK, -1).T.astype(_BF16)

    def block_diag(w):
        eye = jnp.eye(LRU_BLOCKS, dtype=w.dtype)
        dense = jnp.einsum('hij,hk->hikj', w, eye)
        return dense.reshape(LRU_WIDTH, LRU_WIDTH).astype(_BF16)

    wa_dense = block_diag(lru_w_a[l])
    wx_dense = block_diag(lru_w_x[l])
    k_max = max(k1 - k0 for k0, k1 in map(_gate_k_range, range(LRU_WIDTH // GATE_TN)))
    wax = []
    for c in range(LRU_WIDTH // GATE_TN):
        k0, k1 = _gate_k_range(c)
        cs = slice(c * GATE_TN, (c + 1) * GATE_TN)
        tile = jnp.concatenate([wa_dense[k0:k1, cs], wx_dense[k0:k1, cs]], axis=1)
        wax.append(jnp.pad(tile, ((0, k_max - (k1 - k0)), (0, 0))))
    wax = jnp.stack(wax)

    return dict(w_bm=w_bm, wxc=x_c.astype(_BF16), wz=z_c.astype(_BF16), wg=g_c.astype(_BF16),
                wq=wq_all, wkn=wkn, wvt=wvt, wax=wax,
                wpa=w_proj_a[l].astype(_BF16), wpb=w_proj_b[l].astype(_BF16),
                wpc=w_proj_c[l].astype(_BF16), wo=w_out[l].astype(_BF16))


def _rope_tables():
    pos = jnp.arange(SEQ, dtype=_F32)
    inv_freq = ROPE_THETA ** (-jnp.arange(0, MLA_ROPE, 2, dtype=_F32) / MLA_ROPE)
    ang = pos[:, None] * inv_freq[None, :]
    cos = jnp.cos(ang)
    sin = jnp.sin(ang)
    cos128t = jnp.tile(cos, (1, 4)).T
    sin128t = jnp.tile(sin, (1, 4)).T
    csk = jnp.concatenate([cos, cos, sin, sin], axis=1)
    return cos128t, sin128t, csk


def kernel(x, pre_norm_g, w_in, gm_ln_g, gm_ln_b, gm_ws, gm_bs, mla_q_norm_g, mla_w_uq,
           mla_kv_norm_g, mla_w_ukv, lru_conv_w, lru_conv_b, lru_w_a, lru_b_a, lru_w_x,
           lru_b_x, lru_lambda, w_proj_a, w_proj_b, w_proj_c, w_out, post_norm_g):
    cos128t, sin128t, csk = _rope_tables()
    x2d = x.reshape(TOKENS, D_MODEL)
    row = lambda a: a.reshape(1, -1)
    for l in range(DEPTH):
        w = _layer_weights(l, w_in, mla_w_uq, mla_w_ukv, lru_w_a, lru_w_x, w_proj_a, w_proj_b,
                           w_proj_c, w_out)
        g_pre = row(pre_norm_g[l])
        p_bm = _proj(x2d, g_pre, w["w_bm"])

        cc = _lru_mixer(x2d.reshape(BATCH, SEQ, D_MODEL), g_pre, w["wxc"], w["wz"], w["wg"],
                        lru_conv_w[l], row(lru_conv_b[l]), w["wax"], row(lru_b_a[l]),
                        row(lru_b_x[l]), row(lru_lambda[l]), w["wpc"]).reshape(TOKENS, D_MODEL)

        qt, k, vt = _mla_prep(p_bm, cos128t, sin128t, csk, row(mla_q_norm_g[l]),
                              row(mla_kv_norm_g[l]), w["wq"], w["wkn"], w["wvt"])
        cb = _attention(qt, k, vt, p_bm, w["wpb"])

        x2d = _final(x2d, p_bm, cb, cc, row(gm_ln_g[l]), row(gm_ln_b[l]), gm_ws[l],
                     gm_bs[l][:, :, None], w["wpa"], w["wo"], row(post_norm_g[l]))
    return x2d.reshape(BATCH, SEQ, D_MODEL)
```

```python
import functools
import math

import jax
import jax.numpy as jnp
from jax import lax
from jax.experimental import pallas as pl
from jax.experimental.pallas import tpu as pltpu

D_MODEL = 1024
BATCH = 8
SEQ = 2048
DEPTH = 2
TOKENS = BATCH * SEQ
CHUNK = 64
EPS = 1e-6

GM_WIDTH = 1024
GM_GROUPS = 4
GM_BLOCK = 128
GM_GW = GM_WIDTH // GM_GROUPS

MLA_HEADS = 8
MLA_NOPE = 128
MLA_ROPE = 64
MLA_VDIM = 128
MLA_QK_DIM = MLA_NOPE + MLA_ROPE
MLA_Q_RANK = 384
MLA_KV_RANK = 256
MLA_WIDTH = MLA_HEADS * MLA_VDIM
ROPE_THETA = 10000.0

LRU_WIDTH = 1280
LRU_BLOCKS = 16
LRU_BW = LRU_WIDTH // LRU_BLOCKS
LRU_C = 8.0
CONV_W = 4

IN_SIZES = (GM_WIDTH, GM_WIDTH, GM_WIDTH, MLA_Q_RANK, MLA_KV_RANK, MLA_ROPE, MLA_WIDTH,
            LRU_WIDTH, LRU_WIDTH, D_MODEL, D_MODEL, D_MODEL)

BM_SEG = 1024
(COL_U, COL_V, COL_ZA, COL_ZB, COL_GA, COL_GB) = range(6)
LATENT_W = MLA_Q_RANK + MLA_KV_RANK + 2 * MLA_ROPE
N_BM = 6 * BM_SEG + LATENT_W
COL_LATENT = (6 * BM_SEG) // LATENT_W

VMEM_LIMIT_BYTES = 56 * 1024 * 1024

PROJ_TM = 1024
PROJ_TN = 2304
PROJ_DOT_N = 768
LRU_TS = 64
GATE_TN = 256
PREP_TM = 512
ATT_TQ = 256
ATT_TK = 256
ATT_D = 256

_F32 = jnp.float32
_BF16 = jnp.bfloat16


def _params(*sem):
    return pltpu.CompilerParams(dimension_semantics=sem, vmem_limit_bytes=VMEM_LIMIT_BYTES)


def _sigmoid(x):
    return 1.0 / (1.0 + jnp.exp(-x))


def _silu(x):
    return x * _sigmoid(x)


def _rmsnorm_bf16(x, g):
    ms = jnp.mean(x * x, axis=-1, keepdims=True)
    return (x * lax.rsqrt(ms + EPS) * g).astype(_BF16)


def _dot_nt(a, b_t):
    return lax.dot_general(a, b_t, (((1,), (1,)), ((), ())), preferred_element_type=_F32)


def _proj_kernel(x_ref, g_ref, wt_ref, o_ref, h_scr):
    @pl.when(pl.program_id(1) == 0)
    def _():
        h_scr[...] = _rmsnorm_bf16(x_ref[...], g_ref[...])

    for k in range(PROJ_TN // PROJ_DOT_N):
        cs = slice(k * PROJ_DOT_N, (k + 1) * PROJ_DOT_N)
        o_ref[:, cs] = _dot_nt(h_scr[...], wt_ref[cs, :]).astype(o_ref.dtype)


def _proj(x2d, g, wt):
    return pl.pallas_call(
        _proj_kernel,
        grid=(TOKENS // PROJ_TM, N_BM // PROJ_TN),
        in_specs=[
            pl.BlockSpec((PROJ_TM, D_MODEL), lambda i, j: (i, 0)),
            pl.BlockSpec((1, D_MODEL), lambda i, j: (0, 0)),
            pl.BlockSpec((PROJ_TN, D_MODEL), lambda i, j: (j, 0)),
        ],
        out_specs=pl.BlockSpec((PROJ_TM, PROJ_TN), lambda i, j: (i, j)),
        out_shape=jax.ShapeDtypeStruct((TOKENS, N_BM), _BF16),
        scratch_shapes=[pltpu.VMEM((PROJ_TM, D_MODEL), _BF16)],
        compiler_params=_params("arbitrary", "arbitrary"),
        name="proj_bm",
    )(x2d, g, wt)


def _gate_k_range(c):
    first_block = (c * GATE_TN) // LRU_BW
    last_block = ((c + 1) * GATE_TN - 1) // LRU_BW
    k0 = (first_block * LRU_BW) // GATE_TN * GATE_TN
    k1 = -(-((last_block + 1) * LRU_BW) // GATE_TN) * GATE_TN
    return k0, k1


def _lru_kernel(x_ref, pg_ref, wxc_ref, wz_ref, wg_ref, cw_ref, cb_ref, wax_ref, ba_ref, bx_ref,
                lam_ref, wp_ref, o_ref, xext, a_scr, b_scr, h_scr, sz_scr, sg_scr):
    rows = LRU_TS * BATCH
    halo = (CONV_W - 1) * BATCH

    @pl.when(pl.program_id(0) == 0)
    def _():
        xext[0:halo, :] = jnp.zeros((halo, LRU_WIDTH), _F32)
        h_scr[...] = jnp.zeros_like(h_scr)

    xt = jnp.swapaxes(x_ref[...], 0, 1).reshape(rows, D_MODEL)
    hn = _rmsnorm_bf16(xt, pg_ref[...])

    xext[halo:halo + rows, :] = _dot_nt(hn, wxc_ref[...])
    xc = cb_ref[...] + cw_ref[CONV_W - 1:CONV_W, :] * xext[halo:halo + rows, :]
    for k in range(CONV_W - 1):
        xc = xc + cw_ref[k:k + 1, :] * xext[k * BATCH:k * BATCH + rows, :]
    xext[0:halo, :] = xext[rows:rows + halo, :]

    xcb = xc.astype(_BF16)
    lam = lam_ref[...]
    softplus_neg_lam = jnp.maximum(-lam, 0.0) + jnp.log(1.0 + jnp.exp(-jnp.abs(lam)))
    for c in range(LRU_WIDTH // GATE_TN):
        cs = slice(c * GATE_TN, (c + 1) * GATE_TN)
        k0, k1 = _gate_k_range(c)
        ri = jnp.dot(xcb[:, k0:k1], wax_ref[c, 0:k1 - k0, :], preferred_element_type=_F32)
        r = _sigmoid(ri[:, 0:GATE_TN] + ba_ref[:, cs])
        i = _sigmoid(ri[:, GATE_TN:2 * GATE_TN] + bx_ref[:, cs])
        a = jnp.exp((-LRU_C) * r * softplus_neg_lam[:, cs])
        mult = jnp.sqrt(jnp.maximum(1.0 - a * a, 0.0))
        a_scr[:, cs] = a
        b_scr[:, cs] = mult * (i * xc[:, cs])

    sz_scr[...] = _silu(_dot_nt(hn, wz_ref[...]))
    sg_scr[...] = _sigmoid(_dot_nt(hn, wg_ref[...]))

    h = h_scr[...]
    for s in range(LRU_TS):
        rs = slice(s * BATCH, (s + 1) * BATCH)
        h = a_scr[rs, :] * h + b_scr[rs, :]
        b_scr[rs, :] = h
    h_scr[...] = h

    y = (b_scr[...] * sz_scr[...]).astype(_BF16)
    cc = sg_scr[...] * jnp.dot(y, wp_ref[...], preferred_element_type=_F32)
    o_ref[...] = jnp.swapaxes(cc.reshape(LRU_TS, BATCH, D_MODEL), 0, 1).astype(o_ref.dtype)


def _lru_mixer(x3d, pre_g, wxc, wz, wg, conv_w, conv_b, wax, ba, bx, lam, wp):
    rows = LRU_TS * BATCH
    halo = (CONV_W - 1) * BATCH

    def resident(a):
        return pl.BlockSpec(a.shape, lambda t: (0,) * a.ndim, pipeline_mode=pl.Buffered(1))

    return pl.pallas_call(
        _lru_kernel,
        grid=(SEQ // LRU_TS,),
        in_specs=[pl.BlockSpec((BATCH, LRU_TS, D_MODEL), lambda t: (0, t, 0))]
        + [resident(a) for a in (pre_g, wxc, wz, wg, conv_w, conv_b, wax, ba, bx, lam, wp)],
        out_specs=pl.BlockSpec((BATCH, LRU_TS, D_MODEL), lambda t: (0, t, 0)),
        out_shape=jax.ShapeDtypeStruct((BATCH, SEQ, D_MODEL), _BF16),
        scratch_shapes=[
            pltpu.VMEM((rows + halo, LRU_WIDTH), _F32),
            pltpu.VMEM((rows, LRU_WIDTH), _F32),
            pltpu.VMEM((rows, LRU_WIDTH), _F32),
            pltpu.VMEM((BATCH, LRU_WIDTH), _F32),
            pltpu.VMEM((rows, LRU_WIDTH), _F32),
            pltpu.VMEM((rows, D_MODEL), _F32),
        ],
        compiler_params=_params("arbitrary"),
        name="lru_mixer",
    )(x3d, pre_g, wxc, wz, wg, conv_w, conv_b, wax, ba, bx, lam, wp)


def _mla_prep_kernel(c_ref, cost_ref, sint_ref, csk_ref, qg_ref, kvg_ref, wqt_ref, wkn_ref, wvt_ref,
                     qt_ref, k_ref, vt_ref):
    qscale = math.log2(math.e) / math.sqrt(MLA_QK_DIM)
    c = c_ref[...].astype(_F32)
    cq = c[:, 0:MLA_Q_RANK]
    ckv = c[:, MLA_Q_RANK:MLA_Q_RANK + MLA_KV_RANK]
    krk = c[:, MLA_Q_RANK + MLA_KV_RANK:LATENT_W]

    hq = (cq * lax.rsqrt(jnp.mean(cq * cq, axis=-1, keepdims=True) + EPS)
          * qg_ref[...]).astype(_BF16)
    qt = lax.dot_general(wqt_ref[...], hq, (((1,), (1,)), ((), ())), preferred_element_type=_F32)
    nope_w = MLA_HEADS * MLA_NOPE
    rope_w = MLA_HEADS * MLA_ROPE
    cost = cost_ref[...]
    sint = sint_ref[...]
    for p in range(rope_w // 128):
        qr = qt[nope_w + 128 * p:nope_w + 128 * (p + 1), :]
        qrr = qt[nope_w + rope_w + 128 * p:nope_w + rope_w + 128 * (p + 1), :]
        q_pair = ((qr * cost + qrr * sint) * qscale).astype(_BF16)
        for h in (2 * p, 2 * p + 1):
            qt_ref[ATT_D * h:ATT_D * h + MLA_NOPE, :] = (
                qt[MLA_NOPE * h:MLA_NOPE * (h + 1), :] * qscale).astype(_BF16)
            qt_ref[ATT_D * h + MLA_NOPE:ATT_D * (h + 1), :] = q_pair

    hkv = (ckv * lax.rsqrt(jnp.mean(ckv * ckv, axis=-1, keepdims=True) + EPS)
           * kvg_ref[...]).astype(_BF16)
    kn = jnp.dot(hkv, wkn_ref[...], preferred_element_type=_F32)
    vt_ref[...] = lax.dot_general(wvt_ref[...], hkv, (((1,), (1,)), ((), ())),
                                  preferred_element_type=_F32).astype(_BF16)

    t = krk * csk_ref[...]
    kf2 = t + pltpu.roll(t, MLA_ROPE, 1)
    lane = lax.broadcasted_iota(jnp.int32, kf2.shape, 1)
    kr_even = jnp.where(lane < MLA_ROPE, kf2, 0.0).astype(_BF16)
    kr_odd = jnp.where(lane >= MLA_ROPE, kf2, 0.0).astype(_BF16)
    for h in range(MLA_HEADS):
        k_ref[:, ATT_D * h:ATT_D * h + MLA_NOPE] = kn[:, MLA_NOPE * h:MLA_NOPE * (h + 1)].astype(_BF16)
        k_ref[:, ATT_D * h + MLA_NOPE:ATT_D * (h + 1)] = kr_even if h % 2 == 0 else kr_odd


def _mla_prep(p_bm, cos128t, sin128t, csk, qg, kvg, wqt, wkn, wvt):
    tiles_per_seq = SEQ // PREP_TM
    const = lambda i: (0, 0)
    pos = lambda i: (i % tiles_per_seq, 0)
    pos_t = lambda i: (0, i % tiles_per_seq)
    return pl.pallas_call(
        _mla_prep_kernel,
        grid=(TOKENS // PREP_TM,),
        in_specs=[
            pl.BlockSpec((PREP_TM, LATENT_W), lambda i: (i, COL_LATENT)),
            pl.BlockSpec((128, PREP_TM), pos_t),
            pl.BlockSpec((128, PREP_TM), pos_t),
            pl.BlockSpec((PREP_TM, 128), pos),
            pl.BlockSpec((1, MLA_Q_RANK), const),
            pl.BlockSpec((1, MLA_KV_RANK), const),
            pl.BlockSpec(wqt.shape, const),
            pl.BlockSpec(wkn.shape, const),
            pl.BlockSpec(wvt.shape, const),
        ],
        out_specs=[
            pl.BlockSpec((MLA_HEADS * ATT_D, PREP_TM), lambda i: (0, i)),
            pl.BlockSpec((PREP_TM, MLA_HEADS * ATT_D), lambda i: (i, 0)),
            pl.BlockSpec((MLA_WIDTH, PREP_TM), lambda i: (0, i)),
        ],
        out_shape=[
            jax.ShapeDtypeStruct((MLA_HEADS * ATT_D, TOKENS), _BF16),
            jax.ShapeDtypeStruct((TOKENS, MLA_HEADS * ATT_D), _BF16),
            jax.ShapeDtypeStruct((MLA_WIDTH, TOKENS), _BF16),
        ],
        compiler_params=_params("arbitrary"),
        name="mla_prep",
    )(p_bm, cos128t, sin128t, csk, qg, kvg, wqt, wkn, wvt)


def _attn_kernel(qt_ref, k_ref, vt_ref, qt_next_ref, k_next_ref, zb_ref, gb_ref, wp_ref,
                 x_ref, u_ref, v_ref, za_ref, ga_ref, cc_ref, lng_ref, lnb_ref, ws_ref, bs_ref,
                 wpa_ref, wo_ref, pg_ref, o_ref,
                 st_scr, mt_scr, m_scr, l_scr, acc_scr, y_scr, ya_scr):
    qi = pl.program_id(1)
    key_chunk = lax.broadcasted_iota(jnp.int32, (ATT_TK, ATT_TQ), 0) // CHUNK
    query_chunk = lax.broadcasted_iota(jnp.int32, (ATT_TK, ATT_TQ), 1) // CHUNK
    diag_mask = key_chunk <= query_chunk

    m_scr[...] = jnp.full(m_scr.shape, -1e30, _F32)
    l_scr[...] = jnp.zeros(l_scr.shape, _F32)
    acc_scr[...] = jnp.zeros(acc_scr.shape, _F32)

    def store_scores(k_tile, q_t, h):
        st = jnp.dot(k_tile, q_t, preferred_element_type=_F32)
        st_scr[h] = st
        mt_scr[h] = jnp.max(st, axis=0, keepdims=True)

    def scores(j, h):
        off = pl.multiple_of(j * ATT_TK, ATT_TK)
        ds_ = slice(ATT_D * h, ATT_D * (h + 1))
        store_scores(k_ref[pl.ds(off, ATT_TK), ds_], qt_ref[ds_, :], h)

    def accumulate(j, h, masked):
        off = pl.multiple_of(j * ATT_TK, ATT_TK)
        st = st_scr[h]
        if masked:
            st = jnp.where(diag_mask, st, -1e30)
            mt = jnp.max(st, axis=0, keepdims=True)
        else:
            mt = mt_scr[h]
        m_prev = m_scr[h]
        m_new = jnp.maximum(m_prev, mt)
        alpha = jnp.exp2(m_prev - m_new)
        p = jnp.exp2(st - m_new)
        l_scr[h] = alpha * l_scr[h] + jnp.sum(p, axis=0, keepdims=True)
        vt = vt_ref[MLA_VDIM * h:MLA_VDIM * (h + 1), pl.ds(off, ATT_TK)]
        acc_scr[h] = alpha * acc_scr[h] + jnp.dot(vt, p.astype(_BF16), preferred_element_type=_F32)
        m_scr[h] = m_new

    @pl.when((pl.program_id(0) == 0) & (qi == 0))
    def _():
        for h in range(MLA_HEADS):
            scores(0, h)

    def body(j, carry):
        for h in range(MLA_HEADS):
            accumulate(j, h, False)
            scores(j + 1, h)
        return carry

    lax.fori_loop(0, qi, body, 0)
    for h in range(MLA_HEADS):
        accumulate(qi, h, True)
        ds_ = slice(ATT_D * h, ATT_D * (h + 1))
        store_scores(k_next_ref[:, ds_], qt_next_ref[ds_, :], h)

    for h in range(MLA_HEADS):
        hs = slice(MLA_VDIM * h, MLA_VDIM * (h + 1))
        o = (acc_scr[h] * (1.0 / l_scr[h])).T
        y_scr[:, hs] = (o * _silu(zb_ref[:, hs].astype(_F32))).astype(_BF16)

    cb = _sigmoid(gb_ref[...].astype(_F32)) * jnp.dot(y_scr[...], wp_ref[...],
                                                      preferred_element_type=_F32)

    idx_r = lax.broadcasted_iota(jnp.int32, (GM_BLOCK, GM_BLOCK), 0) // CHUNK
    idx_c = lax.broadcasted_iota(jnp.int32, (GM_BLOCK, GM_BLOCK), 1) // CHUNK
    causal = idx_c <= idx_r

    v = v_ref[...].astype(_F32)
    mu = jnp.mean(v, axis=-1, keepdims=True)
    vc = v - mu
    var = jnp.mean(vc * vc, axis=-1, keepdims=True)
    vln = (vc * lax.rsqrt(var + EPS) * lng_ref[...] + lnb_ref[...]).astype(_BF16)

    for g in range(GM_GROUPS):
        ws = jnp.where(causal, ws_ref[g], 0.0).astype(_BF16)
        cs = slice(GM_GW * g, GM_GW * (g + 1))
        for r in range(ATT_TQ // GM_BLOCK):
            rs = slice(GM_BLOCK * r, GM_BLOCK * (r + 1))
            sv = jnp.dot(ws, vln[rs, cs], preferred_element_type=_F32) + bs_ref[g]
            y = u_ref[rs, cs].astype(_F32) * sv * _silu(za_ref[rs, cs].astype(_F32))
            ya_scr[rs, cs] = y.astype(_BF16)

    ca = _sigmoid(ga_ref[...].astype(_F32)) * jnp.dot(ya_scr[...], wpa_ref[...],
                                                      preferred_element_type=_F32)
    merged = ca + cb + cc_ref[...].astype(_F32)
    o = jnp.dot(merged.astype(_BF16), wo_ref[...], preferred_element_type=_F32)
    o = o * lax.rsqrt(jnp.mean(o * o, axis=-1, keepdims=True) + EPS) * pg_ref[...]
    o_ref[...] = x_ref[...] + o


def _attention_merge(qt, k, vt, p_bm, x2d, cc, wpb, lng, lnb, ws, bs_col, wpa, wo, pg):
    nq = SEQ // ATT_TQ
    rows = lambda b, i: (b * nq + i, 0)
    next_step = lambda b, i: jnp.minimum(b * nq + i + 1, BATCH * nq - 1)
    seg = lambda c: pl.BlockSpec((ATT_TQ, BM_SEG), lambda b, i: (b * nq + i, c))
    tile = pl.BlockSpec((ATT_TQ, D_MODEL), rows)

    def resident(a):
        return pl.BlockSpec(a.shape, lambda b, i: (0,) * a.ndim, pipeline_mode=pl.Buffered(1))

    return pl.pallas_call(
        _attn_kernel,
        grid=(BATCH, nq),
        in_specs=[
            pl.BlockSpec((MLA_HEADS * ATT_D, ATT_TQ), lambda b, i: (0, b * nq + i)),
            pl.BlockSpec((SEQ, MLA_HEADS * ATT_D), lambda b, i: (b, 0)),
            pl.BlockSpec((MLA_WIDTH, SEQ), lambda b, i: (0, b)),
            pl.BlockSpec((MLA_HEADS * ATT_D, ATT_TQ), lambda b, i: (0, next_step(b, i))),
            pl.BlockSpec((ATT_TK, MLA_HEADS * ATT_D),
                         lambda b, i: ((next_step(b, i) // nq) * (SEQ // ATT_TK), 0)),
            seg(COL_ZB), seg(COL_GB), resident(wpb),
            tile, seg(COL_U), seg(COL_V), seg(COL_ZA), seg(COL_GA), tile,
            resident(lng), resident(lnb), resident(ws), resident(bs_col),
            resident(wpa), resident(wo), resident(pg),
        ],
        out_specs=tile,
        out_shape=jax.ShapeDtypeStruct((TOKENS, D_MODEL), _F32),
        scratch_shapes=[
            pltpu.VMEM((MLA_HEADS, ATT_TK, ATT_TQ), _F32),
            pltpu.VMEM((MLA_HEADS, 1, ATT_TQ), _F32),
            pltpu.VMEM((MLA_HEADS, 1, ATT_TQ), _F32),
            pltpu.VMEM((MLA_HEADS, 1, ATT_TQ), _F32),
            pltpu.VMEM((MLA_HEADS, MLA_VDIM, ATT_TQ), _F32),
            pltpu.VMEM((ATT_TQ, MLA_WIDTH), _BF16),
            pltpu.VMEM((ATT_TQ, GM_WIDTH), _BF16),
        ],
        compiler_params=_params("arbitrary", "arbitrary"),
        name="mla_attn_merge",
    )(qt, k, vt, qt, k, p_bm, p_bm, wpb, x2d, p_bm, p_bm, p_bm, p_bm, cc,
      lng, lnb, ws, bs_col, wpa, wo, pg)


def _layer_weights(l, w_in, mla_w_uq, mla_w_ukv, lru_w_a, lru_w_x, w_proj_a, w_proj_b, w_proj_c,
                   w_out):
    cuts = [0]
    for s in IN_SIZES:
        cuts.append(cuts[-1] + s)
    w_in_t = jnp.swapaxes(w_in, 1, 2)
    seg = lambda k: lax.slice(w_in_t, (l, cuts[k], 0), (l + 1, cuts[k + 1], D_MODEL)).reshape(
        cuts[k + 1] - cuts[k], D_MODEL)
    (u, v, z_a, c_q, c_kv, k_rope, z_b, x_c, z_c, g_a, g_b, g_c) = [seg(k) for k in range(12)]
    half = MLA_ROPE // 2
    k_rope_rot = jnp.concatenate([-k_rope[half:, :], k_rope[:half, :]], axis=0)
    w_bm = jnp.concatenate([u, v, z_a, z_b, g_a, g_b, c_q, c_kv, k_rope, k_rope_rot],
                           axis=0).astype(_BF16)

    wq = mla_w_uq[l].reshape(MLA_Q_RANK, MLA_HEADS, MLA_QK_DIM)
    wq_nope = wq[:, :, :MLA_NOPE].reshape(MLA_Q_RANK, -1)
    wq_rope = wq[:, :, MLA_NOPE:].reshape(MLA_Q_RANK, -1)
    wq_rope_rot = jnp.concatenate([-wq[:, :, MLA_NOPE + half:], wq[:, :, MLA_NOPE:MLA_NOPE + half]],
                                  axis=2).reshape(MLA_Q_RANK, -1)
    wq_all = jnp.concatenate([wq_nope, wq_rope, wq_rope_rot], axis=1).T.astype(_BF16)

    wkv = mla_w_ukv[l].reshape(MLA_KV_RANK, MLA_HEADS, MLA_NOPE + MLA_VDIM)
    wkn = wkv[:, :, :MLA_NOPE].reshape(MLA_KV_RANK, -1).astype(_BF16)
    wvt = wkv[:, :, MLA_NOPE:].reshape(MLA_KV_RANK, -1).T.astype(_BF16)

    def block_diag(w):
        eye = jnp.eye(LRU_BLOCKS, dtype=w.dtype)
        dense = jnp.einsum('hij,hk->hikj', w, eye)
        return dense.reshape(LRU_WIDTH, LRU_WIDTH).astype(_BF16)

    wa_dense = block_diag(lru_w_a[l])
    wx_dense = block_diag(lru_w_x[l])
    k_max = max(k1 - k0 for k0, k1 in map(_gate_k_range, range(LRU_WIDTH // GATE_TN)))
    wax = []
    for c in range(LRU_WIDTH // GATE_TN):
        k0, k1 = _gate_k_range(c)
        cs = slice(c * GATE_TN, (c + 1) * GATE_TN)
        tile = jnp.concatenate([wa_dense[k0:k1, cs], wx_dense[k0:k1, cs]], axis=1)
        wax.append(jnp.pad(tile, ((0, k_max - (k1 - k0)), (0, 0))))
    wax = jnp.stack(wax)

    return dict(w_bm=w_bm, wxc=x_c.astype(_BF16), wz=z_c.astype(_BF16), wg=g_c.astype(_BF16),
                wq=wq_all, wkn=wkn, wvt=wvt, wax=wax,
                wpa=w_proj_a[l].astype(_BF16), wpb=w_proj_b[l].astype(_BF16),
                wpc=w_proj_c[l].astype(_BF16), wo=w_out[l].astype(_BF16))


def _rope_tables():
    pos = jnp.arange(SEQ, dtype=_F32)
    inv_freq = ROPE_THETA ** (-jnp.arange(0, MLA_ROPE, 2, dtype=_F32) / MLA_ROPE)
    ang = pos[:, None] * inv_freq[None, :]
    cos = jnp.cos(ang)
    sin = jnp.sin(ang)
    cos128t = jnp.tile(cos, (1, 4)).T
    sin128t = jnp.tile(sin, (1, 4)).T
    csk = jnp.concatenate([cos, cos, sin, sin], axis=1)
    return cos128t, sin128t, csk


def kernel(x, pre_norm_g, w_in, gm_ln_g, gm_ln_b, gm_ws, gm_bs, mla_q_norm_g, mla_w_uq,
           mla_kv_norm_g, mla_w_ukv, lru_conv_w, lru_conv_b, lru_w_a, lru_b_a, lru_w_x,
           lru_b_x, lru_lambda, w_proj_a, w_proj_b, w_proj_c, w_out, post_norm_g):
    cos128t, sin128t, csk = _rope_tables()
    x2d = x.reshape(TOKENS, D_MODEL)
    row = lambda a: a.reshape(1, -1)
    for l in range(DEPTH):
        w = _layer_weights(l, w_in, mla_w_uq, mla_w_ukv, lru_w_a, lru_w_x, w_proj_a, w_proj_b,
                           w_proj_c, w_out)
        g_pre = row(pre_norm_g[l])
        p_bm = _proj(x2d, g_pre, w["w_bm"])

        cc = _lru_mixer(x2d.reshape(BATCH, SEQ, D_MODEL), g_pre, w["wxc"], w["wz"], w["wg"],
                        lru_conv_w[l], row(lru_conv_b[l]), w["wax"], row(lru_b_a[l]),
                        row(lru_b_x[l]), row(lru_lambda[l]), w["wpc"]).reshape(TOKENS, D_MODEL)

        qt, k, vt = _mla_prep(p_bm, cos128t, sin128t, csk, row(mla_q_norm_g[l]),
                              row(mla_kv_norm_g[l]), w["wq"], w["wkn"], w["wvt"])
        x2d = _attention_merge(qt, k, vt, p_bm, x2d, cc, w["wpb"], row(gm_ln_g[l]),
                               row(gm_ln_b[l]), gm_ws[l], gm_bs[l][:, :, None], w["wpa"],
                               w["wo"], row(post_norm_g[l]))
    return x2d.reshape(BATCH, SEQ, D_MODEL)
```

```python
import math

import jax
import jax.numpy as jnp
from jax import lax
from jax.experimental import pallas as pl
from jax.experimental.pallas import tpu as pltpu

D_MODEL = 1024
BATCH = 8
SEQ = 2048
DEPTH = 2
TOKENS = BATCH * SEQ
CHUNK = 64
EPS = 1e-6

GM_WIDTH = 1024
GM_GROUPS = 4
GM_BLOCK = 128
GM_GW = GM_WIDTH // GM_GROUPS

MLA_HEADS = 8
MLA_NOPE = 128
MLA_ROPE = 64
MLA_VDIM = 128
MLA_QK_DIM = MLA_NOPE + MLA_ROPE
MLA_Q_RANK = 384
MLA_KV_RANK = 256
MLA_WIDTH = MLA_HEADS * MLA_VDIM
ROPE_THETA = 10000.0

LRU_WIDTH = 1280
LRU_BLOCKS = 16
LRU_BW = LRU_WIDTH // LRU_BLOCKS
LRU_C = 8.0
CONV_W = 4

IN_SIZES = (GM_WIDTH, GM_WIDTH, GM_WIDTH, MLA_Q_RANK, MLA_KV_RANK, MLA_ROPE, MLA_WIDTH,
            LRU_WIDTH, LRU_WIDTH, D_MODEL, D_MODEL, D_MODEL)

BM_SEG = 1024
(COL_U, COL_V, COL_ZA, COL_ZB, COL_GA, COL_GB) = range(6)
BM_SEG_KINDS = ("raw", "ln", "silu", "silu", "sigmoid", "sigmoid")
LATENT_W = MLA_Q_RANK + MLA_KV_RANK + 2 * MLA_ROPE
N_BM = 6 * BM_SEG + LATENT_W
COL_LATENT = (6 * BM_SEG) // LATENT_W

VMEM_LIMIT_BYTES = 56 * 1024 * 1024

PROJ_TM = 1024
PROJ_TN = 2304
LRU_TS = 64
GATE_TN = 256
PREP_TM = 512
ATT_TQ = 256
ATT_TK = 256
ATT_D = 256

_F32 = jnp.float32
_BF16 = jnp.bfloat16


def _params(*sem):
    return pltpu.CompilerParams(dimension_semantics=sem, vmem_limit_bytes=VMEM_LIMIT_BYTES)


def _sigmoid(x):
    return 1.0 / (1.0 + jnp.exp(-x))


def _silu(x):
    return x * _sigmoid(x)


def _rmsnorm_bf16(x, g):
    ms = jnp.mean(x * x, axis=-1, keepdims=True)
    return (x * lax.rsqrt(ms + EPS) * g).astype(_BF16)


def _dot_nt(a, b_t):
    return lax.dot_general(a, b_t, (((1,), (1,)), ((), ())), preferred_element_type=_F32)


def _proj_plan(step):
    lo, hi = step * PROJ_TN, (step + 1) * PROJ_TN
    segments = [(s * BM_SEG, (s + 1) * BM_SEG, kind) for s, kind in enumerate(BM_SEG_KINDS)]
    segments.append((len(BM_SEG_KINDS) * BM_SEG, N_BM, "raw"))
    plan = []
    for c0, c1, kind in segments:
        a, b = max(c0, lo), min(c1, hi)
        if a < b:
            assert kind != "ln" or (a, b) == (c0, c1), "layernorm needs its whole segment in one step"
            plan.append((a - lo, b - lo, kind))
    return plan


def _proj_kernel(x_ref, g_ref, lng_ref, lnb_ref, wt_ref, o_ref, h_scr):
    j = pl.program_id(1)

    @pl.when(j == 0)
    def _():
        h_scr[...] = _rmsnorm_bf16(x_ref[...], g_ref[...])

    def piece(c0, c1, kind):
        r = _dot_nt(h_scr[...], wt_ref[c0:c1, :])
        if kind == "ln":
            mu = jnp.mean(r, axis=-1, keepdims=True)
            rc = r - mu
            var = jnp.mean(rc * rc, axis=-1, keepdims=True)
            r = rc * lax.rsqrt(var + EPS) * lng_ref[...] + lnb_ref[...]
        elif kind == "silu":
            r = _silu(r)
        elif kind == "sigmoid":
            r = _sigmoid(r)
        o_ref[:, c0:c1] = r.astype(o_ref.dtype)

    for step in range(N_BM // PROJ_TN):
        @pl.when(j == step)
        def _(step=step):
            for c0, c1, kind in _proj_plan(step):
                piece(c0, c1, kind)


def _proj(x2d, g, lng, lnb, wt):
    vec = pl.BlockSpec((1, D_MODEL), lambda i, j: (0, 0))
    return pl.pallas_call(
        _proj_kernel,
        grid=(TOKENS // PROJ_TM, N_BM // PROJ_TN),
        in_specs=[
            pl.BlockSpec((PROJ_TM, D_MODEL), lambda i, j: (i, 0)),
            vec, vec, vec,
            pl.BlockSpec((PROJ_TN, D_MODEL), lambda i, j: (j, 0)),
        ],
        out_specs=pl.BlockSpec((PROJ_TM, PROJ_TN), lambda i, j: (i, j)),
        out_shape=jax.ShapeDtypeStruct((TOKENS, N_BM), _BF16),
        scratch_shapes=[pltpu.VMEM((PROJ_TM, D_MODEL), _BF16)],
        compiler_params=_params("arbitrary", "arbitrary"),
        name="proj_bm",
    )(x2d, g, lng, lnb, wt)


def _gate_k_range(c):
    first_block = (c * GATE_TN) // LRU_BW
    last_block = ((c + 1) * GATE_TN - 1) // LRU_BW
    k0 = (first_block * LRU_BW) // GATE_TN * GATE_TN
    k1 = -(-((last_block + 1) * LRU_BW) // GATE_TN) * GATE_TN
    return k0, k1


def _lru_kernel(x_ref, pg_ref, wxc_ref, wz_ref, wg_ref, cw_ref, cb_ref, wax_ref, ba_ref, bx_ref,
                lam_ref, wp_ref, o_ref, xext, a_scr, b_scr, h_scr, sz_scr, sg_scr):
    rows = LRU_TS * BATCH
    halo = (CONV_W - 1) * BATCH

    @pl.when(pl.program_id(0) == 0)
    def _():
        xext[0:halo, :] = jnp.zeros((halo, LRU_WIDTH), _F32)
        h_scr[...] = jnp.zeros_like(h_scr)

    xt = jnp.swapaxes(x_ref[...], 0, 1).reshape(rows, D_MODEL)
    hn = _rmsnorm_bf16(xt, pg_ref[...])

    xext[halo:halo + rows, :] = _dot_nt(hn, wxc_ref[...])
    xc = cb_ref[...] + cw_ref[CONV_W - 1:CONV_W, :] * xext[halo:halo + rows, :]
    for k in range(CONV_W - 1):
        xc = xc + cw_ref[k:k + 1, :] * xext[k * BATCH:k * BATCH + rows, :]
    xext[0:halo, :] = xext[rows:rows + halo, :]

    xcb = xc.astype(_BF16)
    lam = lam_ref[...]
    softplus_neg_lam = jnp.maximum(-lam, 0.0) + jnp.log(1.0 + jnp.exp(-jnp.abs(lam)))
    for c in range(LRU_WIDTH // GATE_TN):
        cs = slice(c * GATE_TN, (c + 1) * GATE_TN)
        k0, k1 = _gate_k_range(c)
        ri = jnp.dot(xcb[:, k0:k1], wax_ref[c, 0:k1 - k0, :], preferred_element_type=_F32)
        r = _sigmoid(ri[:, 0:GATE_TN] + ba_ref[:, cs])
        i = _sigmoid(ri[:, GATE_TN:2 * GATE_TN] + bx_ref[:, cs])
        a = jnp.exp((-LRU_C) * r * softplus_neg_lam[:, cs])
        mult = jnp.sqrt(jnp.maximum(1.0 - a * a, 0.0))
        a_scr[:, cs] = a
        b_scr[:, cs] = mult * (i * xc[:, cs])

    sz_scr[...] = _silu(_dot_nt(hn, wz_ref[...]))
    sg_scr[...] = _sigmoid(_dot_nt(hn, wg_ref[...]))

    h = h_scr[...]
    for s in range(LRU_TS):
        rs = slice(s * BATCH, (s + 1) * BATCH)
        h = a_scr[rs, :] * h + b_scr[rs, :]
        b_scr[rs, :] = h
    h_scr[...] = h

    y = (b_scr[...] * sz_scr[...]).astype(_BF16)
    cc = sg_scr[...] * jnp.dot(y, wp_ref[...], preferred_element_type=_F32)
    o_ref[...] = jnp.swapaxes(cc.reshape(LRU_TS, BATCH, D_MODEL), 0, 1).astype(o_ref.dtype)


def _lru_mixer(x3d, pre_g, wxc, wz, wg, conv_w, conv_b, wax, ba, bx, lam, wp):
    rows = LRU_TS * BATCH
    halo = (CONV_W - 1) * BATCH

    def resident(a):
        return pl.BlockSpec(a.shape, lambda t: (0,) * a.ndim, pipeline_mode=pl.Buffered(1))

    return pl.pallas_call(
        _lru_kernel,
        grid=(SEQ // LRU_TS,),
        in_specs=[pl.BlockSpec((BATCH, LRU_TS, D_MODEL), lambda t: (0, t, 0))]
        + [resident(a) for a in (pre_g, wxc, wz, wg, conv_w, conv_b, wax, ba, bx, lam, wp)],
        out_specs=pl.BlockSpec((BATCH, LRU_TS, D_MODEL), lambda t: (0, t, 0)),
        out_shape=jax.ShapeDtypeStruct((BATCH, SEQ, D_MODEL), _BF16),
        scratch_shapes=[
            pltpu.VMEM((rows + halo, LRU_WIDTH), _F32),
            pltpu.VMEM((rows, LRU_WIDTH), _F32),
            pltpu.VMEM((rows, LRU_WIDTH), _F32),
            pltpu.VMEM((BATCH, LRU_WIDTH), _F32),
            pltpu.VMEM((rows, LRU_WIDTH), _F32),
            pltpu.VMEM((rows, D_MODEL), _F32),
        ],
        compiler_params=_params("arbitrary"),
        name="lru_mixer",
    )(x3d, pre_g, wxc, wz, wg, conv_w, conv_b, wax, ba, bx, lam, wp)


def _mla_prep_kernel(c_ref, cost_ref, sint_ref, csk_ref, qg_ref, kvg_ref, wqt_ref, wkn_ref, wvt_ref,
                     qt_ref, k_ref, vt_ref):
    qscale = math.log2(math.e) / math.sqrt(MLA_QK_DIM)
    c = c_ref[...].astype(_F32)
    cq = c[:, 0:MLA_Q_RANK]
    ckv = c[:, MLA_Q_RANK:MLA_Q_RANK + MLA_KV_RANK]
    krk = c[:, MLA_Q_RANK + MLA_KV_RANK:LATENT_W]

    hq = (cq * lax.rsqrt(jnp.mean(cq * cq, axis=-1, keepdims=True) + EPS)
          * qg_ref[...]).astype(_BF16)
    qt = lax.dot_general(wqt_ref[...], hq, (((1,), (1,)), ((), ())), preferred_element_type=_F32)
    nope_w = MLA_HEADS * MLA_NOPE
    rope_w = MLA_HEADS * MLA_ROPE
    cost = cost_ref[...]
    sint = sint_ref[...]
    for p in range(rope_w // 128):
        qr = qt[nope_w + 128 * p:nope_w + 128 * (p + 1), :]
        qrr = qt[nope_w + rope_w + 128 * p:nope_w + rope_w + 128 * (p + 1), :]
        q_pair = ((qr * cost + qrr * sint) * qscale).astype(_BF16)
        for h in (2 * p, 2 * p + 1):
            qt_ref[ATT_D * h:ATT_D * h + MLA_NOPE, :] = (
                qt[MLA_NOPE * h:MLA_NOPE * (h + 1), :] * qscale).astype(_BF16)
            qt_ref[ATT_D * h + MLA_NOPE:ATT_D * (h + 1), :] = q_pair

    hkv = (ckv * lax.rsqrt(jnp.mean(ckv * ckv, axis=-1, keepdims=True) + EPS)
           * kvg_ref[...]).astype(_BF16)
    kn = jnp.dot(hkv, wkn_ref[...], preferred_element_type=_F32)
    vt_ref[...] = lax.dot_general(wvt_ref[...], hkv, (((1,), (1,)), ((), ())),
                                  preferred_element_type=_F32).astype(_BF16)

    t = krk * csk_ref[...]
    kf2 = t + pltpu.roll(t, MLA_ROPE, 1)
    lane = lax.broadcasted_iota(jnp.int32, kf2.shape, 1)
    kr_even = jnp.where(lane < MLA_ROPE, kf2, 0.0).astype(_BF16)
    kr_odd = jnp.where(lane >= MLA_ROPE, kf2, 0.0).astype(_BF16)
    for h in range(MLA_HEADS):
        k_ref[:, ATT_D * h:ATT_D * h + MLA_NOPE] = kn[:, MLA_NOPE * h:MLA_NOPE * (h + 1)].astype(_BF16)
        k_ref[:, ATT_D * h + MLA_NOPE:ATT_D * (h + 1)] = kr_even if h % 2 == 0 else kr_odd


def _mla_prep(p_bm, cos128t, sin128t, csk, qg, kvg, wqt, wkn, wvt):
    tiles_per_seq = SEQ // PREP_TM
    const = lambda i: (0, 0)
    pos = lambda i: (i % tiles_per_seq, 0)
    pos_t = lambda i: (0, i % tiles_per_seq)
    return pl.pallas_call(
        _mla_prep_kernel,
        grid=(TOKENS // PREP_TM,),
        in_specs=[
            pl.BlockSpec((PREP_TM, LATENT_W), lambda i: (i, COL_LATENT)),
            pl.BlockSpec((128, PREP_TM), pos_t),
            pl.BlockSpec((128, PREP_TM), pos_t),
            pl.BlockSpec((PREP_TM, 128), pos),
            pl.BlockSpec((1, MLA_Q_RANK), const),
            pl.BlockSpec((1, MLA_KV_RANK), const),
            pl.BlockSpec(wqt.shape, const),
            pl.BlockSpec(wkn.shape, const),
            pl.BlockSpec(wvt.shape, const),
        ],
        out_specs=[
            pl.BlockSpec((MLA_HEADS * ATT_D, PREP_TM), lambda i: (0, i)),
            pl.BlockSpec((PREP_TM, MLA_HEADS * ATT_D), lambda i: (i, 0)),
            pl.BlockSpec((MLA_WIDTH, PREP_TM), lambda i: (0, i)),
        ],
        out_shape=[
            jax.ShapeDtypeStruct((MLA_HEADS * ATT_D, TOKENS), _BF16),
            jax.ShapeDtypeStruct((TOKENS, MLA_HEADS * ATT_D), _BF16),
            jax.ShapeDtypeStruct((MLA_WIDTH, TOKENS), _BF16),
        ],
        compiler_params=_params("arbitrary"),
        name="mla_prep",
    )(p_bm, cos128t, sin128t, csk, qg, kvg, wqt, wkn, wvt)


def _attn_kernel(qt_ref, k_ref, vt_ref, qt_next_ref, k_next_ref, szb_ref, sgb_ref, wp_ref,
                 x_ref, u_ref, vln_ref, sza_ref, sga_ref, cc_ref, ws_ref, bs_ref,
                 wpa_ref, wo_ref, pg_ref, o_ref,
                 st_scr, mt_scr, m_scr, l_scr, acc_scr, y_scr, ya_scr):
    qi = pl.program_id(1)
    key_chunk = lax.broadcasted_iota(jnp.int32, (ATT_TK, ATT_TQ), 0) // CHUNK
    query_chunk = lax.broadcasted_iota(jnp.int32, (ATT_TK, ATT_TQ), 1) // CHUNK
    diag_mask = key_chunk <= query_chunk

    m_scr[...] = jnp.full(m_scr.shape, -1e30, _F32)
    l_scr[...] = jnp.zeros(l_scr.shape, _F32)
    acc_scr[...] = jnp.zeros(acc_scr.shape, _F32)

    def store_scores(k_tile, q_t, h):
        st = jnp.dot(k_tile, q_t, preferred_element_type=_F32)
        st_scr[h] = st
        mt_scr[h] = jnp.max(st, axis=0, keepdims=True)

    def scores(j, h):
        off = pl.multiple_of(j * ATT_TK, ATT_TK)
        ds_ = slice(ATT_D * h, ATT_D * (h + 1))
        store_scores(k_ref[pl.ds(off, ATT_TK), ds_], qt_ref[ds_, :], h)

    def accumulate(j, h, masked):
        off = pl.multiple_of(j * ATT_TK, ATT_TK)
        st = st_scr[h]
        if masked:
            st = jnp.where(diag_mask, st, -1e30)
            mt = jnp.max(st, axis=0, keepdims=True)
        else:
            mt = mt_scr[h]
        m_prev = m_scr[h]
        m_new = jnp.maximum(m_prev, mt)
        alpha = jnp.exp2(m_prev - m_new)
        p = jnp.exp2(st - m_new)
        l_scr[h] = alpha * l_scr[h] + jnp.sum(p, axis=0, keepdims=True)
        vt = vt_ref[MLA_VDIM * h:MLA_VDIM * (h + 1), pl.ds(off, ATT_TK)]
        acc_scr[h] = alpha * acc_scr[h] + jnp.dot(vt, p.astype(_BF16), preferred_element_type=_F32)
        m_scr[h] = m_new

    @pl.when((pl.program_id(0) == 0) & (qi == 0))
    def _():
        for h in range(MLA_HEADS):
            scores(0, h)

    def body(j, carry):
        for h in range(MLA_HEADS):
            accumulate(j, h, False)
            scores(j + 1, h)
        return carry

    lax.fori_loop(0, qi, body, 0)
    for h in range(MLA_HEADS):
        accumulate(qi, h, True)
        ds_ = slice(ATT_D * h, ATT_D * (h + 1))
        store_scores(k_next_ref[:, ds_], qt_next_ref[ds_, :], h)

    for h in range(MLA_HEADS):
        hs = slice(MLA_VDIM * h, MLA_VDIM * (h + 1))
        o = (acc_scr[h] * (1.0 / l_scr[h])).T
        y_scr[:, hs] = (o * szb_ref[:, hs].astype(_F32)).astype(_BF16)

    cb = sgb_ref[...].astype(_F32) * jnp.dot(y_scr[...], wp_ref[...], preferred_element_type=_F32)

    idx_r = lax.broadcasted_iota(jnp.int32, (GM_BLOCK, GM_BLOCK), 0) // CHUNK
    idx_c = lax.broadcasted_iota(jnp.int32, (GM_BLOCK, GM_BLOCK), 1) // CHUNK
    causal = idx_c <= idx_r

    for g in range(GM_GROUPS):
        ws = jnp.where(causal, ws_ref[g], 0.0).astype(_BF16)
        cs = slice(GM_GW * g, GM_GW * (g + 1))
        for r in range(ATT_TQ // GM_BLOCK):
            rs = slice(GM_BLOCK * r, GM_BLOCK * (r + 1))
            sv = jnp.dot(ws, vln_ref[rs, cs], preferred_element_type=_F32) + bs_ref[g]
            y = u_ref[rs, cs].astype(_F32) * sv * sza_ref[rs, cs].astype(_F32)
            ya_scr[rs, cs] = y.astype(_BF16)

    ca = sga_ref[...].astype(_F32) * jnp.dot(ya_scr[...], wpa_ref[...], preferred_element_type=_F32)
    merged = ca + cb + cc_ref[...].astype(_F32)
    o = jnp.dot(merged.astype(_BF16), wo_ref[...], preferred_element_type=_F32)
    o = o * lax.rsqrt(jnp.mean(o * o, axis=-1, keepdims=True) + EPS) * pg_ref[...]
    o_ref[...] = x_ref[...] + o


def _attention_merge(qt, k, vt, p_bm, x2d, cc, wpb, ws, bs_col, wpa, wo, pg):
    nq = SEQ // ATT_TQ
    rows = lambda b, i: (b * nq + i, 0)
    next_step = lambda b, i: jnp.minimum(b * nq + i + 1, BATCH * nq - 1)
    seg = lambda c: pl.BlockSpec((ATT_TQ, BM_SEG), lambda b, i: (b * nq + i, c))
    tile = pl.BlockSpec((ATT_TQ, D_MODEL), rows)

    def resident(a):
        return pl.BlockSpec(a.shape, lambda b, i: (0,) * a.ndim, pipeline_mode=pl.Buffered(1))

    return pl.pallas_call(
        _attn_kernel,
        grid=(BATCH, nq),
        in_specs=[
            pl.BlockSpec((MLA_HEADS * ATT_D, ATT_TQ), lambda b, i: (0, b * nq + i)),
            pl.BlockSpec((SEQ, MLA_HEADS * ATT_D), lambda b, i: (b, 0)),
            pl.BlockSpec((MLA_WIDTH, SEQ), lambda b, i: (0, b)),
            pl.BlockSpec((MLA_HEADS * ATT_D, ATT_TQ), lambda b, i: (0, next_step(b, i))),
            pl.BlockSpec((ATT_TK, MLA_HEADS * ATT_D),
                         lambda b, i: ((next_step(b, i) // nq) * (SEQ // ATT_TK), 0)),
            seg(COL_ZB), seg(COL_GB), resident(wpb),
            tile, seg(COL_U), seg(COL_V), seg(COL_ZA), seg(COL_GA), tile,
            resident(ws), resident(bs_col),
            resident(wpa), resident(wo), resident(pg),
        ],
        out_specs=tile,
        out_shape=jax.ShapeDtypeStruct((TOKENS, D_MODEL), _F32),
        scratch_shapes=[
            pltpu.VMEM((MLA_HEADS, ATT_TK, ATT_TQ), _F32),
            pltpu.VMEM((MLA_HEADS, 1, ATT_TQ), _F32),
            pltpu.VMEM((MLA_HEADS, 1, ATT_TQ), _F32),
            pltpu.VMEM((MLA_HEADS, 1, ATT_TQ), _F32),
            pltpu.VMEM((MLA_HEADS, MLA_VDIM, ATT_TQ), _F32),
            pltpu.VMEM((ATT_TQ, MLA_WIDTH), _BF16),
            pltpu.VMEM((ATT_TQ, GM_WIDTH), _BF16),
        ],
        compiler_params=_params("arbitrary", "arbitrary"),
        name="mla_attn_merge",
    )(qt, k, vt, qt, k, p_bm, p_bm, wpb, x2d, p_bm, p_bm, p_bm, p_bm, cc,
      ws, bs_col, wpa, wo, pg)


def _layer_weights(l, w_in, mla_w_uq, mla_w_ukv, lru_w_a, lru_w_x, w_proj_a, w_proj_b, w_proj_c,
                   w_out):
    cuts = [0]
    for s in IN_SIZES:
        cuts.append(cuts[-1] + s)
    w_in_t = jnp.swapaxes(w_in, 1, 2)
    seg = lambda k: lax.slice(w_in_t, (l, cuts[k], 0), (l + 1, cuts[k + 1], D_MODEL)).reshape(
        cuts[k + 1] - cuts[k], D_MODEL)
    (u, v, z_a, c_q, c_kv, k_rope, z_b, x_c, z_c, g_a, g_b, g_c) = [seg(k) for k in range(12)]
    half = MLA_ROPE // 2
    k_rope_rot = jnp.concatenate([-k_rope[half:, :], k_rope[:half, :]], axis=0)
    w_bm = jnp.concatenate([u, v, z_a, z_b, g_a, g_b, c_q, c_kv, k_rope, k_rope_rot],
                           axis=0).astype(_BF16)

    wq = mla_w_uq[l].reshape(MLA_Q_RANK, MLA_HEADS, MLA_QK_DIM)
    wq_nope = wq[:, :, :MLA_NOPE].reshape(MLA_Q_RANK, -1)
    wq_rope = wq[:, :, MLA_NOPE:].reshape(MLA_Q_RANK, -1)
    wq_rope_rot = jnp.concatenate([-wq[:, :, MLA_NOPE + half:], wq[:, :, MLA_NOPE:MLA_NOPE + half]],
                                  axis=2).reshape(MLA_Q_RANK, -1)
    wq_all = jnp.concatenate([wq_nope, wq_rope, wq_rope_rot], axis=1).T.astype(_BF16)

    wkv = mla_w_ukv[l].reshape(MLA_KV_RANK, MLA_HEADS, MLA_NOPE + MLA_VDIM)
    wkn = wkv[:, :, :MLA_NOPE].reshape(MLA_KV_RANK, -1).astype(_BF16)
    wvt = wkv[:, :, MLA_NOPE:].reshape(MLA_KV_RANK, -1).T.astype(_BF16)

    def block_diag(w):
        eye = jnp.eye(LRU_BLOCKS, dtype=w.dtype)
        dense = jnp.einsum('hij,hk->hikj', w, eye)
        return dense.reshape(LRU_WIDTH, LRU_WIDTH).astype(_BF16)

    wa_dense = block_diag(lru_w_a[l])
    wx_dense = block_diag(lru_w_x[l])
    k_max = max(k1 - k0 for k0, k1 in map(_gate_k_range, range(LRU_WIDTH // GATE_TN)))
    wax = []
    for c in range(LRU_WIDTH // GATE_TN):
        k0, k1 = _gate_k_range(c)
        cs = slice(c * GATE_TN, (c + 1) * GATE_TN)
        tile = jnp.concatenate([wa_dense[k0:k1, cs], wx_dense[k0:k1, cs]], axis=1)
        wax.append(jnp.pad(tile, ((0, k_max - (k1 - k0)), (0, 0))))
    wax = jnp.stack(wax)

    return dict(w_bm=w_bm, wxc=x_c.astype(_BF16), wz=z_c.astype(_BF16), wg=g_c.astype(_BF16),
                wq=wq_all, wkn=wkn, wvt=wvt, wax=wax,
                wpa=w_proj_a[l].astype(_BF16), wpb=w_proj_b[l].astype(_BF16),
                wpc=w_proj_c[l].astype(_BF16), wo=w_out[l].astype(_BF16))


def _rope_tables():
    pos = jnp.arange(SEQ, dtype=_F32)
    inv_freq = ROPE_THETA ** (-jnp.arange(0, MLA_ROPE, 2, dtype=_F32) / MLA_ROPE)
    ang = pos[:, None] * inv_freq[None, :]
    cos = jnp.cos(ang)
    sin = jnp.sin(ang)
    cos128t = jnp.tile(cos, (1, 4)).T
    sin128t = jnp.tile(sin, (1, 4)).T
    csk = jnp.concatenate([cos, cos, sin, sin], axis=1)
    return cos128t, sin128t, csk


def kernel(x, pre_norm_g, w_in, gm_ln_g, gm_ln_b, gm_ws, gm_bs, mla_q_norm_g, mla_w_uq,
           mla_kv_norm_g, mla_w_ukv, lru_conv_w, lru_conv_b, lru_w_a, lru_b_a, lru_w_x,
           lru_b_x, lru_lambda, w_proj_a, w_proj_b, w_proj_c, w_out, post_norm_g):
    cos128t, sin128t, csk = _rope_tables()
    x2d = x.reshape(TOKENS, D_MODEL)
    row = lambda a: a.reshape(1, -1)
    for l in range(DEPTH):
        w = _layer_weights(l, w_in, mla_w_uq, mla_w_ukv, lru_w_a, lru_w_x, w_proj_a, w_proj_b,
                           w_proj_c, w_out)
        g_pre = row(pre_norm_g[l])
        p_bm = _proj(x2d, g_pre, row(gm_ln_g[l]), row(gm_ln_b[l]), w["w_bm"])

        cc = _lru_mixer(x2d.reshape(BATCH, SEQ, D_MODEL), g_pre, w["wxc"], w["wz"], w["wg"],
                        lru_conv_w[l], row(lru_conv_b[l]), w["wax"], row(lru_b_a[l]),
                        row(lru_b_x[l]), row(lru_lambda[l]), w["wpc"]).reshape(TOKENS, D_MODEL)

        qt, k, vt = _mla_prep(p_bm, cos128t, sin128t, csk, row(mla_q_norm_g[l]),
                              row(mla_kv_norm_g[l]), w["wq"], w["wkn"], w["wvt"])
        x2d = _attention_merge(qt, k, vt, p_bm, x2d, cc, w["wpb"], gm_ws[l],
                               gm_bs[l][:, :, None], w["wpa"], w["wo"], row(post_norm_g[l]))
    return x2d.reshape(BATCH, SEQ, D_MODEL)
```

```python
import math

import jax
import jax.numpy as jnp
from jax import lax
from jax.experimental import pallas as pl
from jax.experimental.pallas import tpu as pltpu

D_MODEL = 1024
BATCH = 8
SEQ = 2048
DEPTH = 2
TOKENS = BATCH * SEQ
CHUNK = 64
EPS = 1e-6

GM_WIDTH = 1024
GM_GROUPS = 4
GM_BLOCK = 128
GM_GW = GM_WIDTH // GM_GROUPS

MLA_HEADS = 8
MLA_NOPE = 128
MLA_ROPE = 64
MLA_VDIM = 128
MLA_QK_DIM = MLA_NOPE + MLA_ROPE
MLA_Q_RANK = 384
MLA_KV_RANK = 256
MLA_WIDTH = MLA_HEADS * MLA_VDIM
ROPE_THETA = 10000.0

LRU_WIDTH = 1280
LRU_BLOCKS = 16
LRU_BW = LRU_WIDTH // LRU_BLOCKS
LRU_C = 8.0
CONV_W = 4

IN_SIZES = (GM_WIDTH, GM_WIDTH, GM_WIDTH, MLA_Q_RANK, MLA_KV_RANK, MLA_ROPE, MLA_WIDTH,
            LRU_WIDTH, LRU_WIDTH, D_MODEL, D_MODEL, D_MODEL)

BM_SEG = 1024
(COL_U, COL_V, COL_ZA, COL_ZB, COL_GA, COL_GB) = range(6)
BM_SEG_KINDS = ("raw", "ln", "silu", "silu", "sigmoid", "sigmoid")
LATENT_W = MLA_Q_RANK + MLA_KV_RANK + 2 * MLA_ROPE
N_BM = 6 * BM_SEG + LATENT_W
COL_LATENT = (6 * BM_SEG) // LATENT_W

VMEM_LIMIT_BYTES = 56 * 1024 * 1024

PROJ_TM = 1024
PROJ_TN = 2304
LRU_TS = 64
GATE_TN = 256
PREP_TM = 512
ATT_TQ = 256
ATT_TK = 256
ATT_D = 256

_F32 = jnp.float32
_BF16 = jnp.bfloat16


def _params(*sem):
    return pltpu.CompilerParams(dimension_semantics=sem, vmem_limit_bytes=VMEM_LIMIT_BYTES)


def _sigmoid(x):
    return 1.0 / (1.0 + jnp.exp(-x))


def _silu(x):
    return x * _sigmoid(x)


def _rmsnorm_bf16(x, g):
    ms = jnp.mean(x * x, axis=-1, keepdims=True)
    return (x * lax.rsqrt(ms + EPS) * g).astype(_BF16)


def _dot_nt(a, b_t):
    return lax.dot_general(a, b_t, (((1,), (1,)), ((), ())), preferred_element_type=_F32)


def _proj_plan(step):
    lo, hi = step * PROJ_TN, (step + 1) * PROJ_TN
    segments = [(s * BM_SEG, (s + 1) * BM_SEG, kind) for s, kind in enumerate(BM_SEG_KINDS)]
    segments.append((len(BM_SEG_KINDS) * BM_SEG, N_BM, "raw"))
    plan = []
    for c0, c1, kind in segments:
        a, b = max(c0, lo), min(c1, hi)
        if a < b:
            assert kind != "ln" or (a, b) == (c0, c1), "layernorm needs its whole segment in one step"
            plan.append((a - lo, b - lo, kind))
    return plan


def _proj_kernel(x_ref, g_ref, lng_ref, lnb_ref, wt_ref, o_ref, h_scr):
    j = pl.program_id(1)

    @pl.when(j == 0)
    def _():
        h_scr[...] = _rmsnorm_bf16(x_ref[...], g_ref[...])

    def piece(c0, c1, kind):
        r = _dot_nt(h_scr[...], wt_ref[c0:c1, :])
        if kind == "ln":
            mu = jnp.mean(r, axis=-1, keepdims=True)
            rc = r - mu
            var = jnp.mean(rc * rc, axis=-1, keepdims=True)
            r = rc * lax.rsqrt(var + EPS) * lng_ref[...] + lnb_ref[...]
        elif kind == "silu":
            r = _silu(r)
        elif kind == "sigmoid":
            r = _sigmoid(r)
        o_ref[:, c0:c1] = r.astype(o_ref.dtype)

    for step in range(N_BM // PROJ_TN):
        @pl.when(j == step)
        def _(step=step):
            for c0, c1, kind in _proj_plan(step):
                piece(c0, c1, kind)


def _proj(x2d, g, lng, lnb, wt):
    vec = pl.BlockSpec((1, D_MODEL), lambda i, j: (0, 0))
    return pl.pallas_call(
        _proj_kernel,
        grid=(TOKENS // PROJ_TM, N_BM // PROJ_TN),
        in_specs=[
            pl.BlockSpec((PROJ_TM, D_MODEL), lambda i, j: (i, 0)),
            vec, vec, vec,
            pl.BlockSpec((PROJ_TN, D_MODEL), lambda i, j: (j, 0)),
        ],
        out_specs=pl.BlockSpec((PROJ_TM, PROJ_TN), lambda i, j: (i, j)),
        out_shape=jax.ShapeDtypeStruct((TOKENS, N_BM), _BF16),
        scratch_shapes=[pltpu.VMEM((PROJ_TM, D_MODEL), _BF16)],
        compiler_params=_params("arbitrary", "arbitrary"),
        name="proj_bm",
    )(x2d, g, lng, lnb, wt)


def _gate_k_range(c):
    first_block = (c * GATE_TN) // LRU_BW
    last_block = ((c + 1) * GATE_TN - 1) // LRU_BW
    k0 = (first_block * LRU_BW) // GATE_TN * GATE_TN
    k1 = -(-((last_block + 1) * LRU_BW) // GATE_TN) * GATE_TN
    return k0, k1


GATE_K_MAX = max(k1 - k0 for k0, k1 in map(_gate_k_range, range(LRU_WIDTH // GATE_TN)))


def _gate_weights_kernel(wa_ref, wx_ref, o_ref, stage):
    for c in range(LRU_WIDTH // GATE_TN):
        k0, _ = _gate_k_range(c)
        stage[...] = jnp.zeros(stage.shape, _F32)
        for h in range(LRU_BLOCKS):
            lo = max(LRU_BW * h, GATE_TN * c)
            hi = min(LRU_BW * (h + 1), GATE_TN * (c + 1))
            if lo >= hi:
                continue
            r0 = LRU_BW * h - k0
            src = slice(lo - LRU_BW * h, hi - LRU_BW * h)
            dst = slice(lo - GATE_TN * c, hi - GATE_TN * c)
            stage[r0:r0 + LRU_BW, dst] = wa_ref[h][:, src]
            stage[r0:r0 + LRU_BW, GATE_TN + dst.start:GATE_TN + dst.stop] = wx_ref[h][:, src]
        o_ref[c] = stage[...].astype(o_ref.dtype)


def _gate_weights(w_a, w_x):
    n_tiles = LRU_WIDTH // GATE_TN
    blocks = pl.BlockSpec((None, LRU_BLOCKS, LRU_BW, LRU_BW), lambda l: (l, 0, 0, 0))
    return pl.pallas_call(
        _gate_weights_kernel,
        grid=(DEPTH,),
        in_specs=[blocks, blocks],
        out_specs=pl.BlockSpec((None, n_tiles, GATE_K_MAX, 2 * GATE_TN), lambda l: (l, 0, 0, 0)),
        out_shape=jax.ShapeDtypeStruct((DEPTH, n_tiles, GATE_K_MAX, 2 * GATE_TN), _BF16),
        scratch_shapes=[pltpu.VMEM((GATE_K_MAX, 2 * GATE_TN), _F32)],
        compiler_params=_params("arbitrary"),
        name="gate_weights",
    )(w_a, w_x)


def _lru_kernel(x_ref, pg_ref, wxc_ref, wz_ref, wg_ref, cw_ref, cb_ref, wax_ref, ba_ref, bx_ref,
                lam_ref, wp_ref, o_ref, xext, a_scr, b_scr, h_scr, sz_scr, sg_scr):
    rows = LRU_TS * BATCH
    halo = (CONV_W - 1) * BATCH

    @pl.when(pl.program_id(0) == 0)
    def _():
        xext[0:halo, :] = jnp.zeros((halo, LRU_WIDTH), _F32)
        h_scr[...] = jnp.zeros_like(h_scr)

    xt = jnp.swapaxes(x_ref[...], 0, 1).reshape(rows, D_MODEL)
    hn = _rmsnorm_bf16(xt, pg_ref[...])

    xext[halo:halo + rows, :] = _dot_nt(hn, wxc_ref[...])
    xc = cb_ref[...] + cw_ref[CONV_W - 1:CONV_W, :] * xext[halo:halo + rows, :]
    for k in range(CONV_W - 1):
        xc = xc + cw_ref[k:k + 1, :] * xext[k * BATCH:k * BATCH + rows, :]
    xext[0:halo, :] = xext[rows:rows + halo, :]

    xcb = xc.astype(_BF16)
    lam = lam_ref[...]
    softplus_neg_lam = jnp.maximum(-lam, 0.0) + jnp.log(1.0 + jnp.exp(-jnp.abs(lam)))
    for c in range(LRU_WIDTH // GATE_TN):
        cs = slice(c * GATE_TN, (c + 1) * GATE_TN)
        k0, k1 = _gate_k_range(c)
        ri = jnp.dot(xcb[:, k0:k1], wax_ref[c, 0:k1 - k0, :], preferred_element_type=_F32)
        r = _sigmoid(ri[:, 0:GATE_TN] + ba_ref[:, cs])
        i = _sigmoid(ri[:, GATE_TN:2 * GATE_TN] + bx_ref[:, cs])
        a = jnp.exp((-LRU_C) * r * softplus_neg_lam[:, cs])
        mult = jnp.sqrt(jnp.maximum(1.0 - a * a, 0.0))
        a_scr[:, cs] = a
        b_scr[:, cs] = mult * (i * xc[:, cs])

    sz_scr[...] = _silu(_dot_nt(hn, wz_ref[...]))
    sg_scr[...] = _sigmoid(_dot_nt(hn, wg_ref[...]))

    h = h_scr[...]
    for s in range(LRU_TS):
        rs = slice(s * BATCH, (s + 1) * BATCH)
        h = a_scr[rs, :] * h + b_scr[rs, :]
        b_scr[rs, :] = h
    h_scr[...] = h

    y = (b_scr[...] * sz_scr[...]).astype(_BF16)
    cc = sg_scr[...] * jnp.dot(y, wp_ref[...], preferred_element_type=_F32)
    o_ref[...] = jnp.swapaxes(cc.reshape(LRU_TS, BATCH, D_MODEL), 0, 1).astype(o_ref.dtype)


def _lru_mixer(x3d, pre_g, wxc, wz, wg, conv_w, conv_b, wax, ba, bx, lam, wp):
    rows = LRU_TS * BATCH
    halo = (CONV_W - 1) * BATCH

    def resident(a):
        return pl.BlockSpec(a.shape, lambda t: (0,) * a.ndim, pipeline_mode=pl.Buffered(1))

    return pl.pallas_call(
        _lru_kernel,
        grid=(SEQ // LRU_TS,),
        in_specs=[pl.BlockSpec((BATCH, LRU_TS, D_MODEL), lambda t: (0, t, 0))]
        + [resident(a) for a in (pre_g, wxc, wz, wg, conv_w, conv_b, wax, ba, bx, lam, wp)],
        out_specs=pl.BlockSpec((BATCH, LRU_TS, D_MODEL), lambda t: (0, t, 0)),
        out_shape=jax.ShapeDtypeStruct((BATCH, SEQ, D_MODEL), _BF16),
        scratch_shapes=[
            pltpu.VMEM((rows + halo, LRU_WIDTH), _F32),
            pltpu.VMEM((rows, LRU_WIDTH), _F32),
            pltpu.VMEM((rows, LRU_WIDTH), _F32),
            pltpu.VMEM((BATCH, LRU_WIDTH), _F32),
            pltpu.VMEM((rows, LRU_WIDTH), _F32),
            pltpu.VMEM((rows, D_MODEL), _F32),
        ],
        compiler_params=_params("arbitrary"),
        name="lru_mixer",
    )(x3d, pre_g, wxc, wz, wg, conv_w, conv_b, wax, ba, bx, lam, wp)


def _mla_prep_kernel(c_ref, cost_ref, sint_ref, csk_ref, qg_ref, kvg_ref, wqt_ref, wkn_ref, wvt_ref,
                     qt_ref, k_ref, vt_ref):
    qscale = math.log2(math.e) / math.sqrt(MLA_QK_DIM)
    c = c_ref[...].astype(_F32)
    cq = c[:, 0:MLA_Q_RANK]
    ckv = c[:, MLA_Q_RANK:MLA_Q_RANK + MLA_KV_RANK]
    krk = c[:, MLA_Q_RANK + MLA_KV_RANK:LATENT_W]

    hq = (cq * lax.rsqrt(jnp.mean(cq * cq, axis=-1, keepdims=True) + EPS)
          * qg_ref[...]).astype(_BF16)
    qt = lax.dot_general(wqt_ref[...], hq, (((1,), (1,)), ((), ())), preferred_element_type=_F32)
    nope_w = MLA_HEADS * MLA_NOPE
    rope_w = MLA_HEADS * MLA_ROPE
    cost = cost_ref[...]
    sint = sint_ref[...]
    for p in range(rope_w // 128):
        qr = qt[nope_w + 128 * p:nope_w + 128 * (p + 1), :]
        qrr = qt[nope_w + rope_w + 128 * p:nope_w + rope_w + 128 * (p + 1), :]
        q_pair = ((qr * cost + qrr * sint) * qscale).astype(_BF16)
        for h in (2 * p, 2 * p + 1):
            qt_ref[ATT_D * h:ATT_D * h + MLA_NOPE, :] = (
                qt[MLA_NOPE * h:MLA_NOPE * (h + 1), :] * qscale).astype(_BF16)
            qt_ref[ATT_D * h + MLA_NOPE:ATT_D * (h + 1), :] = q_pair

    hkv = (ckv * lax.rsqrt(jnp.mean(ckv * ckv, axis=-1, keepdims=True) + EPS)
           * kvg_ref[...]).astype(_BF16)
    kn = jnp.dot(hkv, wkn_ref[...], preferred_element_type=_F32)
    vt_ref[...] = lax.dot_general(wvt_ref[...], hkv, (((1,), (1,)), ((), ())),
                                  preferred_element_type=_F32).astype(_BF16)

    t = krk * csk_ref[...]
    kf2 = t + pltpu.roll(t, MLA_ROPE, 1)
    lane = lax.broadcasted_iota(jnp.int32, kf2.shape, 1)
    kr_even = jnp.where(lane < MLA_ROPE, kf2, 0.0).astype(_BF16)
    kr_odd = jnp.where(lane >= MLA_ROPE, kf2, 0.0).astype(_BF16)
    for h in range(MLA_HEADS):
        k_ref[:, ATT_D * h:ATT_D * h + MLA_NOPE] = kn[:, MLA_NOPE * h:MLA_NOPE * (h + 1)].astype(_BF16)
        k_ref[:, ATT_D * h + MLA_NOPE:ATT_D * (h + 1)] = kr_even if h % 2 == 0 else kr_odd


def _mla_prep(p_bm, cos128t, sin128t, csk, qg, kvg, wqt, wkn, wvt):
    tiles_per_seq = SEQ // PREP_TM
    const = lambda i: (0, 0)
    pos = lambda i: (i % tiles_per_seq, 0)
    pos_t = lambda i: (0, i % tiles_per_seq)
    return pl.pallas_call(
        _mla_prep_kernel,
        grid=(TOKENS // PREP_TM,),
        in_specs=[
            pl.BlockSpec((PREP_TM, LATENT_W), lambda i: (i, COL_LATENT)),
            pl.BlockSpec((128, PREP_TM), pos_t),
            pl.BlockSpec((128, PREP_TM), pos_t),
            pl.BlockSpec((PREP_TM, 128), pos),
            pl.BlockSpec((1, MLA_Q_RANK), const),
            pl.BlockSpec((1, MLA_KV_RANK), const),
            pl.BlockSpec(wqt.shape, const),
            pl.BlockSpec(wkn.shape, const),
            pl.BlockSpec(wvt.shape, const),
        ],
        out_specs=[
            pl.BlockSpec((MLA_HEADS * ATT_D, PREP_TM), lambda i: (0, i)),
            pl.BlockSpec((PREP_TM, MLA_HEADS * ATT_D), lambda i: (i, 0)),
            pl.BlockSpec((MLA_WIDTH, PREP_TM), lambda i: (0, i)),
        ],
        out_shape=[
            jax.ShapeDtypeStruct((MLA_HEADS * ATT_D, TOKENS), _BF16),
            jax.ShapeDtypeStruct((TOKENS, MLA_HEADS * ATT_D), _BF16),
            jax.ShapeDtypeStruct((MLA_WIDTH, TOKENS), _BF16),
        ],
        compiler_params=_params("arbitrary"),
        name="mla_prep",
    )(p_bm, cos128t, sin128t, csk, qg, kvg, wqt, wkn, wvt)


def _attn_kernel(qt_ref, k_ref, vt_ref, qt_next_ref, k_next_ref, szb_ref, sgb_ref, wp_ref,
                 x_ref, u_ref, vln_ref, sza_ref, sga_ref, cc_ref, ws_ref, bs_ref,
                 wpa_ref, wo_ref, pg_ref, o_ref,
                 st_scr, mt_scr, m_scr, l_scr, acc_scr, y_scr, ya_scr):
    qi = pl.program_id(1)
    key_chunk = lax.broadcasted_iota(jnp.int32, (ATT_TK, ATT_TQ), 0) // CHUNK
    query_chunk = lax.broadcasted_iota(jnp.int32, (ATT_TK, ATT_TQ), 1) // CHUNK
    diag_mask = key_chunk <= query_chunk

    m_scr[...] = jnp.full(m_scr.shape, -1e30, _F32)
    l_scr[...] = jnp.zeros(l_scr.shape, _F32)
    acc_scr[...] = jnp.zeros(acc_scr.shape, _F32)

    def store_scores(k_tile, q_t, h):
        st = jnp.dot(k_tile, q_t, preferred_element_type=_F32)
        st_scr[h] = st
        mt_scr[h] = jnp.max(st, axis=0, keepdims=True)

    def scores(j, h):
        off = pl.multiple_of(j * ATT_TK, ATT_TK)
        ds_ = slice(ATT_D * h, ATT_D * (h + 1))
        store_scores(k_ref[pl.ds(off, ATT_TK), ds_], qt_ref[ds_, :], h)

    def accumulate(j, h, masked):
        off = pl.multiple_of(j * ATT_TK, ATT_TK)
        st = st_scr[h]
        if masked:
            st = jnp.where(diag_mask, st, -1e30)
            mt = jnp.max(st, axis=0, keepdims=True)
        else:
            mt = mt_scr[h]
        m_prev = m_scr[h]
        m_new = jnp.maximum(m_prev, mt)
        alpha = jnp.exp2(m_prev - m_new)
        p = jnp.exp2(st - m_new)
        l_scr[h] = alpha * l_scr[h] + jnp.sum(p, axis=0, keepdims=True)
        vt = vt_ref[MLA_VDIM * h:MLA_VDIM * (h + 1), pl.ds(off, ATT_TK)]
        acc_scr[h] = alpha * acc_scr[h] + jnp.dot(vt, p.astype(_BF16), preferred_element_type=_F32)
        m_scr[h] = m_new

    @pl.when((pl.program_id(0) == 0) & (qi == 0))
    def _():
        for h in range(MLA_HEADS):
            scores(0, h)

    def body(j, carry):
        for h in range(MLA_HEADS):
            accumulate(j, h, False)
            scores(j + 1, h)
        return carry

    lax.fori_loop(0, qi, body, 0)
    for h in range(MLA_HEADS):
        accumulate(qi, h, True)
        ds_ = slice(ATT_D * h, ATT_D * (h + 1))
        store_scores(k_next_ref[:, ds_], qt_next_ref[ds_, :], h)

    for h in range(MLA_HEADS):
        hs = slice(MLA_VDIM * h, MLA_VDIM * (h + 1))
        o = (acc_scr[h] * (1.0 / l_scr[h])).T
        y_scr[:, hs] = (o * szb_ref[:, hs].astype(_F32)).astype(_BF16)

    cb = sgb_ref[...].astype(_F32) * jnp.dot(y_scr[...], wp_ref[...], preferred_element_type=_F32)

    idx_r = lax.broadcasted_iota(jnp.int32, (GM_BLOCK, GM_BLOCK), 0) // CHUNK
    idx_c = lax.broadcasted_iota(jnp.int32, (GM_BLOCK, GM_BLOCK), 1) // CHUNK
    causal = idx_c <= idx_r

    for g in range(GM_GROUPS):
        ws = jnp.where(causal, ws_ref[g], 0.0).astype(_BF16)
        cs = slice(GM_GW * g, GM_GW * (g + 1))
        for r in range(ATT_TQ // GM_BLOCK):
            rs = slice(GM_BLOCK * r, GM_BLOCK * (r + 1))
            sv = jnp.dot(ws, vln_ref[rs, cs], preferred_element_type=_F32) + bs_ref[g]
            y = u_ref[rs, cs].astype(_F32) * sv * sza_ref[rs, cs].astype(_F32)
            ya_scr[rs, cs] = y.astype(_BF16)

    ca = sga_ref[...].astype(_F32) * jnp.dot(ya_scr[...], wpa_ref[...], preferred_element_type=_F32)
    merged = ca + cb + cc_ref[...].astype(_F32)
    o = jnp.dot(merged.astype(_BF16), wo_ref[...], preferred_element_type=_F32)
    o = o * lax.rsqrt(jnp.mean(o * o, axis=-1, keepdims=True) + EPS) * pg_ref[...]
    o_ref[...] = x_ref[...] + o


def _attention_merge(qt, k, vt, p_bm, x2d, cc, wpb, ws, bs_col, wpa, wo, pg):
    nq = SEQ // ATT_TQ
    rows = lambda b, i: (b * nq + i, 0)
    next_step = lambda b, i: jnp.minimum(b * nq + i + 1, BATCH * nq - 1)
    seg = lambda c: pl.BlockSpec((ATT_TQ, BM_SEG), lambda b, i: (b * nq + i, c))
    tile = pl.BlockSpec((ATT_TQ, D_MODEL), rows)

    def resident(a):
        return pl.BlockSpec(a.shape, lambda b, i: (0,) * a.ndim, pipeline_mode=pl.Buffered(1))

    return pl.pallas_call(
        _attn_kernel,
        grid=(BATCH, nq),
        in_specs=[
            pl.BlockSpec((MLA_HEADS * ATT_D, ATT_TQ), lambda b, i: (0, b * nq + i)),
            pl.BlockSpec((SEQ, MLA_HEADS * ATT_D), lambda b, i: (b, 0)),
            pl.BlockSpec((MLA_WIDTH, SEQ), lambda b, i: (0, b)),
            pl.BlockSpec((MLA_HEADS * ATT_D, ATT_TQ), lambda b, i: (0, next_step(b, i))),
            pl.BlockSpec((ATT_TK, MLA_HEADS * ATT_D),
                         lambda b, i: ((next_step(b, i) // nq) * (SEQ // ATT_TK), 0)),
            seg(COL_ZB), seg(COL_GB), resident(wpb),
            tile, seg(COL_U), seg(COL_V), seg(COL_ZA), seg(COL_GA), tile,
            resident(ws), resident(bs_col),
            resident(wpa), resident(wo), resident(pg),
        ],
        out_specs=tile,
        out_shape=jax.ShapeDtypeStruct((TOKENS, D_MODEL), _F32),
        scratch_shapes=[
            pltpu.VMEM((MLA_HEADS, ATT_TK, ATT_TQ), _F32),
            pltpu.VMEM((MLA_HEADS, 1, ATT_TQ), _F32),
            pltpu.VMEM((MLA_HEADS, 1, ATT_TQ), _F32),
            pltpu.VMEM((MLA_HEADS, 1, ATT_TQ), _F32),
            pltpu.VMEM((MLA_HEADS, MLA_VDIM, ATT_TQ), _F32),
            pltpu.VMEM((ATT_TQ, MLA_WIDTH), _BF16),
            pltpu.VMEM((ATT_TQ, GM_WIDTH), _BF16),
        ],
        compiler_params=_params("arbitrary", "arbitrary"),
        name="mla_attn_merge",
    )(qt, k, vt, qt, k, p_bm, p_bm, wpb, x2d, p_bm, p_bm, p_bm, p_bm, cc,
      ws, bs_col, wpa, wo, pg)


def _layer_weights(l, w_in, mla_w_uq, mla_w_ukv, w_proj_a, w_proj_b, w_proj_c, w_out):
    cuts = [0]
    for s in IN_SIZES:
        cuts.append(cuts[-1] + s)
    w_in_t = jnp.swapaxes(w_in, 1, 2)
    seg = lambda k: lax.slice(w_in_t, (l, cuts[k], 0), (l + 1, cuts[k + 1], D_MODEL)).reshape(
        cuts[k + 1] - cuts[k], D_MODEL)
    (u, v, z_a, c_q, c_kv, k_rope, z_b, x_c, z_c, g_a, g_b, g_c) = [seg(k) for k in range(12)]
    half = MLA_ROPE // 2
    k_rope_rot = jnp.concatenate([-k_rope[half:, :], k_rope[:half, :]], axis=0)
    w_bm = jnp.concatenate([u, v, z_a, z_b, g_a, g_b, c_q, c_kv, k_rope, k_rope_rot],
                           axis=0).astype(_BF16)

    wq = mla_w_uq[l].reshape(MLA_Q_RANK, MLA_HEADS, MLA_QK_DIM)
    wq_nope = wq[:, :, :MLA_NOPE].reshape(MLA_Q_RANK, -1)
    wq_rope = wq[:, :, MLA_NOPE:].reshape(MLA_Q_RANK, -1)
    wq_rope_rot = jnp.concatenate([-wq[:, :, MLA_NOPE + half:], wq[:, :, MLA_NOPE:MLA_NOPE + half]],
                                  axis=2).reshape(MLA_Q_RANK, -1)
    wq_all = jnp.concatenate([wq_nope, wq_rope, wq_rope_rot], axis=1).T.astype(_BF16)

    wkv = mla_w_ukv[l].reshape(MLA_KV_RANK, MLA_HEADS, MLA_NOPE + MLA_VDIM)
    wkn = wkv[:, :, :MLA_NOPE].reshape(MLA_KV_RANK, -1).astype(_BF16)
    wvt = wkv[:, :, MLA_NOPE:].reshape(MLA_KV_RANK, -1).T.astype(_BF16)

    return dict(w_bm=w_bm, wxc=x_c.astype(_BF16), wz=z_c.astype(_BF16), wg=g_c.astype(_BF16),
                wq=wq_all, wkn=wkn, wvt=wvt,
                wpa=w_proj_a[l].astype(_BF16), wpb=w_proj_b[l].astype(_BF16),
                wpc=w_proj_c[l].astype(_BF16), wo=w_out[l].astype(_BF16))


def _rope_tables():
    pos = jnp.arange(SEQ, dtype=_F32)
    inv_freq = ROPE_THETA ** (-jnp.arange(0, MLA_ROPE, 2, dtype=_F32) / MLA_ROPE)
    ang = pos[:, None] * inv_freq[None, :]
    cos = jnp.cos(ang)
    sin = jnp.sin(ang)
    cos128t = jnp.tile(cos, (1, 4)).T
    sin128t = jnp.tile(sin, (1, 4)).T
    csk = jnp.concatenate([cos, cos, sin, sin], axis=1)
    return cos128t, sin128t, csk


def kernel(x, pre_norm_g, w_in, gm_ln_g, gm_ln_b, gm_ws, gm_bs, mla_q_norm_g, mla_w_uq,
           mla_kv_norm_g, mla_w_ukv, lru_conv_w, lru_conv_b, lru_w_a, lru_b_a, lru_w_x,
           lru_b_x, lru_lambda, w_proj_a, w_proj_b, w_proj_c, w_out, post_norm_g):
    cos128t, sin128t, csk = _rope_tables()
    wax = _gate_weights(lru_w_a, lru_w_x)
    x2d = x.reshape(TOKENS, D_MODEL)
    row = lambda a: a.reshape(1, -1)
    for l in range(DEPTH):
        w = _layer_weights(l, w_in, mla_w_uq, mla_w_ukv, w_proj_a, w_proj_b, w_proj_c, w_out)
        g_pre = row(pre_norm_g[l])
        p_bm = _proj(x2d, g_pre, row(gm_ln_g[l]), row(gm_ln_b[l]), w["w_bm"])

        cc = _lru_mixer(x2d.reshape(BATCH, SEQ, D_MODEL), g_pre, w["wxc"], w["wz"], w["wg"],
                        lru_conv_w[l], row(lru_conv_b[l]), wax[l], row(lru_b_a[l]),
                        row(lru_b_x[l]), row(lru_lambda[l]), w["wpc"]).reshape(TOKENS, D_MODEL)

        qt, k, vt = _mla_prep(p_bm, cos128t, sin128t, csk, row(mla_q_norm_g[l]),
                              row(mla_kv_norm_g[l]), w["wq"], w["wkn"], w["wvt"])
        x2d = _attention_merge(qt, k, vt, p_bm, x2d, cc, w["wpb"], gm_ws[l],
                               gm_bs[l][:, :, None], w["wpa"], w["wo"], row(post_norm_g[l]))
    return x2d.reshape(BATCH, SEQ, D_MODEL)
```

```python
import math

import jax
import jax.numpy as jnp
from jax import lax
from jax.experimental import pallas as pl
from jax.experimental.pallas import tpu as pltpu

D_MODEL = 1024
BATCH = 8
SEQ = 2048
DEPTH = 2
TOKENS = BATCH * SEQ
CHUNK = 64
EPS = 1e-6

GM_WIDTH = 1024
GM_GROUPS = 4
GM_BLOCK = 128
GM_GW = GM_WIDTH // GM_GROUPS

MLA_HEADS = 8
MLA_NOPE = 128
MLA_ROPE = 64
MLA_VDIM = 128
MLA_QK_DIM = MLA_NOPE + MLA_ROPE
MLA_Q_RANK = 384
MLA_KV_RANK = 256
MLA_WIDTH = MLA_HEADS * MLA_VDIM
ROPE_THETA = 10000.0

LRU_WIDTH = 1280
LRU_BLOCKS = 16
LRU_BW = LRU_WIDTH // LRU_BLOCKS
LRU_C = 8.0
CONV_W = 4

IN_SIZES = (GM_WIDTH, GM_WIDTH, GM_WIDTH, MLA_Q_RANK, MLA_KV_RANK, MLA_ROPE, MLA_WIDTH,
            LRU_WIDTH, LRU_WIDTH, D_MODEL, D_MODEL, D_MODEL)

BM_SEG = 1024
(COL_U, COL_V, COL_ZA, COL_ZB, COL_GA, COL_GB) = range(6)
BM_SEG_KINDS = ("raw", "ln", "silu", "silu", "sigmoid", "sigmoid")
LATENT_W = MLA_Q_RANK + MLA_KV_RANK + 2 * MLA_ROPE
N_BM = 6 * BM_SEG + LATENT_W
COL_LATENT = (6 * BM_SEG) // LATENT_W

VMEM_LIMIT_BYTES = 56 * 1024 * 1024

PROJ_TM = 1024
PROJ_TN = 2304
LRU_TS = 64
GATE_TN = 256
PREP_TM = 512
ATT_TQ = 256
ATT_TK = 256
ATT_D = 256
ATT_VROWS = MLA_VDIM + 16

_F32 = jnp.float32
_BF16 = jnp.bfloat16


def _params(*sem):
    return pltpu.CompilerParams(dimension_semantics=sem, vmem_limit_bytes=VMEM_LIMIT_BYTES)


def _sigmoid(x):
    return 0.5 * jnp.tanh(0.5 * x) + 0.5


def _silu(x):
    h = 0.5 * x
    return h * jnp.tanh(h) + h


def _rmsnorm_bf16(x, g):
    ms = jnp.mean(x * x, axis=-1, keepdims=True)
    return (x * lax.rsqrt(ms + EPS) * g).astype(_BF16)


def _dot_nt(a, b_t):
    return lax.dot_general(a, b_t, (((1,), (1,)), ((), ())), preferred_element_type=_F32)


def _proj_plan(step):
    lo, hi = step * PROJ_TN, (step + 1) * PROJ_TN
    segments = [(s * BM_SEG, (s + 1) * BM_SEG, kind) for s, kind in enumerate(BM_SEG_KINDS)]
    segments.append((len(BM_SEG_KINDS) * BM_SEG, N_BM, "raw"))
    plan = []
    for c0, c1, kind in segments:
        a, b = max(c0, lo), min(c1, hi)
        if a < b:
            assert kind != "ln" or (a, b) == (c0, c1), "layernorm needs its whole segment in one step"
            plan.append((a - lo, b - lo, kind))
    return plan


def _proj_kernel(x_ref, g_ref, lng_ref, lnb_ref, wt_ref, o_ref, h_scr):
    j = pl.program_id(1)

    @pl.when(j == 0)
    def _():
        h_scr[...] = _rmsnorm_bf16(x_ref[...], g_ref[...])

    def piece(c0, c1, kind):
        r = _dot_nt(h_scr[...], wt_ref[c0:c1, :])
        if kind == "ln":
            mu = jnp.mean(r, axis=-1, keepdims=True)
            rc = r - mu
            var = jnp.mean(rc * rc, axis=-1, keepdims=True)
            r = rc * lax.rsqrt(var + EPS) * lng_ref[...] + lnb_ref[...]
        elif kind == "silu":
            r = _silu(r)
        elif kind == "sigmoid":
            r = _sigmoid(r)
        o_ref[:, c0:c1] = r.astype(o_ref.dtype)

    for step in range(N_BM // PROJ_TN):
        @pl.when(j == step)
        def _(step=step):
            for c0, c1, kind in _proj_plan(step):
                piece(c0, c1, kind)


def _proj(x2d, g, lng, lnb, wt):
    vec = pl.BlockSpec((1, D_MODEL), lambda i, j: (0, 0))
    return pl.pallas_call(
        _proj_kernel,
        grid=(TOKENS // PROJ_TM, N_BM // PROJ_TN),
        in_specs=[
            pl.BlockSpec((PROJ_TM, D_MODEL), lambda i, j: (i, 0)),
            vec, vec, vec,
            pl.BlockSpec((PROJ_TN, D_MODEL), lambda i, j: (j, 0)),
        ],
        out_specs=pl.BlockSpec((PROJ_TM, PROJ_TN), lambda i, j: (i, j)),
        out_shape=jax.ShapeDtypeStruct((TOKENS, N_BM), _BF16),
        scratch_shapes=[pltpu.VMEM((PROJ_TM, D_MODEL), _BF16)],
        compiler_params=_params("arbitrary", "arbitrary"),
        name="proj_bm",
    )(x2d, g, lng, lnb, wt)


def _gate_k_range(c):
    first_block = (c * GATE_TN) // LRU_BW
    last_block = ((c + 1) * GATE_TN - 1) // LRU_BW
    k0 = (first_block * LRU_BW) // GATE_TN * GATE_TN
    k1 = -(-((last_block + 1) * LRU_BW) // GATE_TN) * GATE_TN
    return k0, k1


GATE_K_MAX = max(k1 - k0 for k0, k1 in map(_gate_k_range, range(LRU_WIDTH // GATE_TN)))


def _gate_weights_kernel(wa_ref, wx_ref, o_ref, stage):
    for c in range(LRU_WIDTH // GATE_TN):
        k0, _ = _gate_k_range(c)
        stage[...] = jnp.zeros(stage.shape, _F32)
        for h in range(LRU_BLOCKS):
            lo = max(LRU_BW * h, GATE_TN * c)
            hi = min(LRU_BW * (h + 1), GATE_TN * (c + 1))
            if lo >= hi:
                continue
            r0 = LRU_BW * h - k0
            src = slice(lo - LRU_BW * h, hi - LRU_BW * h)
            dst = slice(lo - GATE_TN * c, hi - GATE_TN * c)
            stage[r0:r0 + LRU_BW, dst] = wa_ref[h][:, src]
            stage[r0:r0 + LRU_BW, GATE_TN + dst.start:GATE_TN + dst.stop] = wx_ref[h][:, src]
        o_ref[c] = stage[...].astype(o_ref.dtype)


def _gate_weights(w_a, w_x):
    n_tiles = LRU_WIDTH // GATE_TN
    blocks = pl.BlockSpec((None, LRU_BLOCKS, LRU_BW, LRU_BW), lambda l: (l, 0, 0, 0))
    return pl.pallas_call(
        _gate_weights_kernel,
        grid=(DEPTH,),
        in_specs=[blocks, blocks],
        out_specs=pl.BlockSpec((None, n_tiles, GATE_K_MAX, 2 * GATE_TN), lambda l: (l, 0, 0, 0)),
        out_shape=jax.ShapeDtypeStruct((DEPTH, n_tiles, GATE_K_MAX, 2 * GATE_TN), _BF16),
        scratch_shapes=[pltpu.VMEM((GATE_K_MAX, 2 * GATE_TN), _F32)],
        compiler_params=_params("arbitrary"),
        name="gate_weights",
    )(w_a, w_x)


def _lru_kernel(x_ref, pg_ref, wxc_ref, wz_ref, wg_ref, cw_ref, cb_ref, wax_ref, ba_ref, bx_ref,
                lam_ref, wp_ref, o_ref, xext, a_scr, b_scr, h_scr, sz_scr, sg_scr):
    rows = LRU_TS * BATCH
    halo = (CONV_W - 1) * BATCH

    @pl.when(pl.program_id(0) == 0)
    def _():
        xext[0:halo, :] = jnp.zeros((halo, LRU_WIDTH), _F32)
        h_scr[...] = jnp.zeros_like(h_scr)

    xt = jnp.swapaxes(x_ref[...], 0, 1).reshape(rows, D_MODEL)
    hn = _rmsnorm_bf16(xt, pg_ref[...])

    xext[halo:halo + rows, :] = _dot_nt(hn, wxc_ref[...])
    xc = cb_ref[...] + cw_ref[CONV_W - 1:CONV_W, :] * xext[halo:halo + rows, :]
    for k in range(CONV_W - 1):
        xc = xc + cw_ref[k:k + 1, :] * xext[k * BATCH:k * BATCH + rows, :]
    xext[0:halo, :] = xext[rows:rows + halo, :]

    xcb = xc.astype(_BF16)
    lam = lam_ref[...]
    softplus_neg_lam = jnp.maximum(-lam, 0.0) + jnp.log(1.0 + jnp.exp(-jnp.abs(lam)))
    for c in range(LRU_WIDTH // GATE_TN):
        cs = slice(c * GATE_TN, (c + 1) * GATE_TN)
        k0, k1 = _gate_k_range(c)
        ri = jnp.dot(xcb[:, k0:k1], wax_ref[c, 0:k1 - k0, :], preferred_element_type=_F32)
        half_log = (-0.5 * LRU_C) * softplus_neg_lam[:, cs]
        a = jnp.exp(half_log * jnp.tanh(0.5 * (ri[:, 0:GATE_TN] + ba_ref[:, cs])) + half_log)
        i = _sigmoid(ri[:, GATE_TN:2 * GATE_TN] + bx_ref[:, cs])
        gap = 1.0 - a * a
        mult = jnp.where(gap > 0.0, gap * lax.rsqrt(gap), 0.0)
        a_scr[:, cs] = a
        b_scr[:, cs] = mult * (i * xc[:, cs])

    sz_scr[...] = _silu(_dot_nt(hn, wz_ref[...]))
    sg_scr[...] = _sigmoid(_dot_nt(hn, wg_ref[...]))

    h = h_scr[...]
    for s in range(LRU_TS):
        rs = slice(s * BATCH, (s + 1) * BATCH)
        h = a_scr[rs, :] * h + b_scr[rs, :]
        b_scr[rs, :] = h
    h_scr[...] = h

    y = (b_scr[...] * sz_scr[...]).astype(_BF16)
    cc = sg_scr[...] * jnp.dot(y, wp_ref[...], preferred_element_type=_F32)
    o_ref[...] = jnp.swapaxes(cc.reshape(LRU_TS, BATCH, D_MODEL), 0, 1).astype(o_ref.dtype)


def _lru_mixer(x3d, pre_g, wxc, wz, wg, conv_w, conv_b, wax, ba, bx, lam, wp):
    rows = LRU_TS * BATCH
    halo = (CONV_W - 1) * BATCH

    def resident(a):
        return pl.BlockSpec(a.shape, lambda t: (0,) * a.ndim, pipeline_mode=pl.Buffered(1))

    return pl.pallas_call(
        _lru_kernel,
        grid=(SEQ // LRU_TS,),
        in_specs=[pl.BlockSpec((BATCH, LRU_TS, D_MODEL), lambda t: (0, t, 0))]
        + [resident(a) for a in (pre_g, wxc, wz, wg, conv_w, conv_b, wax, ba, bx, lam, wp)],
        out_specs=pl.BlockSpec((BATCH, LRU_TS, D_MODEL), lambda t: (0, t, 0)),
        out_shape=jax.ShapeDtypeStruct((BATCH, SEQ, D_MODEL), _BF16),
        scratch_shapes=[
            pltpu.VMEM((rows + halo, LRU_WIDTH), _F32),
            pltpu.VMEM((rows, LRU_WIDTH), _F32),
            pltpu.VMEM((rows, LRU_WIDTH), _F32),
            pltpu.VMEM((BATCH, LRU_WIDTH), _F32),
            pltpu.VMEM((rows, LRU_WIDTH), _F32),
            pltpu.VMEM((rows, D_MODEL), _F32),
        ],
        compiler_params=_params("arbitrary"),
        name="lru_mixer",
    )(x3d, pre_g, wxc, wz, wg, conv_w, conv_b, wax, ba, bx, lam, wp)


def _mla_prep_kernel(c_ref, cost_ref, sint_ref, csk_ref, qg_ref, kvg_ref, wqt_ref, wkn_ref, wvt_ref,
                     qt_ref, k_ref, vt_ref):
    qscale = math.log2(math.e) / math.sqrt(MLA_QK_DIM)
    c = c_ref[...].astype(_F32)
    cq = c[:, 0:MLA_Q_RANK]
    ckv = c[:, MLA_Q_RANK:MLA_Q_RANK + MLA_KV_RANK]
    krk = c[:, MLA_Q_RANK + MLA_KV_RANK:LATENT_W]

    hq = (cq * lax.rsqrt(jnp.mean(cq * cq, axis=-1, keepdims=True) + EPS)
          * qg_ref[...]).astype(_BF16)
    qt = lax.dot_general(wqt_ref[...], hq, (((1,), (1,)), ((), ())), preferred_element_type=_F32)
    nope_w = MLA_HEADS * MLA_NOPE
    rope_w = MLA_HEADS * MLA_ROPE
    cost = cost_ref[...]
    sint = sint_ref[...]
    for p in range(rope_w // 128):
        qr = qt[nope_w + 128 * p:nope_w + 128 * (p + 1), :]
        qrr = qt[nope_w + rope_w + 128 * p:nope_w + rope_w + 128 * (p + 1), :]
        q_pair = ((qr * cost + qrr * sint) * qscale).astype(_BF16)
        for h in (2 * p, 2 * p + 1):
            qt_ref[ATT_D * h:ATT_D * h + MLA_NOPE, :] = (
                qt[MLA_NOPE * h:MLA_NOPE * (h + 1), :] * qscale).astype(_BF16)
            qt_ref[ATT_D * h + MLA_NOPE:ATT_D * (h + 1), :] = q_pair

    hkv = (ckv * lax.rsqrt(jnp.mean(ckv * ckv, axis=-1, keepdims=True) + EPS)
           * kvg_ref[...]).astype(_BF16)
    kn = jnp.dot(hkv, wkn_ref[...], preferred_element_type=_F32)
    vt = lax.dot_general(wvt_ref[...], hkv, (((1,), (1,)), ((), ())),
                         preferred_element_type=_F32).astype(_BF16)
    for h in range(MLA_HEADS):
        vt_ref[ATT_VROWS * h:ATT_VROWS * h + MLA_VDIM, :] = vt[MLA_VDIM * h:MLA_VDIM * (h + 1), :]
        vt_ref[ATT_VROWS * h + MLA_VDIM:ATT_VROWS * (h + 1), :] = jnp.ones(
            (ATT_VROWS - MLA_VDIM, vt.shape[1]), _BF16)

    t = krk * csk_ref[...]
    kf2 = t + pltpu.roll(t, MLA_ROPE, 1)
    lane = lax.broadcasted_iota(jnp.int32, kf2.shape, 1)
    kr_even = jnp.where(lane < MLA_ROPE, kf2, 0.0).astype(_BF16)
    kr_odd = jnp.where(lane >= MLA_ROPE, kf2, 0.0).astype(_BF16)
    for h in range(MLA_HEADS):
        k_ref[:, ATT_D * h:ATT_D * h + MLA_NOPE] = kn[:, MLA_NOPE * h:MLA_NOPE * (h + 1)].astype(_BF16)
        k_ref[:, ATT_D * h + MLA_NOPE:ATT_D * (h + 1)] = kr_even if h % 2 == 0 else kr_odd


def _mla_prep(p_bm, cos128t, sin128t, csk, qg, kvg, wqt, wkn, wvt):
    tiles_per_seq = SEQ // PREP_TM
    const = lambda i: (0, 0)
    pos = lambda i: (i % tiles_per_seq, 0)
    pos_t = lambda i: (0, i % tiles_per_seq)
    return pl.pallas_call(
        _mla_prep_kernel,
        grid=(TOKENS // PREP_TM,),
        in_specs=[
            pl.BlockSpec((PREP_TM, LATENT_W), lambda i: (i, COL_LATENT)),
            pl.BlockSpec((128, PREP_TM), pos_t),
            pl.BlockSpec((128, PREP_TM), pos_t),
            pl.BlockSpec((PREP_TM, 128), pos),
            pl.BlockSpec((1, MLA_Q_RANK), const),
            pl.BlockSpec((1, MLA_KV_RANK), const),
            pl.BlockSpec(wqt.shape, const),
            pl.BlockSpec(wkn.shape, const),
            pl.BlockSpec(wvt.shape, const),
        ],
        out_specs=[
            pl.BlockSpec((MLA_HEADS * ATT_D, PREP_TM), lambda i: (0, i)),
            pl.BlockSpec((PREP_TM, MLA_HEADS * ATT_D), lambda i: (i, 0)),
            pl.BlockSpec((MLA_HEADS * ATT_VROWS, PREP_TM), lambda i: (0, i)),
        ],
        out_shape=[
            jax.ShapeDtypeStruct((MLA_HEADS * ATT_D, TOKENS), _BF16),
            jax.ShapeDtypeStruct((TOKENS, MLA_HEADS * ATT_D), _BF16),
            jax.ShapeDtypeStruct((MLA_HEADS * ATT_VROWS, TOKENS), _BF16),
        ],
        compiler_params=_params("arbitrary"),
        name="mla_prep",
    )(p_bm, cos128t, sin128t, csk, qg, kvg, wqt, wkn, wvt)


def _attn_kernel(qt_ref, k_ref, vt_ref, qt_next_ref, k_next_ref, szb_ref, sgb_ref, wp_ref,
                 x_ref, u_ref, vln_ref, sza_ref, sga_ref, cc_ref, ws_ref, bs_ref,
                 wpa_ref, wo_ref, pg_ref, o_ref,
                 st_scr, mt_scr, m_scr, l_scr, acc_scr, y_scr, ya_scr):
    qi = pl.program_id(1)
    key_chunk = lax.broadcasted_iota(jnp.int32, (ATT_TK, ATT_TQ), 0) // CHUNK
    query_chunk = lax.broadcasted_iota(jnp.int32, (ATT_TK, ATT_TQ), 1) // CHUNK
    diag_mask = key_chunk <= query_chunk

    m_scr[...] = jnp.full(m_scr.shape, -1e30, _F32)
    l_scr[...] = jnp.zeros(l_scr.shape, _F32)
    acc_scr[...] = jnp.zeros(acc_scr.shape, _F32)

    def store_scores(k_tile, q_t, h):
        st = jnp.dot(k_tile, q_t, preferred_element_type=_F32)
        st_scr[h] = st
        mt_scr[h] = jnp.max(st, axis=0, keepdims=True)

    def scores(j, h):
        off = pl.multiple_of(j * ATT_TK, ATT_TK)
        ds_ = slice(ATT_D * h, ATT_D * (h + 1))
        store_scores(k_ref[pl.ds(off, ATT_TK), ds_], qt_ref[ds_, :], h)

    def accumulate(j, h, masked):
        off = pl.multiple_of(j * ATT_TK, ATT_TK)
        st = st_scr[h]
        if masked:
            st = jnp.where(diag_mask, st, -1e30)
            mt = jnp.max(st, axis=0, keepdims=True)
        else:
            mt = mt_scr[h]
        m_prev = m_scr[h]
        m_new = jnp.maximum(m_prev, mt)
        alpha = jnp.exp2(m_prev - m_new)
        p = jnp.exp2(st - m_new)
        vt = vt_ref[ATT_VROWS * h:ATT_VROWS * (h + 1), pl.ds(off, ATT_TK)]
        pv = jnp.dot(vt, p.astype(_BF16), preferred_element_type=_F32)
        acc_scr[h] = alpha * acc_scr[h] + pv[0:MLA_VDIM, :]
        l_scr[h] = alpha * l_scr[h] + pv[MLA_VDIM:MLA_VDIM + 1, :]
        m_scr[h] = m_new

    @pl.when((pl.program_id(0) == 0) & (qi == 0))
    def _():
        for h in range(MLA_HEADS):
            scores(0, h)

    def body(j, carry):
        for h in range(MLA_HEADS):
            accumulate(j, h, False)
            scores(j + 1, h)
        return carry

    lax.fori_loop(0, qi, body, 0)
    for h in range(MLA_HEADS):
        accumulate(qi, h, True)
        ds_ = slice(ATT_D * h, ATT_D * (h + 1))
        store_scores(k_next_ref[:, ds_], qt_next_ref[ds_, :], h)

    for h in range(MLA_HEADS):
        hs = slice(MLA_VDIM * h, MLA_VDIM * (h + 1))
        o = (acc_scr[h] * (1.0 / l_scr[h])).T
        y_scr[:, hs] = (o * szb_ref[:, hs].astype(_F32)).astype(_BF16)

    cb = sgb_ref[...].astype(_F32) * jnp.dot(y_scr[...], wp_ref[...], preferred_element_type=_F32)

    idx_r = lax.broadcasted_iota(jnp.int32, (GM_BLOCK, GM_BLOCK), 0) // CHUNK
    idx_c = lax.broadcasted_iota(jnp.int32, (GM_BLOCK, GM_BLOCK), 1) // CHUNK
    causal = idx_c <= idx_r

    for g in range(GM_GROUPS):
        ws = jnp.where(causal, ws_ref[g], 0.0).astype(_BF16)
        cs = slice(GM_GW * g, GM_GW * (g + 1))
        for r in range(ATT_TQ // GM_BLOCK):
            rs = slice(GM_BLOCK * r, GM_BLOCK * (r + 1))
            sv = jnp.dot(ws, vln_ref[rs, cs], preferred_element_type=_F32) + bs_ref[g]
            y = u_ref[rs, cs].astype(_F32) * sv * sza_ref[rs, cs].astype(_F32)
            ya_scr[rs, cs] = y.astype(_BF16)

    ca = sga_ref[...].astype(_F32) * jnp.dot(ya_scr[...], wpa_ref[...], preferred_element_type=_F32)
    merged = ca + cb + cc_ref[...].astype(_F32)
    o = jnp.dot(merged.astype(_BF16), wo_ref[...], preferred_element_type=_F32)
    o = o * lax.rsqrt(jnp.mean(o * o, axis=-1, keepdims=True) + EPS) * pg_ref[...]
    o_ref[...] = x_ref[...] + o


def _attention_merge(qt, k, vt, p_bm, x2d, cc, wpb, ws, bs_col, wpa, wo, pg):
    nq = SEQ // ATT_TQ
    rows = lambda b, i: (b * nq + i, 0)
    next_step = lambda b, i: jnp.minimum(b * nq + i + 1, BATCH * nq - 1)
    seg = lambda c: pl.BlockSpec((ATT_TQ, BM_SEG), lambda b, i: (b * nq + i, c))
    tile = pl.BlockSpec((ATT_TQ, D_MODEL), rows)

    def resident(a):
        return pl.BlockSpec(a.shape, lambda b, i: (0,) * a.ndim, pipeline_mode=pl.Buffered(1))

    return pl.pallas_call(
        _attn_kernel,
        grid=(BATCH, nq),
        in_specs=[
            pl.BlockSpec((MLA_HEADS * ATT_D, ATT_TQ), lambda b, i: (0, b * nq + i)),
            pl.BlockSpec((SEQ, MLA_HEADS * ATT_D), lambda b, i: (b, 0)),
            pl.BlockSpec((MLA_HEADS * ATT_VROWS, SEQ), lambda b, i: (0, b)),
            pl.BlockSpec((MLA_HEADS * ATT_D, ATT_TQ), lambda b, i: (0, next_step(b, i))),
            pl.BlockSpec((ATT_TK, MLA_HEADS * ATT_D),
                         lambda b, i: ((next_step(b, i) // nq) * (SEQ // ATT_TK), 0)),
            seg(COL_ZB), seg(COL_GB), resident(wpb),
            tile, seg(COL_U), seg(COL_V), seg(COL_ZA), seg(COL_GA), tile,
            resident(ws), resident(bs_col),
            resident(wpa), resident(wo), resident(pg),
        ],
        out_specs=tile,
        out_shape=jax.ShapeDtypeStruct((TOKENS, D_MODEL), _F32),
        scratch_shapes=[
            pltpu.VMEM((MLA_HEADS, ATT_TK, ATT_TQ), _F32),
            pltpu.VMEM((MLA_HEADS, 1, ATT_TQ), _F32),
            pltpu.VMEM((MLA_HEADS, 1, ATT_TQ), _F32),
            pltpu.VMEM((MLA_HEADS, 1, ATT_TQ), _F32),
            pltpu.VMEM((MLA_HEADS, MLA_VDIM, ATT_TQ), _F32),
            pltpu.VMEM((ATT_TQ, MLA_WIDTH), _BF16),
            pltpu.VMEM((ATT_TQ, GM_WIDTH), _BF16),
        ],
        compiler_params=_params("arbitrary", "arbitrary"),
        name="mla_attn_merge",
    )(qt, k, vt, qt, k, p_bm, p_bm, wpb, x2d, p_bm, p_bm, p_bm, p_bm, cc,
      ws, bs_col, wpa, wo, pg)


def _layer_weights(l, w_in, mla_w_uq, mla_w_ukv, w_proj_a, w_proj_b, w_proj_c, w_out):
    cuts = [0]
    for s in IN_SIZES:
        cuts.append(cuts[-1] + s)
    w_in_t = jnp.swapaxes(w_in, 1, 2)
    seg = lambda k: lax.slice(w_in_t, (l, cuts[k], 0), (l + 1, cuts[k + 1], D_MODEL)).reshape(
        cuts[k + 1] - cuts[k], D_MODEL)
    (u, v, z_a, c_q, c_kv, k_rope, z_b, x_c, z_c, g_a, g_b, g_c) = [seg(k) for k in range(12)]
    half = MLA_ROPE // 2
    k_rope_rot = jnp.concatenate([-k_rope[half:, :], k_rope[:half, :]], axis=0)
    w_bm = jnp.concatenate([u, v, z_a, z_b, g_a, g_b, c_q, c_kv, k_rope, k_rope_rot],
                           axis=0).astype(_BF16)

    wq = mla_w_uq[l].reshape(MLA_Q_RANK, MLA_HEADS, MLA_QK_DIM)
    wq_nope = wq[:, :, :MLA_NOPE].reshape(MLA_Q_RANK, -1)
    wq_rope = wq[:, :, MLA_NOPE:].reshape(MLA_Q_RANK, -1)
    wq_rope_rot = jnp.concatenate([-wq[:, :, MLA_NOPE + half:], wq[:, :, MLA_NOPE:MLA_NOPE + half]],
                                  axis=2).reshape(MLA_Q_RANK, -1)
    wq_all = jnp.concatenate([wq_nope, wq_rope, wq_rope_rot], axis=1).T.astype(_BF16)

    wkv = mla_w_ukv[l].reshape(MLA_KV_RANK, MLA_HEADS, MLA_NOPE + MLA_VDIM)
    wkn = wkv[:, :, :MLA_NOPE].reshape(MLA_KV_RANK, -1).astype(_BF16)
    wvt = wkv[:, :, MLA_NOPE:].reshape(MLA_KV_RANK, -1).T.astype(_BF16)

    return dict(w_bm=w_bm, wxc=x_c.astype(_BF16), wz=z_c.astype(_BF16), wg=g_c.astype(_BF16),
                wq=wq_all, wkn=wkn, wvt=wvt,
                wpa=w_proj_a[l].astype(_BF16), wpb=w_proj_b[l].astype(_BF16),
                wpc=w_proj_c[l].astype(_BF16), wo=w_out[l].astype(_BF16))


def _rope_tables():
    pos = jnp.arange(SEQ, dtype=_F32)
    inv_freq = ROPE_THETA ** (-jnp.arange(0, MLA_ROPE, 2, dtype=_F32) / MLA_ROPE)
    ang = pos[:, None] * inv_freq[None, :]
    cos = jnp.cos(ang)
    sin = jnp.sin(ang)
    cos128t = jnp.tile(cos, (1, 4)).T
    sin128t = jnp.tile(sin, (1, 4)).T
    csk = jnp.concatenate([cos, cos, sin, sin], axis=1)
    return cos128t, sin128t, csk


def kernel(x, pre_norm_g, w_in, gm_ln_g, gm_ln_b, gm_ws, gm_bs, mla_q_norm_g, mla_w_uq,
           mla_kv_norm_g, mla_w_ukv, lru_conv_w, lru_conv_b, lru_w_a, lru_b_a, lru_w_x,
           lru_b_x, lru_lambda, w_proj_a, w_proj_b, w_proj_c, w_out, post_norm_g):
    cos128t, sin128t, csk = _rope_tables()
    wax = _gate_weights(lru_w_a, lru_w_x)
    x2d = x.reshape(TOKENS, D_MODEL)
    row = lambda a: a.reshape(1, -1)
    for l in range(DEPTH):
        w = _layer_weights(l, w_in, mla_w_uq, mla_w_ukv, w_proj_a, w_proj_b, w_proj_c, w_out)
        g_pre = row(pre_norm_g[l])
        p_bm = _proj(x2d, g_pre, row(gm_ln_g[l]), row(gm_ln_b[l]), w["w_bm"])

        cc = _lru_mixer(x2d.reshape(BATCH, SEQ, D_MODEL), g_pre, w["wxc"], w["wz"], w["wg"],
                        lru_conv_w[l], row(lru_conv_b[l]), wax[l], row(lru_b_a[l]),
                        row(lru_b_x[l]), row(lru_lambda[l]), w["wpc"]).reshape(TOKENS, D_MODEL)

        qt, k, vt = _mla_prep(p_bm, cos128t, sin128t, csk, row(mla_q_norm_g[l]),
                              row(mla_kv_norm_g[l]), w["wq"], w["wkn"], w["wvt"])
        x2d = _attention_merge(qt, k, vt, p_bm, x2d, cc, w["wpb"], gm_ws[l],
                               gm_bs[l][:, :, None], w["wpa"], w["wo"], row(post_norm_g[l]))
    return x2d.reshape(BATCH, SEQ, D_MODEL)
```

```python
import math

import jax
import jax.numpy as jnp
from jax import lax
from jax.experimental import pallas as pl
from jax.experimental.pallas import tpu as pltpu

D_MODEL = 1024
BATCH = 8
SEQ = 2048
DEPTH = 2
TOKENS = BATCH * SEQ
CHUNK = 64
EPS = 1e-6

GM_WIDTH = 1024
GM_GROUPS = 4
GM_BLOCK = 128
GM_GW = GM_WIDTH // GM_GROUPS

MLA_HEADS = 8
MLA_NOPE = 128
MLA_ROPE = 64
MLA_VDIM = 128
MLA_QK_DIM = MLA_NOPE + MLA_ROPE
MLA_Q_RANK = 384
MLA_KV_RANK = 256
MLA_WIDTH = MLA_HEADS * MLA_VDIM
ROPE_THETA = 10000.0

LRU_WIDTH = 1280
LRU_BLOCKS = 16
LRU_BW = LRU_WIDTH // LRU_BLOCKS
LRU_C = 8.0
CONV_W = 4

IN_SIZES = (GM_WIDTH, GM_WIDTH, GM_WIDTH, MLA_Q_RANK, MLA_KV_RANK, MLA_ROPE, MLA_WIDTH,
            LRU_WIDTH, LRU_WIDTH, D_MODEL, D_MODEL, D_MODEL)

BM_SEG = 1024
(COL_U, COL_V, COL_ZA, COL_ZB, COL_GA, COL_GB) = range(6)
BM_SEG_KINDS = ("raw", "ln", "silu", "silu", "sigmoid", "sigmoid")
LATENT_W = MLA_Q_RANK + MLA_KV_RANK + 2 * MLA_ROPE
N_BM = 6 * BM_SEG + LATENT_W
COL_LATENT = (6 * BM_SEG) // LATENT_W

VMEM_LIMIT_BYTES = 56 * 1024 * 1024

PROJ_TM = 1024
PROJ_TN = 2304
LRU_TS = 64
GATE_TN = 256
PREP_TM = 1024
ATT_TQ = 256
ATT_TK = 256
ATT_D = 256
ATT_QROWS = MLA_HEADS * MLA_NOPE + (MLA_HEADS // 2) * 128
ATT_KCOLS = MLA_HEADS * MLA_NOPE + 2 * 128
ATT_VROWS = MLA_VDIM + 16

_F32 = jnp.float32
_BF16 = jnp.bfloat16


def _params(*sem):
    return pltpu.CompilerParams(dimension_semantics=sem, vmem_limit_bytes=VMEM_LIMIT_BYTES)


def _sigmoid(x):
    return 0.5 * jnp.tanh(0.5 * x) + 0.5


def _silu(x):
    h = 0.5 * x
    return h * jnp.tanh(h) + h


def _rmsnorm_bf16(x, g):
    ms = jnp.mean(x * x, axis=-1, keepdims=True)
    return (x * lax.rsqrt(ms + EPS) * g).astype(_BF16)


def _dot_nt(a, b_t):
    return lax.dot_general(a, b_t, (((1,), (1,)), ((), ())), preferred_element_type=_F32)


def _proj_plan(step):
    lo, hi = step * PROJ_TN, (step + 1) * PROJ_TN
    segments = [(s * BM_SEG, (s + 1) * BM_SEG, kind) for s, kind in enumerate(BM_SEG_KINDS)]
    segments.append((len(BM_SEG_KINDS) * BM_SEG, N_BM, "raw"))
    plan = []
    for c0, c1, kind in segments:
        a, b = max(c0, lo), min(c1, hi)
        if a < b:
            assert kind != "ln" or (a, b) == (c0, c1), "layernorm needs its whole segment in one step"
            plan.append((a - lo, b - lo, kind))
    return plan


def _proj_kernel(x_ref, g_ref, lng_ref, lnb_ref, wt_ref, o_ref, h_scr):
    j = pl.program_id(1)

    @pl.when(j == 0)
    def _():
        h_scr[...] = _rmsnorm_bf16(x_ref[...], g_ref[...])

    def piece(c0, c1, kind):
        r = _dot_nt(h_scr[...], wt_ref[c0:c1, :])
        if kind == "ln":
            mu = jnp.mean(r, axis=-1, keepdims=True)
            rc = r - mu
            var = jnp.mean(rc * rc, axis=-1, keepdims=True)
            r = rc * lax.rsqrt(var + EPS) * lng_ref[...] + lnb_ref[...]
        elif kind == "silu":
            r = _silu(r)
        elif kind == "sigmoid":
            r = _sigmoid(r)
        o_ref[:, c0:c1] = r.astype(o_ref.dtype)

    for step in range(N_BM // PROJ_TN):
        @pl.when(j == step)
        def _(step=step):
            for c0, c1, kind in _proj_plan(step):
                piece(c0, c1, kind)


def _proj(x2d, g, lng, lnb, wt):
    vec = pl.BlockSpec((1, D_MODEL), lambda i, j: (0, 0))
    return pl.pallas_call(
        _proj_kernel,
        grid=(TOKENS // PROJ_TM, N_BM // PROJ_TN),
        in_specs=[
            pl.BlockSpec((PROJ_TM, D_MODEL), lambda i, j: (i, 0)),
            vec, vec, vec,
            pl.BlockSpec((PROJ_TN, D_MODEL), lambda i, j: (j, 0)),
        ],
        out_specs=pl.BlockSpec((PROJ_TM, PROJ_TN), lambda i, j: (i, j)),
        out_shape=jax.ShapeDtypeStruct((TOKENS, N_BM), _BF16),
        scratch_shapes=[pltpu.VMEM((PROJ_TM, D_MODEL), _BF16)],
        compiler_params=_params("arbitrary", "arbitrary"),
        name="proj_bm",
    )(x2d, g, lng, lnb, wt)


def _gate_k_range(c):
    first_block = (c * GATE_TN) // LRU_BW
    last_block = ((c + 1) * GATE_TN - 1) // LRU_BW
    k0 = (first_block * LRU_BW) // GATE_TN * GATE_TN
    k1 = -(-((last_block + 1) * LRU_BW) // GATE_TN) * GATE_TN
    return k0, k1


GATE_K_MAX = max(k1 - k0 for k0, k1 in map(_gate_k_range, range(LRU_WIDTH // GATE_TN)))


def _gate_weights_kernel(wa_ref, wx_ref, o_ref, stage):
    for c in range(LRU_WIDTH // GATE_TN):
        k0, _ = _gate_k_range(c)
        stage[...] = jnp.zeros(stage.shape, _F32)
        for h in range(LRU_BLOCKS):
            lo = max(LRU_BW * h, GATE_TN * c)
            hi = min(LRU_BW * (h + 1), GATE_TN * (c + 1))
            if lo >= hi:
                continue
            r0 = LRU_BW * h - k0
            src = slice(lo - LRU_BW * h, hi - LRU_BW * h)
            dst = slice(lo - GATE_TN * c, hi - GATE_TN * c)
            stage[r0:r0 + LRU_BW, dst] = wa_ref[h][:, src]
            stage[r0:r0 + LRU_BW, GATE_TN + dst.start:GATE_TN + dst.stop] = wx_ref[h][:, src]
        o_ref[c] = stage[...].astype(o_ref.dtype)


def _gate_weights(w_a, w_x):
    n_tiles = LRU_WIDTH // GATE_TN
    blocks = pl.BlockSpec((None, LRU_BLOCKS, LRU_BW, LRU_BW), lambda l: (l, 0, 0, 0))
    return pl.pallas_call(
        _gate_weights_kernel,
        grid=(DEPTH,),
        in_specs=[blocks, blocks],
        out_specs=pl.BlockSpec((None, n_tiles, GATE_K_MAX, 2 * GATE_TN), lambda l: (l, 0, 0, 0)),
        out_shape=jax.ShapeDtypeStruct((DEPTH, n_tiles, GATE_K_MAX, 2 * GATE_TN), _BF16),
        scratch_shapes=[pltpu.VMEM((GATE_K_MAX, 2 * GATE_TN), _F32)],
        compiler_params=_params("arbitrary"),
        name="gate_weights",
    )(w_a, w_x)


def _lru_kernel(x_ref, pg_ref, wxc_ref, wz_ref, wg_ref, cw_ref, cb_ref, wax_ref, ba_ref, bx_ref,
                lam_ref, wp_ref, o_ref, xext, a_scr, b_scr, h_scr, sz_scr, sg_scr):
    rows = LRU_TS * BATCH
    halo = (CONV_W - 1) * BATCH

    @pl.when(pl.program_id(0) == 0)
    def _():
        xext[0:halo, :] = jnp.zeros((halo, LRU_WIDTH), _F32)
        h_scr[...] = jnp.zeros_like(h_scr)

    xt = jnp.swapaxes(x_ref[...], 0, 1).reshape(rows, D_MODEL)
    hn = _rmsnorm_bf16(xt, pg_ref[...])

    xext[halo:halo + rows, :] = _dot_nt(hn, wxc_ref[...])
    xc = cb_ref[...] + cw_ref[CONV_W - 1:CONV_W, :] * xext[halo:halo + rows, :]
    for k in range(CONV_W - 1):
        xc = xc + cw_ref[k:k + 1, :] * xext[k * BATCH:k * BATCH + rows, :]
    xext[0:halo, :] = xext[rows:rows + halo, :]

    xcb = xc.astype(_BF16)
    lam = lam_ref[...]
    softplus_neg_lam = jnp.maximum(-lam, 0.0) + jnp.log(1.0 + jnp.exp(-jnp.abs(lam)))
    for c in range(LRU_WIDTH // GATE_TN):
        cs = slice(c * GATE_TN, (c + 1) * GATE_TN)
        k0, k1 = _gate_k_range(c)
        ri = jnp.dot(xcb[:, k0:k1], wax_ref[c, 0:k1 - k0, :], preferred_element_type=_F32)
        half_log = (-0.5 * LRU_C) * softplus_neg_lam[:, cs]
        a = jnp.exp(half_log * jnp.tanh(0.5 * (ri[:, 0:GATE_TN] + ba_ref[:, cs])) + half_log)
        i = _sigmoid(ri[:, GATE_TN:2 * GATE_TN] + bx_ref[:, cs])
        gap = 1.0 - a * a
        mult = jnp.where(gap > 0.0, gap * lax.rsqrt(gap), 0.0)
        a_scr[:, cs] = a
        b_scr[:, cs] = mult * (i * xc[:, cs])

    sz_scr[...] = _silu(_dot_nt(hn, wz_ref[...]))
    sg_scr[...] = _sigmoid(_dot_nt(hn, wg_ref[...]))

    h = h_scr[...]
    for s in range(LRU_TS):
        rs = slice(s * BATCH, (s + 1) * BATCH)
        h = a_scr[rs, :] * h + b_scr[rs, :]
        b_scr[rs, :] = h
    h_scr[...] = h

    y = (b_scr[...] * sz_scr[...]).astype(_BF16)
    cc = sg_scr[...] * jnp.dot(y, wp_ref[...], preferred_element_type=_F32)
    o_ref[...] = jnp.swapaxes(cc.reshape(LRU_TS, BATCH, D_MODEL), 0, 1).astype(o_ref.dtype)


def _lru_mixer(x3d, pre_g, wxc, wz, wg, conv_w, conv_b, wax, ba, bx, lam, wp):
    rows = LRU_TS * BATCH
    halo = (CONV_W - 1) * BATCH

    def resident(a):
        return pl.BlockSpec(a.shape, lambda t: (0,) * a.ndim, pipeline_mode=pl.Buffered(1))

    return pl.pallas_call(
        _lru_kernel,
        grid=(SEQ // LRU_TS,),
        in_specs=[pl.BlockSpec((BATCH, LRU_TS, D_MODEL), lambda t: (0, t, 0))]
        + [resident(a) for a in (pre_g, wxc, wz, wg, conv_w, conv_b, wax, ba, bx, lam, wp)],
        out_specs=pl.BlockSpec((BATCH, LRU_TS, D_MODEL), lambda t: (0, t, 0)),
        out_shape=jax.ShapeDtypeStruct((BATCH, SEQ, D_MODEL), _BF16),
        scratch_shapes=[
            pltpu.VMEM((rows + halo, LRU_WIDTH), _F32),
            pltpu.VMEM((rows, LRU_WIDTH), _F32),
            pltpu.VMEM((rows, LRU_WIDTH), _F32),
            pltpu.VMEM((BATCH, LRU_WIDTH), _F32),
            pltpu.VMEM((rows, LRU_WIDTH), _F32),
            pltpu.VMEM((rows, D_MODEL), _F32),
        ],
        compiler_params=_params("arbitrary"),
        name="lru_mixer",
    )(x3d, pre_g, wxc, wz, wg, conv_w, conv_b, wax, ba, bx, lam, wp)


def _mla_prep_kernel(c_ref, cost_ref, sint_ref, csk_ref, qg_ref, kvg_ref, wqt_ref, wkn_ref, wvt_ref,
                     qt_ref, k_ref, vt_ref):
    qscale = math.log2(math.e) / math.sqrt(MLA_QK_DIM)
    c = c_ref[...].astype(_F32)
    cq = c[:, 0:MLA_Q_RANK]
    ckv = c[:, MLA_Q_RANK:MLA_Q_RANK + MLA_KV_RANK]
    krk = c[:, MLA_Q_RANK + MLA_KV_RANK:LATENT_W]

    hq = (cq * lax.rsqrt(jnp.mean(cq * cq, axis=-1, keepdims=True) + EPS)
          * qg_ref[...]).astype(_BF16)
    qt = lax.dot_general(wqt_ref[...], hq, (((1,), (1,)), ((), ())), preferred_element_type=_F32)
    nope_w = MLA_HEADS * MLA_NOPE
    rope_w = MLA_HEADS * MLA_ROPE
    cost = cost_ref[...]
    sint = sint_ref[...]
    for p in range(rope_w // 128):
        qr = qt[nope_w + 128 * p:nope_w + 128 * (p + 1), :]
        qrr = qt[nope_w + rope_w + 128 * p:nope_w + rope_w + 128 * (p + 1), :]
        qt_ref[nope_w + 128 * p:nope_w + 128 * (p + 1), :] = (
            (qr * cost + qrr * sint) * qscale).astype(_BF16)
    qt_ref[0:nope_w, :] = (qt[0:nope_w, :] * qscale).astype(_BF16)

    hkv = (ckv * lax.rsqrt(jnp.mean(ckv * ckv, axis=-1, keepdims=True) + EPS)
           * kvg_ref[...]).astype(_BF16)
    kn = jnp.dot(hkv, wkn_ref[...], preferred_element_type=_F32)
    vt = lax.dot_general(wvt_ref[...], hkv, (((1,), (1,)), ((), ())),
                         preferred_element_type=_F32).astype(_BF16)
    for h in range(MLA_HEADS):
        vt_ref[ATT_VROWS * h:ATT_VROWS * h + MLA_VDIM, :] = vt[MLA_VDIM * h:MLA_VDIM * (h + 1), :]
        vt_ref[ATT_VROWS * h + MLA_VDIM:ATT_VROWS * (h + 1), :] = jnp.ones(
            (ATT_VROWS - MLA_VDIM, vt.shape[1]), _BF16)

    t = krk * csk_ref[...]
    kf2 = t + pltpu.roll(t, MLA_ROPE, 1)
    lane = lax.broadcasted_iota(jnp.int32, kf2.shape, 1)
    k_ref[:, 0:nope_w] = kn.astype(_BF16)
    k_ref[:, nope_w:nope_w + 128] = jnp.where(lane < MLA_ROPE, kf2, 0.0).astype(_BF16)
    k_ref[:, nope_w + 128:nope_w + 256] = jnp.where(lane >= MLA_ROPE, kf2, 0.0).astype(_BF16)


def _mla_prep(p_bm, cos128t, sin128t, csk, qg, kvg, wqt, wkn, wvt):
    tiles_per_seq = SEQ // PREP_TM
    const = lambda i: (0, 0)
    pos = lambda i: (i % tiles_per_seq, 0)
    pos_t = lambda i: (0, i % tiles_per_seq)
    return pl.pallas_call(
        _mla_prep_kernel,
        grid=(TOKENS // PREP_TM,),
        in_specs=[
            pl.BlockSpec((PREP_TM, LATENT_W), lambda i: (i, COL_LATENT)),
            pl.BlockSpec((128, PREP_TM), pos_t),
            pl.BlockSpec((128, PREP_TM), pos_t),
            pl.BlockSpec((PREP_TM, 128), pos),
            pl.BlockSpec((1, MLA_Q_RANK), const),
            pl.BlockSpec((1, MLA_KV_RANK), const),
            pl.BlockSpec(wqt.shape, const),
            pl.BlockSpec(wkn.shape, const),
            pl.BlockSpec(wvt.shape, const),
        ],
        out_specs=[
            pl.BlockSpec((ATT_QROWS, PREP_TM), lambda i: (0, i)),
            pl.BlockSpec((PREP_TM, ATT_KCOLS), lambda i: (i, 0)),
            pl.BlockSpec((MLA_HEADS * ATT_VROWS, PREP_TM), lambda i: (0, i)),
        ],
        out_shape=[
            jax.ShapeDtypeStruct((ATT_QROWS, TOKENS), _BF16),
            jax.ShapeDtypeStruct((TOKENS, ATT_KCOLS), _BF16),
            jax.ShapeDtypeStruct((MLA_HEADS * ATT_VROWS, TOKENS), _BF16),
        ],
        compiler_params=_params("arbitrary"),
        name="mla_prep",
    )(p_bm, cos128t, sin128t, csk, qg, kvg, wqt, wkn, wvt)


def _attn_kernel(qt_ref, k_ref, vt_ref, qt_next_ref, k_next_ref, szb_ref, sgb_ref, wp_ref,
                 x_ref, u_ref, vln_ref, sza_ref, sga_ref, cc_ref, ws_ref, bs_ref,
                 wpa_ref, wo_ref, pg_ref, o_ref,
                 st_scr, mt_scr, m_scr, l_scr, acc_scr, y_scr, ya_scr):
    qi = pl.program_id(1)
    key_chunk = lax.broadcasted_iota(jnp.int32, (ATT_TK, ATT_TQ), 0) // CHUNK
    query_chunk = lax.broadcasted_iota(jnp.int32, (ATT_TK, ATT_TQ), 1) // CHUNK
    diag_mask = key_chunk <= query_chunk

    m_scr[...] = jnp.full(m_scr.shape, -1e30, _F32)
    l_scr[...] = jnp.zeros(l_scr.shape, _F32)
    acc_scr[...] = jnp.zeros(acc_scr.shape, _F32)

    def store_scores(k_tile, q_t, h):
        st = jnp.dot(k_tile, q_t, preferred_element_type=_F32)
        st_scr[h] = st
        mt_scr[h] = jnp.max(st, axis=0, keepdims=True)

    nope_w = MLA_HEADS * MLA_NOPE

    def head_q(q_ref_, h):
        pair = nope_w + 128 * (h // 2)
        return jnp.concatenate([q_ref_[MLA_NOPE * h:MLA_NOPE * (h + 1), :],
                                q_ref_[pair:pair + 128, :]], axis=0)

    def head_k(k_rows, h):
        rope = nope_w + 128 * (h % 2)
        return jnp.concatenate([k_rows[:, MLA_NOPE * h:MLA_NOPE * (h + 1)],
                                k_rows[:, rope:rope + 128]], axis=1)

    def scores(j, h):
        off = pl.multiple_of(j * ATT_TK, ATT_TK)
        store_scores(head_k(k_ref.at[pl.ds(off, ATT_TK), :], h), head_q(qt_ref, h), h)

    def accumulate(j, h, masked):
        off = pl.multiple_of(j * ATT_TK, ATT_TK)
        st = st_scr[h]
        if masked:
            st = jnp.where(diag_mask, st, -1e30)
            mt = jnp.max(st, axis=0, keepdims=True)
        else:
            mt = mt_scr[h]
        m_prev = m_scr[h]
        m_new = jnp.maximum(m_prev, mt)
        alpha = jnp.exp2(m_prev - m_new)
        p = jnp.exp2(st - m_new)
        vt = vt_ref[ATT_VROWS * h:ATT_VROWS * (h + 1), pl.ds(off, ATT_TK)]
        pv = jnp.dot(vt, p.astype(_BF16), preferred_element_type=_F32)
        acc_scr[h] = alpha * acc_scr[h] + pv[0:MLA_VDIM, :]
        l_scr[h] = alpha * l_scr[h] + pv[MLA_VDIM:MLA_VDIM + 1, :]
        m_scr[h] = m_new

    @pl.when((pl.program_id(0) == 0) & (qi == 0))
    def _():
        for h in range(MLA_HEADS):
            scores(0, h)

    def body(j, carry):
        for h in range(MLA_HEADS):
            accumulate(j, h, False)
            scores(j + 1, h)
        return carry

    lax.fori_loop(0, qi, body, 0)
    for h in range(MLA_HEADS):
        accumulate(qi, h, True)
        store_scores(head_k(k_next_ref, h), head_q(qt_next_ref, h), h)

    for h in range(MLA_HEADS):
        hs = slice(MLA_VDIM * h, MLA_VDIM * (h + 1))
        o = (acc_scr[h] * (1.0 / l_scr[h])).T
        y_scr[:, hs] = (o * szb_ref[:, hs].astype(_F32)).astype(_BF16)

    cb = sgb_ref[...].astype(_F32) * jnp.dot(y_scr[...], wp_ref[...], preferred_element_type=_F32)

    idx_r = lax.broadcasted_iota(jnp.int32, (GM_BLOCK, GM_BLOCK), 0) // CHUNK
    idx_c = lax.broadcasted_iota(jnp.int32, (GM_BLOCK, GM_BLOCK), 1) // CHUNK
    causal = idx_c <= idx_r

    for g in range(GM_GROUPS):
        ws = jnp.where(causal, ws_ref[g], 0.0).astype(_BF16)
        cs = slice(GM_GW * g, GM_GW * (g + 1))
        for r in range(ATT_TQ // GM_BLOCK):
            rs = slice(GM_BLOCK * r, GM_BLOCK * (r + 1))
            sv = jnp.dot(ws, vln_ref[rs, cs], preferred_element_type=_F32) + bs_ref[g]
            y = u_ref[rs, cs].astype(_F32) * sv * sza_ref[rs, cs].astype(_F32)
            ya_scr[rs, cs] = y.astype(_BF16)

    ca = sga_ref[...].astype(_F32) * jnp.dot(ya_scr[...], wpa_ref[...], preferred_element_type=_F32)
    merged = ca + cb + cc_ref[...].astype(_F32)
    o = jnp.dot(merged.astype(_BF16), wo_ref[...], preferred_element_type=_F32)
    o = o * lax.rsqrt(jnp.mean(o * o, axis=-1, keepdims=True) + EPS) * pg_ref[...]
    o_ref[...] = x_ref[...] + o


def _attention_merge(qt, k, vt, p_bm, x2d, cc, wpb, ws, bs_col, wpa, wo, pg):
    nq = SEQ // ATT_TQ
    rows = lambda b, i: (b * nq + i, 0)
    next_step = lambda b, i: jnp.minimum(b * nq + i + 1, BATCH * nq - 1)
    seg = lambda c: pl.BlockSpec((ATT_TQ, BM_SEG), lambda b, i: (b * nq + i, c))
    tile = pl.BlockSpec((ATT_TQ, D_MODEL), rows)

    def resident(a):
        return pl.BlockSpec(a.shape, lambda b, i: (0,) * a.ndim, pipeline_mode=pl.Buffered(1))

    return pl.pallas_call(
        _attn_kernel,
        grid=(BATCH, nq),
        in_specs=[
            pl.BlockSpec((ATT_QROWS, ATT_TQ), lambda b, i: (0, b * nq + i)),
            pl.BlockSpec((SEQ, ATT_KCOLS), lambda b, i: (b, 0)),
            pl.BlockSpec((MLA_HEADS * ATT_VROWS, SEQ), lambda b, i: (0, b)),
            pl.BlockSpec((ATT_QROWS, ATT_TQ), lambda b, i: (0, next_step(b, i))),
            pl.BlockSpec((ATT_TK, ATT_KCOLS),
                         lambda b, i: ((next_step(b, i) // nq) * (SEQ // ATT_TK), 0)),
            seg(COL_ZB), seg(COL_GB), resident(wpb),
            tile, seg(COL_U), seg(COL_V), seg(COL_ZA), seg(COL_GA), tile,
            resident(ws), resident(bs_col),
            resident(wpa), resident(wo), resident(pg),
        ],
        out_specs=tile,
        out_shape=jax.ShapeDtypeStruct((TOKENS, D_MODEL), _F32),
        scratch_shapes=[
            pltpu.VMEM((MLA_HEADS, ATT_TK, ATT_TQ), _F32),
            pltpu.VMEM((MLA_HEADS, 1, ATT_TQ), _F32),
            pltpu.VMEM((MLA_HEADS, 1, ATT_TQ), _F32),
            pltpu.VMEM((MLA_HEADS, 1, ATT_TQ), _F32),
            pltpu.VMEM((MLA_HEADS, MLA_VDIM, ATT_TQ), _F32),
            pltpu.VMEM((ATT_TQ, MLA_WIDTH), _BF16),
            pltpu.VMEM((ATT_TQ, GM_WIDTH), _BF16),
        ],
        compiler_params=_params("arbitrary", "arbitrary"),
        name="mla_attn_merge",
    )(qt, k, vt, qt, k, p_bm, p_bm, wpb, x2d, p_bm, p_bm, p_bm, p_bm, cc,
      ws, bs_col, wpa, wo, pg)


def _layer_weights(l, w_in, mla_w_uq, mla_w_ukv, w_proj_a, w_proj_b, w_proj_c, w_out):
    cuts = [0]
    for s in IN_SIZES:
        cuts.append(cuts[-1] + s)
    w_in_t = jnp.swapaxes(w_in, 1, 2)
    seg = lambda k: lax.slice(w_in_t, (l, cuts[k], 0), (l + 1, cuts[k + 1], D_MODEL)).reshape(
        cuts[k + 1] - cuts[k], D_MODEL)
    (u, v, z_a, c_q, c_kv, k_rope, z_b, x_c, z_c, g_a, g_b, g_c) = [seg(k) for k in range(12)]
    half = MLA_ROPE // 2
    k_rope_rot = jnp.concatenate([-k_rope[half:, :], k_rope[:half, :]], axis=0)
    w_bm = jnp.concatenate([u, v, z_a, z_b, g_a, g_b, c_q, c_kv, k_rope, k_rope_rot],
                           axis=0).astype(_BF16)

    wq = mla_w_uq[l].reshape(MLA_Q_RANK, MLA_HEADS, MLA_QK_DIM)
    wq_nope = wq[:, :, :MLA_NOPE].reshape(MLA_Q_RANK, -1)
    wq_rope = wq[:, :, MLA_NOPE:].reshape(MLA_Q_RANK, -1)
    wq_rope_rot = jnp.concatenate([-wq[:, :, MLA_NOPE + half:], wq[:, :, MLA_NOPE:MLA_NOPE + half]],
                                  axis=2).reshape(MLA_Q_RANK, -1)
    wq_all = jnp.concatenate([wq_nope, wq_rope, wq_rope_rot], axis=1).T.astype(_BF16)

    wkv = mla_w_ukv[l].reshape(MLA_KV_RANK, MLA_HEADS, MLA_NOPE + MLA_VDIM)
    wkn = wkv[:, :, :MLA_NOPE].reshape(MLA_KV_RANK, -1).astype(_BF16)
    wvt = wkv[:, :, MLA_NOPE:].reshape(MLA_KV_RANK, -1).T.astype(_BF16)

    return dict(w_bm=w_bm, wxc=x_c.astype(_BF16), wz=z_c.astype(_BF16), wg=g_c.astype(_BF16),
                wq=wq_all, wkn=wkn, wvt=wvt,
                wpa=w_proj_a[l].astype(_BF16), wpb=w_proj_b[l].astype(_BF16),
                wpc=w_proj_c[l].astype(_BF16), wo=w_out[l].astype(_BF16))


def _rope_tables():
    pos = jnp.arange(SEQ, dtype=_F32)
    inv_freq = ROPE_THETA ** (-jnp.arange(0, MLA_ROPE, 2, dtype=_F32) / MLA_ROPE)
    ang = pos[:, None] * inv_freq[None, :]
    cos = jnp.cos(ang)
    sin = jnp.sin(ang)
    cos128t = jnp.tile(cos, (1, 4)).T
    sin128t = jnp.tile(sin, (1, 4)).T
    csk = jnp.concatenate([cos, cos, sin, sin], axis=1)
    return cos128t, sin128t, csk


def kernel(x, pre_norm_g, w_in, gm_ln_g, gm_ln_b, gm_ws, gm_bs, mla_q_norm_g, mla_w_uq,
           mla_kv_norm_g, mla_w_ukv, lru_conv_w, lru_conv_b, lru_w_a, lru_b_a, lru_w_x,
           lru_b_x, lru_lambda, w_proj_a, w_proj_b, w_proj_c, w_out, post_norm_g):
    cos128t, sin128t, csk = _rope_tables()
    wax = _gate_weights(lru_w_a, lru_w_x)
    x2d = x.reshape(TOKENS, D_MODEL)
    row = lambda a: a.reshape(1, -1)
    for l in range(DEPTH):
        w = _layer_weights(l, w_in, mla_w_uq, mla_w_ukv, w_proj_a, w_proj_b, w_proj_c, w_out)
        g_pre = row(pre_norm_g[l])
        p_bm = _proj(x2d, g_pre, row(gm_ln_g[l]), row(gm_ln_b[l]), w["w_bm"])

        cc = _lru_mixer(x2d.reshape(BATCH, SEQ, D_MODEL), g_pre, w["wxc"], w["wz"], w["wg"],
                        lru_conv_w[l], row(lru_conv_b[l]), wax[l], row(lru_b_a[l]),
                        row(lru_b_x[l]), row(lru_lambda[l]), w["wpc"]).reshape(TOKENS, D_MODEL)

        qt, k, vt = _mla_prep(p_bm, cos128t, sin128t, csk, row(mla_q_norm_g[l]),
                              row(mla_kv_norm_g[l]), w["wq"], w["wkn"], w["wvt"])
        x2d = _attention_merge(qt, k, vt, p_bm, x2d, cc, w["wpb"], gm_ws[l],
                               gm_bs[l][:, :, None], w["wpa"], w["wo"], row(post_norm_g[l]))
    return x2d.reshape(BATCH, SEQ, D_MODEL)
```

```python
import math

import jax
import jax.numpy as jnp
from jax import lax
from jax.experimental import pallas as pl
from jax.experimental.pallas import tpu as pltpu

D_MODEL = 1024
BATCH = 8
SEQ = 2048
DEPTH = 2
TOKENS = BATCH * SEQ
CHUNK = 64
EPS = 1e-6

GM_WIDTH = 1024
GM_GROUPS = 4
GM_BLOCK = 128
GM_GW = GM_WIDTH // GM_GROUPS

MLA_HEADS = 8
MLA_NOPE = 128
MLA_ROPE = 64
MLA_VDIM = 128
MLA_QK_DIM = MLA_NOPE + MLA_ROPE
MLA_Q_RANK = 384
MLA_KV_RANK = 256
MLA_WIDTH = MLA_HEADS * MLA_VDIM
ROPE_THETA = 10000.0

LRU_WIDTH = 1280
LRU_BLOCKS = 16
LRU_BW = LRU_WIDTH // LRU_BLOCKS
LRU_C = 8.0
CONV_W = 4

IN_SIZES = (GM_WIDTH, GM_WIDTH, GM_WIDTH, MLA_Q_RANK, MLA_KV_RANK, MLA_ROPE, MLA_WIDTH,
            LRU_WIDTH, LRU_WIDTH, D_MODEL, D_MODEL, D_MODEL)
IN_CUTS = tuple(sum(IN_SIZES[:k]) for k in range(len(IN_SIZES) + 1))
(IN_XC, IN_ZC, IN_GC) = (7, 8, 11)

BM_SEG = 1024
(COL_U, COL_V, COL_ZA, COL_ZB, COL_GA, COL_GB) = range(6)
LATENT_W = MLA_Q_RANK + MLA_KV_RANK + 2 * MLA_ROPE
N_BM = 6 * BM_SEG + LATENT_W
COL_LATENT = (6 * BM_SEG) // LATENT_W

VMEM_LIMIT_BYTES = 56 * 1024 * 1024

PROJ_TM = 1024
PROJ_TN = 2304
LRU_TS = 64
GATE_TN = 256
PREP_TM = 1024
ATT_TQ = 256
ATT_TK = 256
ATT_D = 256
ATT_QROWS = MLA_HEADS * MLA_NOPE + (MLA_HEADS // 2) * 128
ATT_KCOLS = MLA_HEADS * MLA_NOPE + 2 * 128
ATT_VROWS = MLA_VDIM + 16

_F32 = jnp.float32
_BF16 = jnp.bfloat16


def _params(*sem):
    return pltpu.CompilerParams(dimension_semantics=sem, vmem_limit_bytes=VMEM_LIMIT_BYTES)


def _sigmoid(x):
    return 0.5 * jnp.tanh(0.5 * x) + 0.5


def _silu(x):
    h = 0.5 * x
    return h * jnp.tanh(h) + h


def _rmsnorm_bf16(x, g):
    ms = jnp.mean(x * x, axis=-1, keepdims=True)
    return (x * lax.rsqrt(ms + EPS) * g).astype(_BF16)


def _dot_nt(a, b_t):
    return lax.dot_general(a, b_t, (((1,), (1,)), ((), ())), preferred_element_type=_F32)


BM_SOURCES = ((0, GM_WIDTH, "raw"), (1, GM_WIDTH, "ln"), (2, GM_WIDTH, "silu"), (6, MLA_WIDTH, "silu"),
              (9, D_MODEL, "sigmoid"), (10, D_MODEL, "sigmoid"),
              (3, MLA_Q_RANK + MLA_KV_RANK + MLA_ROPE, "raw"))
PROJ_SLOT_ROWS = (1024, 1024, 768)


def _proj_plan(step):
    lo, hi = step * PROJ_TN, (step + 1) * PROJ_TN
    plan, c0 = [], 0
    for k, width, kind in BM_SOURCES:
        a, b = max(c0, lo), min(c0 + width, hi)
        if a < b:
            assert kind != "ln" or b - a == width, "layernorm needs its whole segment in one step"
            plan.append((a - lo, b - lo, kind, IN_CUTS[k] + a - c0))
        c0 += width
    assert len(plan) == len(PROJ_SLOT_ROWS)
    assert all(c1 - c0 <= rows for (c0, c1, _, _), rows in zip(plan, PROJ_SLOT_ROWS))
    return plan


def _proj_kernel(x_ref, g_ref, lng_ref, lnb_ref, w0_ref, w1_ref, w2_ref, wrot_ref, o_ref, h_scr):
    j = pl.program_id(1)

    @pl.when(j == 0)
    def _():
        h_scr[...] = _rmsnorm_bf16(x_ref[...], g_ref[...])

    def piece(c0, c1, kind, w_rows):
        r = _dot_nt(h_scr[...], w_rows)
        if kind == "ln":
            mu = jnp.mean(r, axis=-1, keepdims=True)
            rc = r - mu
            var = jnp.mean(rc * rc, axis=-1, keepdims=True)
            r = rc * lax.rsqrt(var + EPS) * lng_ref[...] + lnb_ref[...]
        elif kind == "silu":
            r = _silu(r)
        elif kind == "sigmoid":
            r = _sigmoid(r)
        o_ref[:, c0:c1] = r.astype(o_ref.dtype)

    n_steps = N_BM // PROJ_TN
    for step in range(n_steps):
        @pl.when(j == step)
        def _(step=step):
            for (c0, c1, kind, _), w_ref in zip(_proj_plan(step), (w0_ref, w1_ref, w2_ref)):
                piece(c0, c1, kind, w_ref[0, 0:c1 - c0, :])
            if step == n_steps - 1:
                piece(PROJ_TN - MLA_ROPE, PROJ_TN, "raw", wrot_ref[...])


def _proj(x2d, g, lng, lnb, w_in_t, layer, w_rot):
    vec = pl.BlockSpec((1, D_MODEL), lambda i, j: (0, 0))
    n_steps = N_BM // PROJ_TN
    plans = [_proj_plan(step) for step in range(n_steps)]

    def w_rows(slot):
        starts = [plans[step][slot][3] for step in range(n_steps)]

        def index_map(i, j):
            row = starts[-1]
            for step in range(n_steps - 2, -1, -1):
                row = jnp.where(j == step, starts[step], row)
            return (layer, row, 0)

        return pl.BlockSpec((pl.Element(1), pl.Element(PROJ_SLOT_ROWS[slot]), pl.Element(D_MODEL)),
                            index_map)

    return pl.pallas_call(
        _proj_kernel,
        grid=(TOKENS // PROJ_TM, n_steps),
        in_specs=[
            pl.BlockSpec((PROJ_TM, D_MODEL), lambda i, j: (i, 0)),
            vec, vec, vec,
            w_rows(0), w_rows(1), w_rows(2),
            pl.BlockSpec((MLA_ROPE, D_MODEL), lambda i, j: (0, 0)),
        ],
        out_specs=pl.BlockSpec((PROJ_TM, PROJ_TN), lambda i, j: (i, j)),
        out_shape=jax.ShapeDtypeStruct((TOKENS, N_BM), _BF16),
        scratch_shapes=[pltpu.VMEM((PROJ_TM, D_MODEL), _BF16)],
        compiler_params=_params("arbitrary", "arbitrary"),
        name="proj_bm",
    )(x2d, g, lng, lnb, w_in_t, w_in_t, w_in_t, w_rot)


def _gate_k_range(c):
    first_block = (c * GATE_TN) // LRU_BW
    last_block = ((c + 1) * GATE_TN - 1) // LRU_BW
    k0 = (first_block * LRU_BW) // GATE_TN * GATE_TN
    k1 = -(-((last_block + 1) * LRU_BW) // GATE_TN) * GATE_TN
    return k0, k1


GATE_K_MAX = max(k1 - k0 for k0, k1 in map(_gate_k_range, range(LRU_WIDTH // GATE_TN)))


def _gate_weights_kernel(wa_ref, wx_ref, o_ref, stage):
    for c in range(LRU_WIDTH // GATE_TN):
        k0, _ = _gate_k_range(c)
        stage[...] = jnp.zeros(stage.shape, _F32)
        for h in range(LRU_BLOCKS):
            lo = max(LRU_BW * h, GATE_TN * c)
            hi = min(LRU_BW * (h + 1), GATE_TN * (c + 1))
            if lo >= hi:
                continue
            r0 = LRU_BW * h - k0
            src = slice(lo - LRU_BW * h, hi - LRU_BW * h)
            dst = slice(lo - GATE_TN * c, hi - GATE_TN * c)
            stage[r0:r0 + LRU_BW, dst] = wa_ref[h][:, src]
            stage[r0:r0 + LRU_BW, GATE_TN + dst.start:GATE_TN + dst.stop] = wx_ref[h][:, src]
        o_ref[c] = stage[...].astype(o_ref.dtype)


def _gate_weights(w_a, w_x):
    n_tiles = LRU_WIDTH // GATE_TN
    blocks = pl.BlockSpec((None, LRU_BLOCKS, LRU_BW, LRU_BW), lambda l: (l, 0, 0, 0))
    return pl.pallas_call(
        _gate_weights_kernel,
        grid=(DEPTH,),
        in_specs=[blocks, blocks],
        out_specs=pl.BlockSpec((None, n_tiles, GATE_K_MAX, 2 * GATE_TN), lambda l: (l, 0, 0, 0)),
        out_shape=jax.ShapeDtypeStruct((DEPTH, n_tiles, GATE_K_MAX, 2 * GATE_TN), _BF16),
        scratch_shapes=[pltpu.VMEM((GATE_K_MAX, 2 * GATE_TN), _F32)],
        compiler_params=_params("arbitrary"),
        name="gate_weights",
    )(w_a, w_x)


def _lru_kernel(x_ref, pg_ref, wxc_ref, wz_ref, wg_ref, cw_ref, cb_ref, wax_ref, ba_ref, bx_ref,
                lam_ref, wp_ref, o_ref, xext, a_scr, b_scr, h_scr, sz_scr, sg_scr):
    rows = LRU_TS * BATCH
    halo = (CONV_W - 1) * BATCH

    @pl.when(pl.program_id(0) == 0)
    def _():
        xext[0:halo, :] = jnp.zeros((halo, LRU_WIDTH), _F32)
        h_scr[...] = jnp.zeros_like(h_scr)

    xt = jnp.swapaxes(x_ref[...], 0, 1).reshape(rows, D_MODEL)
    hn = _rmsnorm_bf16(xt, pg_ref[...])

    xext[halo:halo + rows, :] = _dot_nt(hn, wxc_ref[0])
    xc = cb_ref[...] + cw_ref[CONV_W - 1:CONV_W, :] * xext[halo:halo + rows, :]
    for k in range(CONV_W - 1):
        xc = xc + cw_ref[k:k + 1, :] * xext[k * BATCH:k * BATCH + rows, :]
    xext[0:halo, :] = xext[rows:rows + halo, :]

    xcb = xc.astype(_BF16)
    lam = lam_ref[...]
    softplus_neg_lam = jnp.maximum(-lam, 0.0) + jnp.log(1.0 + jnp.exp(-jnp.abs(lam)))
    for c in range(LRU_WIDTH // GATE_TN):
        cs = slice(c * GATE_TN, (c + 1) * GATE_TN)
        k0, k1 = _gate_k_range(c)
        ri = jnp.dot(xcb[:, k0:k1], wax_ref[c, 0:k1 - k0, :], preferred_element_type=_F32)
        half_log = (-0.5 * LRU_C) * softplus_neg_lam[:, cs]
        a = jnp.exp(half_log * jnp.tanh(0.5 * (ri[:, 0:GATE_TN] + ba_ref[:, cs])) + half_log)
        i = _sigmoid(ri[:, GATE_TN:2 * GATE_TN] + bx_ref[:, cs])
        gap = 1.0 - a * a
        mult = jnp.where(gap > 0.0, gap * lax.rsqrt(gap), 0.0)
        a_scr[:, cs] = a
        b_scr[:, cs] = mult * (i * xc[:, cs])

    sz_scr[...] = _silu(_dot_nt(hn, wz_ref[0]))
    sg_scr[...] = _sigmoid(_dot_nt(hn, wg_ref[0]))

    h = h_scr[...]
    for s in range(LRU_TS):
        rs = slice(s * BATCH, (s + 1) * BATCH)
        h = a_scr[rs, :] * h + b_scr[rs, :]
        b_scr[rs, :] = h
    h_scr[...] = h

    y = (b_scr[...] * sz_scr[...]).astype(_BF16)
    cc = sg_scr[...] * jnp.dot(y, wp_ref[...], preferred_element_type=_F32)
    o_ref[...] = jnp.swapaxes(cc.reshape(LRU_TS, BATCH, D_MODEL), 0, 1).astype(o_ref.dtype)


def _lru_mixer(x3d, pre_g, w_in_t, layer, conv_w, conv_b, wax, ba, bx, lam, wp):
    rows = LRU_TS * BATCH
    halo = (CONV_W - 1) * BATCH

    def resident(a):
        return pl.BlockSpec(a.shape, lambda t: (0,) * a.ndim, pipeline_mode=pl.Buffered(1))

    def w_rows(k):
        return pl.BlockSpec((pl.Element(1), pl.Element(IN_SIZES[k]), pl.Element(D_MODEL)),
                            lambda t: (layer, IN_CUTS[k], 0), pipeline_mode=pl.Buffered(1))

    return pl.pallas_call(
        _lru_kernel,
        grid=(SEQ // LRU_TS,),
        in_specs=[pl.BlockSpec((BATCH, LRU_TS, D_MODEL), lambda t: (0, t, 0)), resident(pre_g),
                  w_rows(IN_XC), w_rows(IN_ZC), w_rows(IN_GC)]
        + [resident(a) for a in (conv_w, conv_b, wax, ba, bx, lam, wp)],
        out_specs=pl.BlockSpec((BATCH, LRU_TS, D_MODEL), lambda t: (0, t, 0)),
        out_shape=jax.ShapeDtypeStruct((BATCH, SEQ, D_MODEL), _BF16),
        scratch_shapes=[
            pltpu.VMEM((rows + halo, LRU_WIDTH), _F32),
            pltpu.VMEM((rows, LRU_WIDTH), _F32),
            pltpu.VMEM((rows, LRU_WIDTH), _F32),
            pltpu.VMEM((BATCH, LRU_WIDTH), _F32),
            pltpu.VMEM((rows, LRU_WIDTH), _F32),
            pltpu.VMEM((rows, D_MODEL), _F32),
        ],
        compiler_params=_params("arbitrary"),
        name="lru_mixer",
    )(x3d, pre_g, w_in_t, w_in_t, w_in_t, conv_w, conv_b, wax, ba, bx, lam, wp)


def _mla_prep_kernel(c_ref, cost_ref, sint_ref, csk_ref, qg_ref, kvg_ref, wqt_ref, wkn_ref, wvt_ref,
                     qt_ref, k_ref, vt_ref):
    qscale = math.log2(math.e) / math.sqrt(MLA_QK_DIM)
    c = c_ref[...].astype(_F32)
    cq = c[:, 0:MLA_Q_RANK]
    ckv = c[:, MLA_Q_RANK:MLA_Q_RANK + MLA_KV_RANK]
    krk = c[:, MLA_Q_RANK + MLA_KV_RANK:LATENT_W]

    hq = (cq * lax.rsqrt(jnp.mean(cq * cq, axis=-1, keepdims=True) + EPS)
          * qg_ref[...]).astype(_BF16)
    qt = lax.dot_general(wqt_ref[...], hq, (((1,), (1,)), ((), ())), preferred_element_type=_F32)
    nope_w = MLA_HEADS * MLA_NOPE
    rope_w = MLA_HEADS * MLA_ROPE
    cost = cost_ref[...]
    sint = sint_ref[...]
    for p in range(rope_w // 128):
        qr = qt[nope_w + 128 * p:nope_w + 128 * (p + 1), :]
        qrr = qt[nope_w + rope_w + 128 * p:nope_w + rope_w + 128 * (p + 1), :]
        qt_ref[nope_w + 128 * p:nope_w + 128 * (p + 1), :] = (
            (qr * cost + qrr * sint) * qscale).astype(_BF16)
    qt_ref[0:nope_w, :] = (qt[0:nope_w, :] * qscale).astype(_BF16)

    hkv = (ckv * lax.rsqrt(jnp.mean(ckv * ckv, axis=-1, keepdims=True) + EPS)
           * kvg_ref[...]).astype(_BF16)
    kn = jnp.dot(hkv, wkn_ref[...], preferred_element_type=_F32)
    vt = lax.dot_general(wvt_ref[...], hkv, (((1,), (1,)), ((), ())),
                         preferred_element_type=_F32).astype(_BF16)
    for h in range(MLA_HEADS):
        vt_ref[ATT_VROWS * h:ATT_VROWS * h + MLA_VDIM, :] = vt[MLA_VDIM * h:MLA_VDIM * (h + 1), :]
        vt_ref[ATT_VROWS * h + MLA_VDIM:ATT_VROWS * (h + 1), :] = jnp.ones(
            (ATT_VROWS - MLA_VDIM, vt.shape[1]), _BF16)

    t = krk * csk_ref[...]
    kf2 = t + pltpu.roll(t, MLA_ROPE, 1)
    lane = lax.broadcasted_iota(jnp.int32, kf2.shape, 1)
    k_ref[:, 0:nope_w] = kn.astype(_BF16)
    k_ref[:, nope_w:nope_w + 128] = jnp.where(lane < MLA_ROPE, kf2, 0.0).astype(_BF16)
    k_ref[:, nope_w + 128:nope_w + 256] = jnp.where(lane >= MLA_ROPE, kf2, 0.0).astype(_BF16)


def _mla_prep(p_bm, cos128t, sin128t, csk, qg, kvg, wqt, wkn, wvt):
    tiles_per_seq = SEQ // PREP_TM
    const = lambda i: (0, 0)
    pos = lambda i: (i % tiles_per_seq, 0)
    pos_t = lambda i: (0, i % tiles_per_seq)
    return pl.pallas_call(
        _mla_prep_kernel,
        grid=(TOKENS // PREP_TM,),
        in_specs=[
            pl.BlockSpec((PREP_TM, LATENT_W), lambda i: (i, COL_LATENT)),
            pl.BlockSpec((128, PREP_TM), pos_t),
            pl.BlockSpec((128, PREP_TM), pos_t),
            pl.BlockSpec((PREP_TM, 128), pos),
            pl.BlockSpec((1, MLA_Q_RANK), const),
            pl.BlockSpec((1, MLA_KV_RANK), const),
            pl.BlockSpec(wqt.shape, const),
            pl.BlockSpec(wkn.shape, const),
            pl.BlockSpec(wvt.shape, const),
        ],
        out_specs=[
            pl.BlockSpec((ATT_QROWS, PREP_TM), lambda i: (0, i)),
            pl.BlockSpec((PREP_TM, ATT_KCOLS), lambda i: (i, 0)),
            pl.BlockSpec((MLA_HEADS * ATT_VROWS, PREP_TM), lambda i: (0, i)),
        ],
        out_shape=[
            jax.ShapeDtypeStruct((ATT_QROWS, TOKENS), _BF16),
            jax.ShapeDtypeStruct((TOKENS, ATT_KCOLS), _BF16),
            jax.ShapeDtypeStruct((MLA_HEADS * ATT_VROWS, TOKENS), _BF16),
        ],
        compiler_params=_params("arbitrary"),
        name="mla_prep",
    )(p_bm, cos128t, sin128t, csk, qg, kvg, wqt, wkn, wvt)


def _attn_kernel(qt_ref, k_ref, vt_ref, qt_next_ref, k_next_ref, szb_ref, sgb_ref, wp_ref,
                 x_ref, u_ref, vln_ref, sza_ref, sga_ref, cc_ref, ws_ref, bs_ref,
                 wpa_ref, wo_ref, pg_ref, o_ref,
                 st_scr, mt_scr, m_scr, l_scr, acc_scr, y_scr, ya_scr):
    qi = pl.program_id(1)
    key_chunk = lax.broadcasted_iota(jnp.int32, (ATT_TK, ATT_TQ), 0) // CHUNK
    query_chunk = lax.broadcasted_iota(jnp.int32, (ATT_TK, ATT_TQ), 1) // CHUNK
    diag_mask = key_chunk <= query_chunk

    m_scr[...] = jnp.full(m_scr.shape, -1e30, _F32)
    l_scr[...] = jnp.zeros(l_scr.shape, _F32)
    acc_scr[...] = jnp.zeros(acc_scr.shape, _F32)

    def store_scores(k_tile, q_t, h):
        st = jnp.dot(k_tile, q_t, preferred_element_type=_F32)
        st_scr[h] = st
        mt_scr[h] = jnp.max(st, axis=0, keepdims=True)

    nope_w = MLA_HEADS * MLA_NOPE

    def head_q(q_ref_, h):
        pair = nope_w + 128 * (h // 2)
        return jnp.concatenate([q_ref_[MLA_NOPE * h:MLA_NOPE * (h + 1), :],
                                q_ref_[pair:pair + 128, :]], axis=0)

    def head_k(k_rows, h):
        rope = nope_w + 128 * (h % 2)
        return jnp.concatenate([k_rows[:, MLA_NOPE * h:MLA_NOPE * (h + 1)],
                                k_rows[:, rope:rope + 128]], axis=1)

    def scores(j, h):
        off = pl.multiple_of(j * ATT_TK, ATT_TK)
        store_scores(head_k(k_ref.at[pl.ds(off, ATT_TK), :], h), head_q(qt_ref, h), h)

    def accumulate(j, h, masked):
        off = pl.multiple_of(j * ATT_TK, ATT_TK)
        st = st_scr[h]
        if masked:
            st = jnp.where(diag_mask, st, -1e30)
            mt = jnp.max(st, axis=0, keepdims=True)
        else:
            mt = mt_scr[h]
        m_prev = m_scr[h]
        m_new = jnp.maximum(m_prev, mt)
        alpha = jnp.exp2(m_prev - m_new)
        p = jnp.exp2(st - m_new)
        vt = vt_ref[ATT_VROWS * h:ATT_VROWS * (h + 1), pl.ds(off, ATT_TK)]
        pv = jnp.dot(vt, p.astype(_BF16), preferred_element_type=_F32)
        acc_scr[h] = alpha * acc_scr[h] + pv[0:MLA_VDIM, :]
        l_scr[h] = alpha * l_scr[h] + pv[MLA_VDIM:MLA_VDIM + 1, :]
        m_scr[h] = m_new

    @pl.when((pl.program_id(0) == 0) & (qi == 0))
    def _():
        for h in range(MLA_HEADS):
            scores(0, h)

    def body(j, carry):
        for h in range(MLA_HEADS):
            accumulate(j, h, False)
            scores(j + 1, h)
        return carry

    lax.fori_loop(0, qi, body, 0)
    for h in range(MLA_HEADS):
        accumulate(qi, h, True)
        store_scores(head_k(k_next_ref, h), head_q(qt_next_ref, h), h)

    for h in range(MLA_HEADS):
        hs = slice(MLA_VDIM * h, MLA_VDIM * (h + 1))
        o = (acc_scr[h] * (1.0 / l_scr[h])).T
        y_scr[:, hs] = (o * szb_ref[:, hs].astype(_F32)).astype(_BF16)

    cb = sgb_ref[...].astype(_F32) * jnp.dot(y_scr[...], wp_ref[...], preferred_element_type=_F32)

    idx_r = lax.broadcasted_iota(jnp.int32, (GM_BLOCK, GM_BLOCK), 0) // CHUNK
    idx_c = lax.broadcasted_iota(jnp.int32, (GM_BLOCK, GM_BLOCK), 1) // CHUNK
    causal = idx_c <= idx_r

    for g in range(GM_GROUPS):
        ws = jnp.where(causal, ws_ref[g], 0.0).astype(_BF16)
        cs = slice(GM_GW * g, GM_GW * (g + 1))
        for r in range(ATT_TQ // GM_BLOCK):
            rs = slice(GM_BLOCK * r, GM_BLOCK * (r + 1))
            sv = jnp.dot(ws, vln_ref[rs, cs], preferred_element_type=_F32) + bs_ref[g]
            y = u_ref[rs, cs].astype(_F32) * sv * sza_ref[rs, cs].astype(_F32)
            ya_scr[rs, cs] = y.astype(_BF16)

    ca = sga_ref[...].astype(_F32) * jnp.dot(ya_scr[...], wpa_ref[...], preferred_element_type=_F32)
    merged = ca + cb + cc_ref[...].astype(_F32)
    o = jnp.dot(merged.astype(_BF16), wo_ref[...], preferred_element_type=_F32)
    o = o * lax.rsqrt(jnp.mean(o * o, axis=-1, keepdims=True) + EPS) * pg_ref[...]
    o_ref[...] = x_ref[...] + o


def _attention_merge(qt, k, vt, p_bm, x2d, cc, wpb, ws, bs_col, wpa, wo, pg):
    nq = SEQ // ATT_TQ
    rows = lambda b, i: (b * nq + i, 0)
    next_step = lambda b, i: jnp.minimum(b * nq + i + 1, BATCH * nq - 1)
    seg = lambda c: pl.BlockSpec((ATT_TQ, BM_SEG), lambda b, i: (b * nq + i, c))
    tile = pl.BlockSpec((ATT_TQ, D_MODEL), rows)

    def resident(a):
        return pl.BlockSpec(a.shape, lambda b, i: (0,) * a.ndim, pipeline_mode=pl.Buffered(1))

    return pl.pallas_call(
        _attn_kernel,
        grid=(BATCH, nq),
        in_specs=[
            pl.BlockSpec((ATT_QROWS, ATT_TQ), lambda b, i: (0, b * nq + i)),
            pl.BlockSpec((SEQ, ATT_KCOLS), lambda b, i: (b, 0)),
            pl.BlockSpec((MLA_HEADS * ATT_VROWS, SEQ), lambda b, i: (0, b)),
            pl.BlockSpec((ATT_QROWS, ATT_TQ), lambda b, i: (0, next_step(b, i))),
            pl.BlockSpec((ATT_TK, ATT_KCOLS),
                         lambda b, i: ((next_step(b, i) // nq) * (SEQ // ATT_TK), 0)),
            seg(COL_ZB), seg(COL_GB), resident(wpb),
            tile, seg(COL_U), seg(COL_V), seg(COL_ZA), seg(COL_GA), tile,
            resident(ws), resident(bs_col),
            resident(wpa), resident(wo), resident(pg),
        ],
        out_specs=tile,
        out_shape=jax.ShapeDtypeStruct((TOKENS, D_MODEL), _F32),
        scratch_shapes=[
            pltpu.VMEM((MLA_HEADS, ATT_TK, ATT_TQ), _F32),
            pltpu.VMEM((MLA_HEADS, 1, ATT_TQ), _F32),
            pltpu.VMEM((MLA_HEADS, 1, ATT_TQ), _F32),
            pltpu.VMEM((MLA_HEADS, 1, ATT_TQ), _F32),
            pltpu.VMEM((MLA_HEADS, MLA_VDIM, ATT_TQ), _F32),
            pltpu.VMEM((ATT_TQ, MLA_WIDTH), _BF16),
            pltpu.VMEM((ATT_TQ, GM_WIDTH), _BF16),
        ],
        compiler_params=_params("arbitrary", "arbitrary"),
        name="mla_attn_merge",
    )(qt, k, vt, qt, k, p_bm, p_bm, wpb, x2d, p_bm, p_bm, p_bm, p_bm, cc,
      ws, bs_col, wpa, wo, pg)


def _layer_weights(l, w_in_t, mla_w_uq, mla_w_ukv, w_proj_a, w_proj_b, w_proj_c, w_out):
    k_rope = lax.slice(w_in_t, (l, IN_CUTS[5], 0), (l + 1, IN_CUTS[6], D_MODEL)).reshape(
        MLA_ROPE, D_MODEL)
    half = MLA_ROPE // 2
    k_rope_rot = jnp.concatenate([-k_rope[half:, :], k_rope[:half, :]], axis=0)

    wq = mla_w_uq[l].reshape(MLA_Q_RANK, MLA_HEADS, MLA_QK_DIM)
    wq_nope = wq[:, :, :MLA_NOPE].reshape(MLA_Q_RANK, -1)
    wq_rope = wq[:, :, MLA_NOPE:].reshape(MLA_Q_RANK, -1)
    wq_rope_rot = jnp.concatenate([-wq[:, :, MLA_NOPE + half:], wq[:, :, MLA_NOPE:MLA_NOPE + half]],
                                  axis=2).reshape(MLA_Q_RANK, -1)
    wq_all = jnp.concatenate([wq_nope, wq_rope, wq_rope_rot], axis=1).T.astype(_BF16)

    wkv = mla_w_ukv[l].reshape(MLA_KV_RANK, MLA_HEADS, MLA_NOPE + MLA_VDIM)
    wkn = wkv[:, :, :MLA_NOPE].reshape(MLA_KV_RANK, -1).astype(_BF16)
    wvt = wkv[:, :, MLA_NOPE:].reshape(MLA_KV_RANK, -1).T.astype(_BF16)

    return dict(w_rot=k_rope_rot, wq=wq_all, wkn=wkn, wvt=wvt,
                wpa=w_proj_a[l].astype(_BF16), wpb=w_proj_b[l].astype(_BF16),
                wpc=w_proj_c[l].astype(_BF16), wo=w_out[l].astype(_BF16))


def _rope_tables():
    pos = jnp.arange(SEQ, dtype=_F32)
    inv_freq = ROPE_THETA ** (-jnp.arange(0, MLA_ROPE, 2, dtype=_F32) / MLA_ROPE)
    ang = pos[:, None] * inv_freq[None, :]
    cos = jnp.cos(ang)
    sin = jnp.sin(ang)
    cos128t = jnp.tile(cos, (1, 4)).T
    sin128t = jnp.tile(sin, (1, 4)).T
    csk = jnp.concatenate([cos, cos, sin, sin], axis=1)
    return cos128t, sin128t, csk


def kernel(x, pre_norm_g, w_in, gm_ln_g, gm_ln_b, gm_ws, gm_bs, mla_q_norm_g, mla_w_uq,
           mla_kv_norm_g, mla_w_ukv, lru_conv_w, lru_conv_b, lru_w_a, lru_b_a, lru_w_x,
           lru_b_x, lru_lambda, w_proj_a, w_proj_b, w_proj_c, w_out, post_norm_g):
    cos128t, sin128t, csk = _rope_tables()
    wax = _gate_weights(lru_w_a, lru_w_x)
    w_in_t = jnp.swapaxes(w_in, 1, 2).astype(_BF16)
    x2d = x.reshape(TOKENS, D_MODEL)
    row = lambda a: a.reshape(1, -1)
    for l in range(DEPTH):
        w = _layer_weights(l, w_in_t, mla_w_uq, mla_w_ukv, w_proj_a, w_proj_b, w_proj_c, w_out)
        g_pre = row(pre_norm_g[l])
        p_bm = _proj(x2d, g_pre, row(gm_ln_g[l]), row(gm_ln_b[l]), w_in_t, l, w["w_rot"])

        cc = _lru_mixer(x2d.reshape(BATCH, SEQ, D_MODEL), g_pre, w_in_t, l,
                        lru_conv_w[l], row(lru_conv_b[l]), wax[l], row(lru_b_a[l]),
                        row(lru_b_x[l]), row(lru_lambda[l]), w["wpc"]).reshape(TOKENS, D_MODEL)

        qt, k, vt = _mla_prep(p_bm, cos128t, sin128t, csk, row(mla_q_norm_g[l]),
                              row(mla_kv_norm_g[l]), w["wq"], w["wkn"], w["wvt"])
        x2d = _attention_merge(qt, k, vt, p_bm, x2d, cc, w["wpb"], gm_ws[l],
                               gm_bs[l][:, :, None], w["wpa"], w["wo"], row(post_norm_g[l]))
    return x2d.reshape(BATCH, SEQ, D_MODEL)
```

```python
import math

import jax
import jax.numpy as jnp
from jax import lax
from jax.experimental import pallas as pl
from jax.experimental.pallas import tpu as pltpu

D_MODEL = 1024
BATCH = 8
SEQ = 2048
DEPTH = 2
TOKENS = BATCH * SEQ
CHUNK = 64
EPS = 1e-6

GM_WIDTH = 1024
GM_GROUPS = 4
GM_BLOCK = 128
GM_GW = GM_WIDTH // GM_GROUPS

MLA_HEADS = 8
MLA_NOPE = 128
MLA_ROPE = 64
MLA_VDIM = 128
MLA_QK_DIM = MLA_NOPE + MLA_ROPE
MLA_Q_RANK = 384
MLA_KV_RANK = 256
MLA_WIDTH = MLA_HEADS * MLA_VDIM
ROPE_THETA = 10000.0

LRU_WIDTH = 1280
LRU_BLOCKS = 16
LRU_BW = LRU_WIDTH // LRU_BLOCKS
LRU_C = 8.0
CONV_W = 4

IN_SIZES = (GM_WIDTH, GM_WIDTH, GM_WIDTH, MLA_Q_RANK, MLA_KV_RANK, MLA_ROPE, MLA_WIDTH,
            LRU_WIDTH, LRU_WIDTH, D_MODEL, D_MODEL, D_MODEL)
IN_CUTS = tuple(sum(IN_SIZES[:k]) for k in range(len(IN_SIZES) + 1))
(IN_XC, IN_ZC, IN_GC) = (7, 8, 11)

BM_SEG = 1024
(COL_U, COL_V, COL_ZA, COL_ZB, COL_GA, COL_GB) = range(6)
LATENT_W = MLA_Q_RANK + MLA_KV_RANK + 2 * MLA_ROPE
N_BM = 6 * BM_SEG + LATENT_W
COL_LATENT = (6 * BM_SEG) // LATENT_W

VMEM_LIMIT_BYTES = 56 * 1024 * 1024

PROJ_TM = 1024
PROJ_TN = 2304
LRU_TS = 64
GATE_TN = 256
PREP_TM = 1024
ATT_TQ = 256
ATT_TK = 256
ATT_D = 256
ATT_QROWS = MLA_HEADS * MLA_NOPE + (MLA_HEADS // 2) * 128
ATT_KCOLS = MLA_HEADS * MLA_NOPE + 2 * 128
ATT_VROWS = MLA_VDIM + 16

_F32 = jnp.float32
_BF16 = jnp.bfloat16


def _params(*sem):
    return pltpu.CompilerParams(dimension_semantics=sem, vmem_limit_bytes=VMEM_LIMIT_BYTES)


def _sigmoid(x):
    return 0.5 * jnp.tanh(0.5 * x) + 0.5


def _silu(x):
    h = 0.5 * x
    return h * jnp.tanh(h) + h


def _rmsnorm_bf16(x, g):
    ms = jnp.mean(x * x, axis=-1, keepdims=True)
    return (x * lax.rsqrt(ms + EPS) * g).astype(_BF16)


def _dot_nt(a, b_t):
    return lax.dot_general(a, b_t, (((1,), (1,)), ((), ())), preferred_element_type=_F32)


BM_SOURCES = ((0, GM_WIDTH, "raw"), (1, GM_WIDTH, "ln"), (2, GM_WIDTH, "silu"), (6, MLA_WIDTH, "silu"),
              (9, D_MODEL, "sigmoid"), (10, D_MODEL, "sigmoid"), (3, LATENT_W, "latent"))
PROJ_SLOT_ROWS = (1024, 1024, 768)


def _proj_plan(step):
    lo, hi = step * PROJ_TN, (step + 1) * PROJ_TN
    plan, c0 = [], 0
    for k, width, kind in BM_SOURCES:
        a, b = max(c0, lo), min(c0 + width, hi)
        if a < b:
            assert kind != "ln" or b - a == width, "layernorm needs its whole segment in one step"
            plan.append((a - lo, b - lo, kind, IN_CUTS[k] + a - c0))
        c0 += width
    assert len(plan) == len(PROJ_SLOT_ROWS)
    assert all(c1 - c0 <= rows for (c0, c1, _, _), rows in zip(plan, PROJ_SLOT_ROWS))
    return plan


def _proj_kernel(x_ref, g_ref, lng_ref, lnb_ref, w0_ref, w1_ref, w2_ref, o_ref, h_scr):
    j = pl.program_id(1)

    @pl.when(j == 0)
    def _():
        h_scr[...] = _rmsnorm_bf16(x_ref[...], g_ref[...])

    def piece(c0, c1, kind, w_rows):
        r = _dot_nt(h_scr[...], w_rows)
        if kind == "ln":
            mu = jnp.mean(r, axis=-1, keepdims=True)
            rc = r - mu
            var = jnp.mean(rc * rc, axis=-1, keepdims=True)
            r = rc * lax.rsqrt(var + EPS) * lng_ref[...] + lnb_ref[...]
        elif kind == "silu":
            r = _silu(r)
        elif kind == "sigmoid":
            r = _sigmoid(r)
        elif kind == "latent":
            t = r[:, c1 - c0 - 128:]
            lane = lax.broadcasted_iota(jnp.int32, t.shape, 1)
            half = MLA_ROPE // 2
            t = jnp.where(lane < MLA_ROPE, t,
                          jnp.where(lane < MLA_ROPE + half, -pltpu.roll(t, half, 1),
                                    pltpu.roll(t, MLA_ROPE + half, 1)))
            r = jnp.concatenate([r[:, :c1 - c0 - 128], t], axis=1)
        o_ref[:, c0:c1] = r.astype(o_ref.dtype)

    for step in range(N_BM // PROJ_TN):
        @pl.when(j == step)
        def _(step=step):
            for (c0, c1, kind, _), w_ref in zip(_proj_plan(step), (w0_ref, w1_ref, w2_ref)):
                piece(c0, c1, kind, w_ref[0, 0:c1 - c0, :])


def _proj(x2d, g, lng, lnb, w_in_t, layer):
    vec = pl.BlockSpec((1, D_MODEL), lambda i, j: (0, 0))
    n_steps = N_BM // PROJ_TN
    plans = [_proj_plan(step) for step in range(n_steps)]

    def w_rows(slot):
        starts = [plans[step][slot][3] for step in range(n_steps)]

        def index_map(i, j):
            row = starts[-1]
            for step in range(n_steps - 2, -1, -1):
                row = jnp.where(j == step, starts[step], row)
            return (layer, row, 0)

        return pl.BlockSpec((pl.Element(1), pl.Element(PROJ_SLOT_ROWS[slot]), pl.Element(D_MODEL)),
                            index_map)

    return pl.pallas_call(
        _proj_kernel,
        grid=(TOKENS // PROJ_TM, n_steps),
        in_specs=[
            pl.BlockSpec((PROJ_TM, D_MODEL), lambda i, j: (i, 0)),
            vec, vec, vec,
            w_rows(0), w_rows(1), w_rows(2),
        ],
        out_specs=pl.BlockSpec((PROJ_TM, PROJ_TN), lambda i, j: (i, j)),
        out_shape=jax.ShapeDtypeStruct((TOKENS, N_BM), _BF16),
        scratch_shapes=[pltpu.VMEM((PROJ_TM, D_MODEL), _BF16)],
        compiler_params=_params("arbitrary", "arbitrary"),
        name="proj_bm",
    )(x2d, g, lng, lnb, w_in_t, w_in_t, w_in_t)


def _gate_k_range(c):
    first_block = (c * GATE_TN) // LRU_BW
    last_block = ((c + 1) * GATE_TN - 1) // LRU_BW
    k0 = (first_block * LRU_BW) // GATE_TN * GATE_TN
    k1 = -(-((last_block + 1) * LRU_BW) // GATE_TN) * GATE_TN
    return k0, k1


GATE_K_MAX = max(k1 - k0 for k0, k1 in map(_gate_k_range, range(LRU_WIDTH // GATE_TN)))


def _gate_weights_kernel(wa_ref, wx_ref, o_ref, stage):
    for c in range(LRU_WIDTH // GATE_TN):
        k0, _ = _gate_k_range(c)
        stage[...] = jnp.zeros(stage.shape, _F32)
        for h in range(LRU_BLOCKS):
            lo = max(LRU_BW * h, GATE_TN * c)
            hi = min(LRU_BW * (h + 1), GATE_TN * (c + 1))
            if lo >= hi:
                continue
            r0 = LRU_BW * h - k0
            src = slice(lo - LRU_BW * h, hi - LRU_BW * h)
            dst = slice(lo - GATE_TN * c, hi - GATE_TN * c)
            stage[r0:r0 + LRU_BW, dst] = wa_ref[h][:, src]
            stage[r0:r0 + LRU_BW, GATE_TN + dst.start:GATE_TN + dst.stop] = wx_ref[h][:, src]
        o_ref[c] = stage[...].astype(o_ref.dtype)


def _gate_weights(w_a, w_x):
    n_tiles = LRU_WIDTH // GATE_TN
    blocks = pl.BlockSpec((None, LRU_BLOCKS, LRU_BW, LRU_BW), lambda l: (l, 0, 0, 0))
    return pl.pallas_call(
        _gate_weights_kernel,
        grid=(DEPTH,),
        in_specs=[blocks, blocks],
        out_specs=pl.BlockSpec((None, n_tiles, GATE_K_MAX, 2 * GATE_TN), lambda l: (l, 0, 0, 0)),
        out_shape=jax.ShapeDtypeStruct((DEPTH, n_tiles, GATE_K_MAX, 2 * GATE_TN), _BF16),
        scratch_shapes=[pltpu.VMEM((GATE_K_MAX, 2 * GATE_TN), _F32)],
        compiler_params=_params("arbitrary"),
        name="gate_weights",
    )(w_a, w_x)


def _lru_kernel(x_ref, pg_ref, wxc_ref, wz_ref, wg_ref, cw_ref, cb_ref, wax_ref, ba_ref, bx_ref,
                lam_ref, wp_ref, o_ref, xext, a_scr, b_scr, h_scr, sz_scr, sg_scr):
    rows = LRU_TS * BATCH
    halo = (CONV_W - 1) * BATCH

    @pl.when(pl.program_id(0) == 0)
    def _():
        xext[0:halo, :] = jnp.zeros((halo, LRU_WIDTH), _F32)
        h_scr[...] = jnp.zeros_like(h_scr)

    xt = jnp.swapaxes(x_ref[...], 0, 1).reshape(rows, D_MODEL)
    hn = _rmsnorm_bf16(xt, pg_ref[...])

    xext[halo:halo + rows, :] = _dot_nt(hn, wxc_ref[0])
    xc = cb_ref[...] + cw_ref[CONV_W - 1:CONV_W, :] * xext[halo:halo + rows, :]
    for k in range(CONV_W - 1):
        xc = xc + cw_ref[k:k + 1, :] * xext[k * BATCH:k * BATCH + rows, :]
    xext[0:halo, :] = xext[rows:rows + halo, :]

    xcb = xc.astype(_BF16)
    lam = lam_ref[...]
    softplus_neg_lam = jnp.maximum(-lam, 0.0) + jnp.log(1.0 + jnp.exp(-jnp.abs(lam)))
    for c in range(LRU_WIDTH // GATE_TN):
        cs = slice(c * GATE_TN, (c + 1) * GATE_TN)
        k0, k1 = _gate_k_range(c)
        ri = jnp.dot(xcb[:, k0:k1], wax_ref[c, 0:k1 - k0, :], preferred_element_type=_F32)
        half_log = (-0.5 * LRU_C) * softplus_neg_lam[:, cs]
        a = jnp.exp(half_log * jnp.tanh(0.5 * (ri[:, 0:GATE_TN] + ba_ref[:, cs])) + half_log)
        i = _sigmoid(ri[:, GATE_TN:2 * GATE_TN] + bx_ref[:, cs])
        gap = 1.0 - a * a
        mult = jnp.where(gap > 0.0, gap * lax.rsqrt(gap), 0.0)
        a_scr[:, cs] = a
        b_scr[:, cs] = mult * (i * xc[:, cs])

    sz_scr[...] = _silu(_dot_nt(hn, wz_ref[0]))
    sg_scr[...] = _sigmoid(_dot_nt(hn, wg_ref[0]))

    h = h_scr[...]
    for s in range(LRU_TS):
        rs = slice(s * BATCH, (s + 1) * BATCH)
        h = a_scr[rs, :] * h + b_scr[rs, :]
        b_scr[rs, :] = h
    h_scr[...] = h

    y = (b_scr[...] * sz_scr[...]).astype(_BF16)
    cc = sg_scr[...] * jnp.dot(y, wp_ref[...], preferred_element_type=_F32)
    o_ref[...] = jnp.swapaxes(cc.reshape(LRU_TS, BATCH, D_MODEL), 0, 1).astype(o_ref.dtype)


def _lru_mixer(x3d, pre_g, w_in_t, layer, conv_w, conv_b, wax, ba, bx, lam, wp):
    rows = LRU_TS * BATCH
    halo = (CONV_W - 1) * BATCH

    def resident(a):
        return pl.BlockSpec(a.shape, lambda t: (0,) * a.ndim, pipeline_mode=pl.Buffered(1))

    def w_rows(k):
        return pl.BlockSpec((pl.Element(1), pl.Element(IN_SIZES[k]), pl.Element(D_MODEL)),
                            lambda t: (layer, IN_CUTS[k], 0), pipeline_mode=pl.Buffered(1))

    return pl.pallas_call(
        _lru_kernel,
        grid=(SEQ // LRU_TS,),
        in_specs=[pl.BlockSpec((BATCH, LRU_TS, D_MODEL), lambda t: (0, t, 0)), resident(pre_g),
                  w_rows(IN_XC), w_rows(IN_ZC), w_rows(IN_GC)]
        + [resident(a) for a in (conv_w, conv_b, wax, ba, bx, lam, wp)],
        out_specs=pl.BlockSpec((BATCH, LRU_TS, D_MODEL), lambda t: (0, t, 0)),
        out_shape=jax.ShapeDtypeStruct((BATCH, SEQ, D_MODEL), _BF16),
        scratch_shapes=[
            pltpu.VMEM((rows + halo, LRU_WIDTH), _F32),
            pltpu.VMEM((rows, LRU_WIDTH), _F32),
            pltpu.VMEM((rows, LRU_WIDTH), _F32),
            pltpu.VMEM((BATCH, LRU_WIDTH), _F32),
            pltpu.VMEM((rows, LRU_WIDTH), _F32),
            pltpu.VMEM((rows, D_MODEL), _F32),
        ],
        compiler_params=_params("arbitrary"),
        name="lru_mixer",
    )(x3d, pre_g, w_in_t, w_in_t, w_in_t, conv_w, conv_b, wax, ba, bx, lam, wp)


def _mla_prep_kernel(c_ref, cost_ref, sint_ref, csk_ref, qg_ref, kvg_ref, wqt_ref, wkn_ref, wvt_ref,
                     qt_ref, k_ref, vt_ref):
    qscale = math.log2(math.e) / math.sqrt(MLA_QK_DIM)
    c = c_ref[...].astype(_F32)
    cq = c[:, 0:MLA_Q_RANK]
    ckv = c[:, MLA_Q_RANK:MLA_Q_RANK + MLA_KV_RANK]
    krk = c[:, MLA_Q_RANK + MLA_KV_RANK:LATENT_W]

    hq = (cq * lax.rsqrt(jnp.mean(cq * cq, axis=-1, keepdims=True) + EPS)
          * qg_ref[...]).astype(_BF16)
    qt = lax.dot_general(wqt_ref[...], hq, (((1,), (1,)), ((), ())), preferred_element_type=_F32)
    nope_w = MLA_HEADS * MLA_NOPE
    rope_w = MLA_HEADS * MLA_ROPE
    cost = cost_ref[...]
    sint = sint_ref[...]
    for p in range(rope_w // 128):
        qr = qt[nope_w + 128 * p:nope_w + 128 * (p + 1), :]
        qrr = qt[nope_w + rope_w + 128 * p:nope_w + rope_w + 128 * (p + 1), :]
        qt_ref[nope_w + 128 * p:nope_w + 128 * (p + 1), :] = (
            (qr * cost + qrr * sint) * qscale).astype(_BF16)
    qt_ref[0:nope_w, :] = (qt[0:nope_w, :] * qscale).astype(_BF16)

    hkv = (ckv * lax.rsqrt(jnp.mean(ckv * ckv, axis=-1, keepdims=True) + EPS)
           * kvg_ref[...]).astype(_BF16)
    kn = jnp.dot(hkv, wkn_ref[...], preferred_element_type=_F32)
    vt = lax.dot_general(wvt_ref[...], hkv, (((1,), (1,)), ((), ())),
                         preferred_element_type=_F32).astype(_BF16)
    for h in range(MLA_HEADS):
        vt_ref[ATT_VROWS * h:ATT_VROWS * h + MLA_VDIM, :] = vt[MLA_VDIM * h:MLA_VDIM * (h + 1), :]
        vt_ref[ATT_VROWS * h + MLA_VDIM:ATT_VROWS * (h + 1), :] = jnp.ones(
            (ATT_VROWS - MLA_VDIM, vt.shape[1]), _BF16)

    t = krk * csk_ref[...]
    kf2 = t + pltpu.roll(t, MLA_ROPE, 1)
    lane = lax.broadcasted_iota(jnp.int32, kf2.shape, 1)
    k_ref[:, 0:nope_w] = kn.astype(_BF16)
    k_ref[:, nope_w:nope_w + 128] = jnp.where(lane < MLA_ROPE, kf2, 0.0).astype(_BF16)
    k_ref[:, nope_w + 128:nope_w + 256] = jnp.where(lane >= MLA_ROPE, kf2, 0.0).astype(_BF16)


def _mla_prep(p_bm, cos128t, sin128t, csk, qg, kvg, wqt, wkn, wvt):
    tiles_per_seq = SEQ // PREP_TM
    const = lambda i: (0, 0)
    pos = lambda i: (i % tiles_per_seq, 0)
    pos_t = lambda i: (0, i % tiles_per_seq)
    return pl.pallas_call(
        _mla_prep_kernel,
        grid=(TOKENS // PREP_TM,),
        in_specs=[
            pl.BlockSpec((PREP_TM, LATENT_W), lambda i: (i, COL_LATENT)),
            pl.BlockSpec((128, PREP_TM), pos_t),
            pl.BlockSpec((128, PREP_TM), pos_t),
            pl.BlockSpec((PREP_TM, 128), pos),
            pl.BlockSpec((1, MLA_Q_RANK), const),
            pl.BlockSpec((1, MLA_KV_RANK), const),
            pl.BlockSpec(wqt.shape, const),
            pl.BlockSpec(wkn.shape, const),
            pl.BlockSpec(wvt.shape, const),
        ],
        out_specs=[
            pl.BlockSpec((ATT_QROWS, PREP_TM), lambda i: (0, i)),
            pl.BlockSpec((PREP_TM, ATT_KCOLS), lambda i: (i, 0)),
            pl.BlockSpec((MLA_HEADS * ATT_VROWS, PREP_TM), lambda i: (0, i)),
        ],
        out_shape=[
            jax.ShapeDtypeStruct((ATT_QROWS, TOKENS), _BF16),
            jax.ShapeDtypeStruct((TOKENS, ATT_KCOLS), _BF16),
            jax.ShapeDtypeStruct((MLA_HEADS * ATT_VROWS, TOKENS), _BF16),
        ],
        compiler_params=_params("arbitrary"),
        name="mla_prep",
    )(p_bm, cos128t, sin128t, csk, qg, kvg, wqt, wkn, wvt)


def _attn_kernel(qt_ref, k_ref, vt_ref, qt_next_ref, k_next_ref, szb_ref, sgb_ref, wp_ref,
                 x_ref, u_ref, vln_ref, sza_ref, sga_ref, cc_ref, ws_ref, bs_ref,
                 wpa_ref, wo_ref, pg_ref, o_ref,
                 st_scr, mt_scr, m_scr, l_scr, acc_scr, y_scr, ya_scr):
    qi = pl.program_id(1)
    key_chunk = lax.broadcasted_iota(jnp.int32, (ATT_TK, ATT_TQ), 0) // CHUNK
    query_chunk = lax.broadcasted_iota(jnp.int32, (ATT_TK, ATT_TQ), 1) // CHUNK
    diag_mask = key_chunk <= query_chunk

    m_scr[...] = jnp.full(m_scr.shape, -1e30, _F32)
    l_scr[...] = jnp.zeros(l_scr.shape, _F32)
    acc_scr[...] = jnp.zeros(acc_scr.shape, _F32)

    def store_scores(k_tile, q_t, h):
        st = jnp.dot(k_tile, q_t, preferred_element_type=_F32)
        st_scr[h] = st
        mt_scr[h] = jnp.max(st, axis=0, keepdims=True)

    nope_w = MLA_HEADS * MLA_NOPE

    def head_q(q_ref_, h):
        pair = nope_w + 128 * (h // 2)
        return jnp.concatenate([q_ref_[MLA_NOPE * h:MLA_NOPE * (h + 1), :],
                                q_ref_[pair:pair + 128, :]], axis=0)

    def head_k(k_rows, h):
        rope = nope_w + 128 * (h % 2)
        return jnp.concatenate([k_rows[:, MLA_NOPE * h:MLA_NOPE * (h + 1)],
                                k_rows[:, rope:rope + 128]], axis=1)

    def scores(j, h):
        off = pl.multiple_of(j * ATT_TK, ATT_TK)
        store_scores(head_k(k_ref.at[pl.ds(off, ATT_TK), :], h), head_q(qt_ref, h), h)

    def accumulate(j, h, masked):
        off = pl.multiple_of(j * ATT_TK, ATT_TK)
        st = st_scr[h]
        if masked:
            st = jnp.where(diag_mask, st, -1e30)
            mt = jnp.max(st, axis=0, keepdims=True)
        else:
            mt = mt_scr[h]
        m_prev = m_scr[h]
        m_new = jnp.maximum(m_prev, mt)
        alpha = jnp.exp2(m_prev - m_new)
        p = jnp.exp2(st - m_new)
        vt = vt_ref[ATT_VROWS * h:ATT_VROWS * (h + 1), pl.ds(off, ATT_TK)]
        pv = jnp.dot(vt, p.astype(_BF16), preferred_element_type=_F32)
        acc_scr[h] = alpha * acc_scr[h] + pv[0:MLA_VDIM, :]
        l_scr[h] = alpha * l_scr[h] + pv[MLA_VDIM:MLA_VDIM + 1, :]
        m_scr[h] = m_new

    @pl.when((pl.program_id(0) == 0) & (qi == 0))
    def _():
        for h in range(MLA_HEADS):
            scores(0, h)

    def body(j, carry):
        for h in range(MLA_HEADS):
            accumulate(j, h, False)
            scores(j + 1, h)
        return carry

    lax.fori_loop(0, qi, body, 0)
    for h in range(MLA_HEADS):
        accumulate(qi, h, True)
        store_scores(head_k(k_next_ref, h), head_q(qt_next_ref, h), h)

    for h in range(MLA_HEADS):
        hs = slice(MLA_VDIM * h, MLA_VDIM * (h + 1))
        o = (acc_scr[h] * (1.0 / l_scr[h])).T
        y_scr[:, hs] = (o * szb_ref[:, hs].astype(_F32)).astype(_BF16)

    cb = sgb_ref[...].astype(_F32) * jnp.dot(y_scr[...], wp_ref[...], preferred_element_type=_F32)

    idx_r = lax.broadcasted_iota(jnp.int32, (GM_BLOCK, GM_BLOCK), 0) // CHUNK
    idx_c = lax.broadcasted_iota(jnp.int32, (GM_BLOCK, GM_BLOCK), 1) // CHUNK
    causal = idx_c <= idx_r

    for g in range(GM_GROUPS):
        ws = jnp.where(causal, ws_ref[g], 0.0).astype(_BF16)
        cs = slice(GM_GW * g, GM_GW * (g + 1))
        for r in range(ATT_TQ // GM_BLOCK):
            rs = slice(GM_BLOCK * r, GM_BLOCK * (r + 1))
            sv = jnp.dot(ws, vln_ref[rs, cs], preferred_element_type=_F32) + bs_ref[g]
            y = u_ref[rs, cs].astype(_F32) * sv * sza_ref[rs, cs].astype(_F32)
            ya_scr[rs, cs] = y.astype(_BF16)

    ca = sga_ref[...].astype(_F32) * jnp.dot(ya_scr[...], wpa_ref[...], preferred_element_type=_F32)
    merged = ca + cb + cc_ref[...].astype(_F32)
    o = jnp.dot(merged.astype(_BF16), wo_ref[...], preferred_element_type=_F32)
    o = o * lax.rsqrt(jnp.mean(o * o, axis=-1, keepdims=True) + EPS) * pg_ref[...]
    o_ref[...] = x_ref[...] + o


def _attention_merge(qt, k, vt, p_bm, x2d, cc, wpb, ws, bs_col, wpa, wo, pg):
    nq = SEQ // ATT_TQ
    rows = lambda b, i: (b * nq + i, 0)
    next_step = lambda b, i: jnp.minimum(b * nq + i + 1, BATCH * nq - 1)
    seg = lambda c: pl.BlockSpec((ATT_TQ, BM_SEG), lambda b, i: (b * nq + i, c))
    tile = pl.BlockSpec((ATT_TQ, D_MODEL), rows)

    def resident(a):
        return pl.BlockSpec(a.shape, lambda b, i: (0,) * a.ndim, pipeline_mode=pl.Buffered(1))

    return pl.pallas_call(
        _attn_kernel,
        grid=(BATCH, nq),
        in_specs=[
            pl.BlockSpec((ATT_QROWS, ATT_TQ), lambda b, i: (0, b * nq + i)),
            pl.BlockSpec((SEQ, ATT_KCOLS), lambda b, i: (b, 0)),
            pl.BlockSpec((MLA_HEADS * ATT_VROWS, SEQ), lambda b, i: (0, b)),
            pl.BlockSpec((ATT_QROWS, ATT_TQ), lambda b, i: (0, next_step(b, i))),
            pl.BlockSpec((ATT_TK, ATT_KCOLS),
                         lambda b, i: ((next_step(b, i) // nq) * (SEQ // ATT_TK), 0)),
            seg(COL_ZB), seg(COL_GB), resident(wpb),
            tile, seg(COL_U), seg(COL_V), seg(COL_ZA), seg(COL_GA), tile,
            resident(ws), resident(bs_col),
            resident(wpa), resident(wo), resident(pg),
        ],
        out_specs=tile,
        out_shape=jax.ShapeDtypeStruct((TOKENS, D_MODEL), _F32),
        scratch_shapes=[
            pltpu.VMEM((MLA_HEADS, ATT_TK, ATT_TQ), _F32),
            pltpu.VMEM((MLA_HEADS, 1, ATT_TQ), _F32),
            pltpu.VMEM((MLA_HEADS, 1, ATT_TQ), _F32),
            pltpu.VMEM((MLA_HEADS, 1, ATT_TQ), _F32),
            pltpu.VMEM((MLA_HEADS, MLA_VDIM, ATT_TQ), _F32),
            pltpu.VMEM((ATT_TQ, MLA_WIDTH), _BF16),
            pltpu.VMEM((ATT_TQ, GM_WIDTH), _BF16),
        ],
        compiler_params=_params("arbitrary", "arbitrary"),
        name="mla_attn_merge",
    )(qt, k, vt, qt, k, p_bm, p_bm, wpb, x2d, p_bm, p_bm, p_bm, p_bm, cc,
      ws, bs_col, wpa, wo, pg)


def _layer_weights(l, w_in_t, mla_w_uq, mla_w_ukv, w_proj_a, w_proj_b, w_proj_c, w_out):
    half = MLA_ROPE // 2
    wq = mla_w_uq[l].reshape(MLA_Q_RANK, MLA_HEADS, MLA_QK_DIM)
    wq_nope = wq[:, :, :MLA_NOPE].reshape(MLA_Q_RANK, -1)
    wq_rope = wq[:, :, MLA_NOPE:].reshape(MLA_Q_RANK, -1)
    wq_rope_rot = jnp.concatenate([-wq[:, :, MLA_NOPE + half:], wq[:, :, MLA_NOPE:MLA_NOPE + half]],
                                  axis=2).reshape(MLA_Q_RANK, -1)
    wq_all = jnp.concatenate([wq_nope, wq_rope, wq_rope_rot], axis=1).T.astype(_BF16)

    wkv = mla_w_ukv[l].reshape(MLA_KV_RANK, MLA_HEADS, MLA_NOPE + MLA_VDIM)
    wkn = wkv[:, :, :MLA_NOPE].reshape(MLA_KV_RANK, -1).astype(_BF16)
    wvt = wkv[:, :, MLA_NOPE:].reshape(MLA_KV_RANK, -1).T.astype(_BF16)

    return dict(wq=wq_all, wkn=wkn, wvt=wvt,
                wpa=w_proj_a[l].astype(_BF16), wpb=w_proj_b[l].astype(_BF16),
                wpc=w_proj_c[l].astype(_BF16), wo=w_out[l].astype(_BF16))


def _rope_tables():
    pos = jnp.arange(SEQ, dtype=_F32)
    inv_freq = ROPE_THETA ** (-jnp.arange(0, MLA_ROPE, 2, dtype=_F32) / MLA_ROPE)
    ang = pos[:, None] * inv_freq[None, :]
    cos = jnp.cos(ang)
    sin = jnp.sin(ang)
    cos128t = jnp.tile(cos, (1, 4)).T
    sin128t = jnp.tile(sin, (1, 4)).T
    csk = jnp.concatenate([cos, cos, sin, sin], axis=1)
    return cos128t, sin128t, csk


def kernel(x, pre_norm_g, w_in, gm_ln_g, gm_ln_b, gm_ws, gm_bs, mla_q_norm_g, mla_w_uq,
           mla_kv_norm_g, mla_w_ukv, lru_conv_w, lru_conv_b, lru_w_a, lru_b_a, lru_w_x,
           lru_b_x, lru_lambda, w_proj_a, w_proj_b, w_proj_c, w_out, post_norm_g):
    cos128t, sin128t, csk = _rope_tables()
    wax = _gate_weights(lru_w_a, lru_w_x)
    w_in_t = jnp.swapaxes(w_in, 1, 2).astype(_BF16)
    x2d = x.reshape(TOKENS, D_MODEL)
    row = lambda a: a.reshape(1, -1)
    for l in range(DEPTH):
        w = _layer_weights(l, w_in_t, mla_w_uq, mla_w_ukv, w_proj_a, w_proj_b, w_proj_c, w_out)
        g_pre = row(pre_norm_g[l])
        p_bm = _proj(x2d, g_pre, row(gm_ln_g[l]), row(gm_ln_b[l]), w_in_t, l)

        cc = _lru_mixer(x2d.reshape(BATCH, SEQ, D_MODEL), g_pre, w_in_t, l,
                        lru_conv_w[l], row(lru_conv_b[l]), wax[l], row(lru_b_a[l]),
                        row(lru_b_x[l]), row(lru_lambda[l]), w["wpc"]).reshape(TOKENS, D_MODEL)

        qt, k, vt = _mla_prep(p_bm, cos128t, sin128t, csk, row(mla_q_norm_g[l]),
                              row(mla_kv_norm_g[l]), w["wq"], w["wkn"], w["wvt"])
        x2d = _attention_merge(qt, k, vt, p_bm, x2d, cc, w["wpb"], gm_ws[l],
                               gm_bs[l][:, :, None], w["wpa"], w["wo"], row(post_norm_g[l]))
    return x2d.reshape(BATCH, SEQ, D_MODEL)
```

```python
import math

import jax
import jax.numpy as jnp
from jax import lax
from jax.experimental import pallas as pl
from jax.experimental.pallas import tpu as pltpu

D_MODEL = 1024
BATCH = 8
SEQ = 2048
DEPTH = 2
TOKENS = BATCH * SEQ
CHUNK = 64
EPS = 1e-6

GM_WIDTH = 1024
GM_GROUPS = 4
GM_BLOCK = 128
GM_GW = GM_WIDTH // GM_GROUPS

MLA_HEADS = 8
MLA_NOPE = 128
MLA_ROPE = 64
MLA_VDIM = 128
MLA_QK_DIM = MLA_NOPE + MLA_ROPE
MLA_Q_RANK = 384
MLA_KV_RANK = 256
MLA_WIDTH = MLA_HEADS * MLA_VDIM
ROPE_THETA = 10000.0
ROPE_GROUP = 2 * MLA_ROPE

LRU_WIDTH = 1280
LRU_BLOCKS = 16
LRU_BW = LRU_WIDTH // LRU_BLOCKS
LRU_C = 8.0
CONV_W = 4

IN_SIZES = (GM_WIDTH, GM_WIDTH, GM_WIDTH, MLA_Q_RANK, MLA_KV_RANK, MLA_ROPE, MLA_WIDTH,
            LRU_WIDTH, LRU_WIDTH, D_MODEL, D_MODEL, D_MODEL)
IN_CUTS = tuple(sum(IN_SIZES[:k]) for k in range(len(IN_SIZES) + 1))
(IN_XC, IN_ZC, IN_GC) = (7, 8, 11)

BM_SEG = 1024
(COL_U, COL_V, COL_ZA, COL_ZB, COL_GA, COL_GB) = range(6)
LATENT_W = MLA_Q_RANK + MLA_KV_RANK + 2 * MLA_ROPE
N_BM = 6 * BM_SEG + LATENT_W
COL_LATENT = (6 * BM_SEG) // LATENT_W

VMEM_LIMIT_BYTES = 56 * 1024 * 1024
LANES = 128

PROJ_TM = 1024
PROJ_TN = 2304
LRU_TS = 64
GATE_TN = 256
PREP_TM = 1024
ATT_TQ = 256
ATT_TK = 256
ATT_D = 256
ATT_QROWS = MLA_HEADS * MLA_NOPE + (MLA_HEADS // 2) * ROPE_GROUP
ATT_KCOLS = MLA_HEADS * MLA_NOPE + 2 * ROPE_GROUP
ATT_VROWS = MLA_VDIM + 16

_F32 = jnp.float32
_BF16 = jnp.bfloat16


def _params(*sem):
    return pltpu.CompilerParams(dimension_semantics=sem, vmem_limit_bytes=VMEM_LIMIT_BYTES)


def _sigmoid(x):
    return 0.5 * jnp.tanh(0.5 * x) + 0.5


def _silu(x):
    h = 0.5 * x
    return h * jnp.tanh(h) + h


def _rmsnorm_bf16(x, g):
    ms = jnp.mean(x * x, axis=-1, keepdims=True)
    return (x * lax.rsqrt(ms + EPS) * g).astype(_BF16)


def _dot_nt(a, b_t):
    return lax.dot_general(a, b_t, (((1,), (1,)), ((), ())), preferred_element_type=_F32)


BM_SOURCES = ((0, GM_WIDTH, "raw"), (1, GM_WIDTH, "ln"), (2, GM_WIDTH, "silu"), (6, MLA_WIDTH, "silu"),
              (9, D_MODEL, "sigmoid"), (10, D_MODEL, "sigmoid"), (3, LATENT_W, "latent"))
PROJ_SLOT_ROWS = (1024, 1024, 768)


def _proj_plan(step):
    lo, hi = step * PROJ_TN, (step + 1) * PROJ_TN
    plan, c0 = [], 0
    for k, width, kind in BM_SOURCES:
        a, b = max(c0, lo), min(c0 + width, hi)
        if a < b:
            assert kind != "ln" or b - a == width, "layernorm needs its whole segment in one step"
            plan.append((a - lo, b - lo, kind, IN_CUTS[k] + a - c0))
        c0 += width
    assert len(plan) == len(PROJ_SLOT_ROWS)
    assert all(c1 - c0 <= rows for (c0, c1, _, _), rows in zip(plan, PROJ_SLOT_ROWS))
    return plan


def _proj_kernel(x_ref, g_ref, lng_ref, lnb_ref, w0_ref, w1_ref, w2_ref, o_ref, h_scr):
    j = pl.program_id(1)

    @pl.when(j == 0)
    def _():
        h_scr[...] = _rmsnorm_bf16(x_ref[...], g_ref[...])

    def piece(c0, c1, kind, w_rows):
        r = _dot_nt(h_scr[...], w_rows)
        if kind == "ln":
            mu = jnp.mean(r, axis=-1, keepdims=True)
            rc = r - mu
            var = jnp.mean(rc * rc, axis=-1, keepdims=True)
            r = rc * lax.rsqrt(var + EPS) * lng_ref[...] + lnb_ref[...]
        elif kind == "silu":
            r = _silu(r)
        elif kind == "sigmoid":
            r = _sigmoid(r)
        elif kind == "latent":
            t = r[:, c1 - c0 - ROPE_GROUP:]
            lane = lax.broadcasted_iota(jnp.int32, t.shape, 1)
            half = MLA_ROPE // 2
            t = jnp.where(lane < MLA_ROPE, t,
                          jnp.where(lane < MLA_ROPE + half, -pltpu.roll(t, half, 1),
                                    pltpu.roll(t, MLA_ROPE + half, 1)))
            r = jnp.concatenate([r[:, :c1 - c0 - ROPE_GROUP], t], axis=1)
        o_ref[:, c0:c1] = r.astype(o_ref.dtype)

    for step in range(N_BM // PROJ_TN):
        @pl.when(j == step)
        def _(step=step):
            for (c0, c1, kind, _), w_ref in zip(_proj_plan(step), (w0_ref, w1_ref, w2_ref)):
                piece(c0, c1, kind, w_ref[0, 0:c1 - c0, :])


def _proj(x2d, g, lng, lnb, w_in_t, layer):
    vec = pl.BlockSpec((1, D_MODEL), lambda i, j: (0, 0))
    n_steps = N_BM // PROJ_TN
    plans = [_proj_plan(step) for step in range(n_steps)]

    def w_rows(slot):
        starts = [plans[step][slot][3] for step in range(n_steps)]

        def index_map(i, j):
            row = starts[-1]
            for step in range(n_steps - 2, -1, -1):
                row = jnp.where(j == step, starts[step], row)
            return (layer, row, 0)

        return pl.BlockSpec((pl.Element(1), pl.Element(PROJ_SLOT_ROWS[slot]), pl.Element(D_MODEL)),
                            index_map)

    return pl.pallas_call(
        _proj_kernel,
        grid=(TOKENS // PROJ_TM, n_steps),
        in_specs=[
            pl.BlockSpec((PROJ_TM, D_MODEL), lambda i, j: (i, 0)),
            vec, vec, vec,
            w_rows(0), w_rows(1), w_rows(2),
        ],
        out_specs=pl.BlockSpec((PROJ_TM, PROJ_TN), lambda i, j: (i, j)),
        out_shape=jax.ShapeDtypeStruct((TOKENS, N_BM), _BF16),
        scratch_shapes=[pltpu.VMEM((PROJ_TM, D_MODEL), _BF16)],
        compiler_params=_params("arbitrary", "arbitrary"),
        name="proj_bm",
    )(x2d, g, lng, lnb, w_in_t, w_in_t, w_in_t)


def _gate_k_range(c):
    first_row = (c * GATE_TN) // LRU_BW * LRU_BW
    last_row = (((c + 1) * GATE_TN - 1) // LRU_BW + 1) * LRU_BW
    k0 = first_row // LANES * LANES
    width = -(-(last_row - k0) // GATE_TN) * GATE_TN
    k0 = min(k0, LRU_WIDTH - width)
    return k0, k0 + width


GATE_K_MAX = max(k1 - k0 for k0, k1 in map(_gate_k_range, range(LRU_WIDTH // GATE_TN)))


def _gate_weights_kernel(wa_ref, wx_ref, o_ref, stage):
    for c in range(LRU_WIDTH // GATE_TN):
        k0, _ = _gate_k_range(c)
        stage[...] = jnp.zeros(stage.shape, _F32)
        for h in range(LRU_BLOCKS):
            lo = max(LRU_BW * h, GATE_TN * c)
            hi = min(LRU_BW * (h + 1), GATE_TN * (c + 1))
            if lo >= hi:
                continue
            r0 = LRU_BW * h - k0
            src = slice(lo - LRU_BW * h, hi - LRU_BW * h)
            dst = slice(lo - GATE_TN * c, hi - GATE_TN * c)
            stage[r0:r0 + LRU_BW, dst] = wa_ref[h][:, src]
            stage[r0:r0 + LRU_BW, GATE_TN + dst.start:GATE_TN + dst.stop] = wx_ref[h][:, src]
        o_ref[c] = stage[...].astype(o_ref.dtype)


def _gate_weights(w_a, w_x):
    n_tiles = LRU_WIDTH // GATE_TN
    blocks = pl.BlockSpec((None, LRU_BLOCKS, LRU_BW, LRU_BW), lambda l: (l, 0, 0, 0))
    return pl.pallas_call(
        _gate_weights_kernel,
        grid=(DEPTH,),
        in_specs=[blocks, blocks],
        out_specs=pl.BlockSpec((None, n_tiles, GATE_K_MAX, 2 * GATE_TN), lambda l: (l, 0, 0, 0)),
        out_shape=jax.ShapeDtypeStruct((DEPTH, n_tiles, GATE_K_MAX, 2 * GATE_TN), _BF16),
        scratch_shapes=[pltpu.VMEM((GATE_K_MAX, 2 * GATE_TN), _F32)],
        compiler_params=_params("arbitrary"),
        name="gate_weights",
    )(w_a, w_x)


def _lru_kernel(x_ref, pg_ref, wxc_ref, wz_ref, wg_ref, cw_ref, cb_ref, wax_ref, ba_ref, bx_ref,
                lam_ref, wp_ref, o_ref, xext, a_scr, b_scr, h_scr, sz_scr, sg_scr):
    rows = LRU_TS * BATCH
    halo = (CONV_W - 1) * BATCH

    @pl.when(pl.program_id(0) == 0)
    def _():
        xext[0:halo, :] = jnp.zeros((halo, LRU_WIDTH), _F32)
        h_scr[...] = jnp.zeros_like(h_scr)

    xt = jnp.swapaxes(x_ref[...], 0, 1).reshape(rows, D_MODEL)
    hn = _rmsnorm_bf16(xt, pg_ref[...])

    xext[halo:halo + rows, :] = _dot_nt(hn, wxc_ref[0])
    xc = cb_ref[...] + cw_ref[CONV_W - 1:CONV_W, :] * xext[halo:halo + rows, :]
    for k in range(CONV_W - 1):
        xc = xc + cw_ref[k:k + 1, :] * xext[k * BATCH:k * BATCH + rows, :]
    xext[0:halo, :] = xext[rows:rows + halo, :]

    xcb = xc.astype(_BF16)
    lam = lam_ref[...]
    softplus_neg_lam = jnp.maximum(-lam, 0.0) + jnp.log(1.0 + jnp.exp(-jnp.abs(lam)))
    for c in range(LRU_WIDTH // GATE_TN):
        cs = slice(c * GATE_TN, (c + 1) * GATE_TN)
        k0, k1 = _gate_k_range(c)
        ri = jnp.dot(xcb[:, k0:k1], wax_ref[c, 0:k1 - k0, :], preferred_element_type=_F32)
        half_log = (-0.5 * LRU_C) * softplus_neg_lam[:, cs]
        a = jnp.exp(half_log * jnp.tanh(0.5 * (ri[:, 0:GATE_TN] + ba_ref[:, cs])) + half_log)
        i = _sigmoid(ri[:, GATE_TN:2 * GATE_TN] + bx_ref[:, cs])
        gap = 1.0 - a * a
        mult = jnp.where(gap > 0.0, gap * lax.rsqrt(gap), 0.0)
        a_scr[:, cs] = a
        b_scr[:, cs] = mult * (i * xc[:, cs])

    sz_scr[...] = _silu(_dot_nt(hn, wz_ref[0]))
    sg_scr[...] = _sigmoid(_dot_nt(hn, wg_ref[0]))

    h = h_scr[...]
    for s in range(LRU_TS):
        rs = slice(s * BATCH, (s + 1) * BATCH)
        h = a_scr[rs, :] * h + b_scr[rs, :]
        b_scr[rs, :] = h
    h_scr[...] = h

    y = (b_scr[...] * sz_scr[...]).astype(_BF16)
    cc = sg_scr[...] * jnp.dot(y, wp_ref[...], preferred_element_type=_F32)
    o_ref[...] = jnp.swapaxes(cc.reshape(LRU_TS, BATCH, D_MODEL), 0, 1).astype(o_ref.dtype)


def _lru_mixer(x3d, pre_g, w_in_t, layer, conv_w, conv_b, wax, ba, bx, lam, wp):
    rows = LRU_TS * BATCH
    halo = (CONV_W - 1) * BATCH

    def resident(a):
        return pl.BlockSpec(a.shape, lambda t: (0,) * a.ndim, pipeline_mode=pl.Buffered(1))

    def w_rows(k):
        return pl.BlockSpec((pl.Element(1), pl.Element(IN_SIZES[k]), pl.Element(D_MODEL)),
                            lambda t: (layer, IN_CUTS[k], 0), pipeline_mode=pl.Buffered(1))

    return pl.pallas_call(
        _lru_kernel,
        grid=(SEQ // LRU_TS,),
        in_specs=[pl.BlockSpec((BATCH, LRU_TS, D_MODEL), lambda t: (0, t, 0)), resident(pre_g),
                  w_rows(IN_XC), w_rows(IN_ZC), w_rows(IN_GC)]
        + [resident(a) for a in (conv_w, conv_b, wax, ba, bx, lam, wp)],
        out_specs=pl.BlockSpec((BATCH, LRU_TS, D_MODEL), lambda t: (0, t, 0)),
        out_shape=jax.ShapeDtypeStruct((BATCH, SEQ, D_MODEL), _BF16),
        scratch_shapes=[
            pltpu.VMEM((rows + halo, LRU_WIDTH), _F32),
            pltpu.VMEM((rows, LRU_WIDTH), _F32),
            pltpu.VMEM((rows, LRU_WIDTH), _F32),
            pltpu.VMEM((BATCH, LRU_WIDTH), _F32),
            pltpu.VMEM((rows, LRU_WIDTH), _F32),
            pltpu.VMEM((rows, D_MODEL), _F32),
        ],
        compiler_params=_params("arbitrary"),
        name="lru_mixer",
    )(x3d, pre_g, w_in_t, w_in_t, w_in_t, conv_w, conv_b, wax, ba, bx, lam, wp)


def _mla_prep_kernel(c_ref, cost_ref, sint_ref, csk_ref, qg_ref, kvg_ref, wqt_ref, wkn_ref, wvt_ref,
                     qt_ref, k_ref, vt_ref):
    qscale = math.log2(math.e) / math.sqrt(MLA_QK_DIM)
    c = c_ref[...].astype(_F32)
    cq = c[:, 0:MLA_Q_RANK]
    ckv = c[:, MLA_Q_RANK:MLA_Q_RANK + MLA_KV_RANK]
    krk = c[:, MLA_Q_RANK + MLA_KV_RANK:LATENT_W]

    hq = (cq * lax.rsqrt(jnp.mean(cq * cq, axis=-1, keepdims=True) + EPS)
          * qg_ref[...]).astype(_BF16)
    qt = lax.dot_general(wqt_ref[...], hq, (((1,), (1,)), ((), ())), preferred_element_type=_F32)
    nope_w = MLA_HEADS * MLA_NOPE
    rope_w = MLA_HEADS * MLA_ROPE
    cost = cost_ref[...]
    sint = sint_ref[...]
    for p in range(rope_w // ROPE_GROUP):
        qr = qt[nope_w + ROPE_GROUP * p:nope_w + ROPE_GROUP * (p + 1), :]
        qrr = qt[nope_w + rope_w + ROPE_GROUP * p:nope_w + rope_w + ROPE_GROUP * (p + 1), :]
        qt_ref[nope_w + ROPE_GROUP * p:nope_w + ROPE_GROUP * (p + 1), :] = (
            (qr * cost + qrr * sint) * qscale).astype(_BF16)
    qt_ref[0:nope_w, :] = (qt[0:nope_w, :] * qscale).astype(_BF16)

    hkv = (ckv * lax.rsqrt(jnp.mean(ckv * ckv, axis=-1, keepdims=True) + EPS)
           * kvg_ref[...]).astype(_BF16)
    kn = jnp.dot(hkv, wkn_ref[...], preferred_element_type=_F32)
    vt = lax.dot_general(wvt_ref[...], hkv, (((1,), (1,)), ((), ())),
                         preferred_element_type=_F32).astype(_BF16)
    for h in range(MLA_HEADS):
        vt_ref[ATT_VROWS * h:ATT_VROWS * h + MLA_VDIM, :] = vt[MLA_VDIM * h:MLA_VDIM * (h + 1), :]
        vt_ref[ATT_VROWS * h + MLA_VDIM:ATT_VROWS * (h + 1), :] = jnp.ones(
            (ATT_VROWS - MLA_VDIM, vt.shape[1]), _BF16)

    t = krk * csk_ref[...]
    kf2 = t + pltpu.roll(t, MLA_ROPE, 1)
    lane = lax.broadcasted_iota(jnp.int32, kf2.shape, 1)
    k_ref[:, 0:nope_w] = kn.astype(_BF16)
    k_ref[:, nope_w:nope_w + ROPE_GROUP] = jnp.where(lane < MLA_ROPE, kf2, 0.0).astype(_BF16)
    k_ref[:, nope_w + ROPE_GROUP:nope_w + 2 * ROPE_GROUP] = jnp.where(
        lane >= MLA_ROPE, kf2, 0.0).astype(_BF16)


def _mla_prep(p_bm, cos128t, sin128t, csk, qg, kvg, wqt, wkn, wvt):
    tiles_per_seq = SEQ // PREP_TM
    const = lambda i: (0, 0)
    pos = lambda i: (i % tiles_per_seq, 0)
    pos_t = lambda i: (0, i % tiles_per_seq)
    return pl.pallas_call(
        _mla_prep_kernel,
        grid=(TOKENS // PREP_TM,),
        in_specs=[
            pl.BlockSpec((PREP_TM, LATENT_W), lambda i: (i, COL_LATENT)),
            pl.BlockSpec((ROPE_GROUP, PREP_TM), pos_t),
            pl.BlockSpec((ROPE_GROUP, PREP_TM), pos_t),
            pl.BlockSpec((PREP_TM, ROPE_GROUP), pos),
            pl.BlockSpec((1, MLA_Q_RANK), const),
            pl.BlockSpec((1, MLA_KV_RANK), const),
            pl.BlockSpec(wqt.shape, const),
            pl.BlockSpec(wkn.shape, const),
            pl.BlockSpec(wvt.shape, const),
        ],
        out_specs=[
            pl.BlockSpec((ATT_QROWS, PREP_TM), lambda i: (0, i)),
            pl.BlockSpec((PREP_TM, ATT_KCOLS), lambda i: (i, 0)),
            pl.BlockSpec((MLA_HEADS * ATT_VROWS, PREP_TM), lambda i: (0, i)),
        ],
        out_shape=[
            jax.ShapeDtypeStruct((ATT_QROWS, TOKENS), _BF16),
            jax.ShapeDtypeStruct((TOKENS, ATT_KCOLS), _BF16),
            jax.ShapeDtypeStruct((MLA_HEADS * ATT_VROWS, TOKENS), _BF16),
        ],
        compiler_params=_params("arbitrary"),
        name="mla_prep",
    )(p_bm, cos128t, sin128t, csk, qg, kvg, wqt, wkn, wvt)


def _attn_kernel(qt_ref, k_ref, vt_ref, qt_next_ref, k_next_ref, szb_ref, sgb_ref, wp_ref,
                 x_ref, u_ref, vln_ref, sza_ref, sga_ref, cc_ref, ws_ref, bs_ref,
                 wpa_ref, wo_ref, pg_ref, o_ref,
                 st_scr, mt_scr, m_scr, l_scr, acc_scr, y_scr, ya_scr):
    qi = pl.program_id(1)
    key_chunk = lax.broadcasted_iota(jnp.int32, (ATT_TK, ATT_TQ), 0) // CHUNK
    query_chunk = lax.broadcasted_iota(jnp.int32, (ATT_TK, ATT_TQ), 1) // CHUNK
    diag_mask = key_chunk <= query_chunk

    m_scr[...] = jnp.full(m_scr.shape, -1e30, _F32)
    l_scr[...] = jnp.zeros(l_scr.shape, _F32)
    acc_scr[...] = jnp.zeros(acc_scr.shape, _F32)

    def store_scores(k_tile, q_t, h):
        st = jnp.dot(k_tile, q_t, preferred_element_type=_F32)
        st_scr[h] = st
        mt_scr[h] = jnp.max(st, axis=0, keepdims=True)

    nope_w = MLA_HEADS * MLA_NOPE

    def head_q(q_ref_, h):
        pair = nope_w + ROPE_GROUP * (h // 2)
        return jnp.concatenate([q_ref_[MLA_NOPE * h:MLA_NOPE * (h + 1), :],
                                q_ref_[pair:pair + ROPE_GROUP, :]], axis=0)

    def head_k(k_rows, h):
        rope = nope_w + ROPE_GROUP * (h % 2)
        return jnp.concatenate([k_rows[:, MLA_NOPE * h:MLA_NOPE * (h + 1)],
                                k_rows[:, rope:rope + ROPE_GROUP]], axis=1)

    def scores(j, h):
        off = pl.multiple_of(j * ATT_TK, ATT_TK)
        store_scores(head_k(k_ref.at[pl.ds(off, ATT_TK), :], h), head_q(qt_ref, h), h)

    def accumulate(j, h, masked):
        off = pl.multiple_of(j * ATT_TK, ATT_TK)
        st = st_scr[h]
        if masked:
            st = jnp.where(diag_mask, st, -1e30)
            mt = jnp.max(st, axis=0, keepdims=True)
        else:
            mt = mt_scr[h]
        m_prev = m_scr[h]
        m_new = jnp.maximum(m_prev, mt)
        alpha = jnp.exp2(m_prev - m_new)
        p = jnp.exp2(st - m_new)
        vt = vt_ref[ATT_VROWS * h:ATT_VROWS * (h + 1), pl.ds(off, ATT_TK)]
        pv = jnp.dot(vt, p.astype(_BF16), preferred_element_type=_F32)
        acc_scr[h] = alpha * acc_scr[h] + pv[0:MLA_VDIM, :]
        l_scr[h] = alpha * l_scr[h] + pv[MLA_VDIM:MLA_VDIM + 1, :]
        m_scr[h] = m_new

    @pl.when((pl.program_id(0) == 0) & (qi == 0))
    def _():
        for h in range(MLA_HEADS):
            scores(0, h)

    def body(j, carry):
        for h in range(MLA_HEADS):
            accumulate(j, h, False)
            scores(j + 1, h)
        return carry

    lax.fori_loop(0, qi, body, 0)
    for h in range(MLA_HEADS):
        accumulate(qi, h, True)
        store_scores(head_k(k_next_ref, h), head_q(qt_next_ref, h), h)

    for h in range(MLA_HEADS):
        hs = slice(MLA_VDIM * h, MLA_VDIM * (h + 1))
        o = (acc_scr[h] * (1.0 / l_scr[h])).T
        y_scr[:, hs] = (o * szb_ref[:, hs].astype(_F32)).astype(_BF16)

    cb = sgb_ref[...].astype(_F32) * jnp.dot(y_scr[...], wp_ref[...], preferred_element_type=_F32)

    idx_r = lax.broadcasted_iota(jnp.int32, (GM_BLOCK, GM_BLOCK), 0) // CHUNK
    idx_c = lax.broadcasted_iota(jnp.int32, (GM_BLOCK, GM_BLOCK), 1) // CHUNK
    causal = idx_c <= idx_r

    for g in range(GM_GROUPS):
        ws = jnp.where(causal, ws_ref[g], 0.0).astype(_BF16)
        cs = slice(GM_GW * g, GM_GW * (g + 1))
        for r in range(ATT_TQ // GM_BLOCK):
            rs = slice(GM_BLOCK * r, GM_BLOCK * (r + 1))
            sv = jnp.dot(ws, vln_ref[rs, cs], preferred_element_type=_F32) + bs_ref[g]
            y = u_ref[rs, cs].astype(_F32) * sv * sza_ref[rs, cs].astype(_F32)
            ya_scr[rs, cs] = y.astype(_BF16)

    ca = sga_ref[...].astype(_F32) * jnp.dot(ya_scr[...], wpa_ref[...], preferred_element_type=_F32)
    merged = ca + cb + cc_ref[...].astype(_F32)
    o = jnp.dot(merged.astype(_BF16), wo_ref[...], preferred_element_type=_F32)
    o = o * lax.rsqrt(jnp.mean(o * o, axis=-1, keepdims=True) + EPS) * pg_ref[...]
    o_ref[...] = x_ref[...] + o


def _attention_merge(qt, k, vt, p_bm, x2d, cc, wpb, ws, bs_col, wpa, wo, pg):
    nq = SEQ // ATT_TQ
    rows = lambda b, i: (b * nq + i, 0)
    next_step = lambda b, i: jnp.minimum(b * nq + i + 1, BATCH * nq - 1)
    seg = lambda c: pl.BlockSpec((ATT_TQ, BM_SEG), lambda b, i: (b * nq + i, c))
    tile = pl.BlockSpec((ATT_TQ, D_MODEL), rows)

    def resident(a):
        return pl.BlockSpec(a.shape, lambda b, i: (0,) * a.ndim, pipeline_mode=pl.Buffered(1))

    return pl.pallas_call(
        _attn_kernel,
        grid=(BATCH, nq),
        in_specs=[
            pl.BlockSpec((ATT_QROWS, ATT_TQ), lambda b, i: (0, b * nq + i)),
            pl.BlockSpec((SEQ, ATT_KCOLS), lambda b, i: (b, 0)),
            pl.BlockSpec((MLA_HEADS * ATT_VROWS, SEQ), lambda b, i: (0, b)),
            pl.BlockSpec((ATT_QROWS, ATT_TQ), lambda b, i: (0, next_step(b, i))),
            pl.BlockSpec((ATT_TK, ATT_KCOLS),
                         lambda b, i: ((next_step(b, i) // nq) * (SEQ // ATT_TK), 0)),
            seg(COL_ZB), seg(COL_GB), resident(wpb),
            tile, seg(COL_U), seg(COL_V), seg(COL_ZA), seg(COL_GA), tile,
            resident(ws), resident(bs_col),
            resident(wpa), resident(wo), resident(pg),
        ],
        out_specs=tile,
        out_shape=jax.ShapeDtypeStruct((TOKENS, D_MODEL), _F32),
        scratch_shapes=[
            pltpu.VMEM((MLA_HEADS, ATT_TK, ATT_TQ), _F32),
            pltpu.VMEM((MLA_HEADS, 1, ATT_TQ), _F32),
            pltpu.VMEM((MLA_HEADS, 1, ATT_TQ), _F32),
            pltpu.VMEM((MLA_HEADS, 1, ATT_TQ), _F32),
            pltpu.VMEM((MLA_HEADS, MLA_VDIM, ATT_TQ), _F32),
            pltpu.VMEM((ATT_TQ, MLA_WIDTH), _BF16),
            pltpu.VMEM((ATT_TQ, GM_WIDTH), _BF16),
        ],
        compiler_params=_params("arbitrary", "arbitrary"),
        name="mla_attn_merge",
    )(qt, k, vt, qt, k, p_bm, p_bm, wpb, x2d, p_bm, p_bm, p_bm, p_bm, cc,
      ws, bs_col, wpa, wo, pg)


def _layer_weights(l, mla_w_uq, mla_w_ukv, w_proj_a, w_proj_b, w_proj_c, w_out):
    half = MLA_ROPE // 2
    wq = mla_w_uq[l].reshape(MLA_Q_RANK, MLA_HEADS, MLA_QK_DIM)
    wq_nope = wq[:, :, :MLA_NOPE].reshape(MLA_Q_RANK, -1)
    wq_rope = wq[:, :, MLA_NOPE:].reshape(MLA_Q_RANK, -1)
    wq_rope_rot = jnp.concatenate([-wq[:, :, MLA_NOPE + half:], wq[:, :, MLA_NOPE:MLA_NOPE + half]],
                                  axis=2).reshape(MLA_Q_RANK, -1)
    wq_all = jnp.concatenate([wq_nope, wq_rope, wq_rope_rot], axis=1).T.astype(_BF16)

    wkv = mla_w_ukv[l].reshape(MLA_KV_RANK, MLA_HEADS, MLA_NOPE + MLA_VDIM)
    wkn = wkv[:, :, :MLA_NOPE].reshape(MLA_KV_RANK, -1).astype(_BF16)
    wvt = wkv[:, :, MLA_NOPE:].reshape(MLA_KV_RANK, -1).T.astype(_BF16)

    return dict(wq=wq_all, wkn=wkn, wvt=wvt,
                wpa=w_proj_a[l].astype(_BF16), wpb=w_proj_b[l].astype(_BF16),
                wpc=w_proj_c[l].astype(_BF16), wo=w_out[l].astype(_BF16))


def _rope_tables():
    pos = jnp.arange(SEQ, dtype=_F32)
    inv_freq = ROPE_THETA ** (-jnp.arange(0, MLA_ROPE, 2, dtype=_F32) / MLA_ROPE)
    ang = pos[:, None] * inv_freq[None, :]
    cos = jnp.cos(ang)
    sin = jnp.sin(ang)
    cos128t = jnp.tile(cos, (1, 4)).T
    sin128t = jnp.tile(sin, (1, 4)).T
    csk = jnp.concatenate([cos, cos, sin, sin], axis=1)
    return cos128t, sin128t, csk


def kernel(x, pre_norm_g, w_in, gm_ln_g, gm_ln_b, gm_ws, gm_bs, mla_q_norm_g, mla_w_uq,
           mla_kv_norm_g, mla_w_ukv, lru_conv_w, lru_conv_b, lru_w_a, lru_b_a, lru_w_x,
           lru_b_x, lru_lambda, w_proj_a, w_proj_b, w_proj_c, w_out, post_norm_g):
    cos128t, sin128t, csk = _rope_tables()
    wax = _gate_weights(lru_w_a, lru_w_x)
    w_in_t = jnp.swapaxes(w_in, 1, 2).astype(_BF16)
    x2d = x.reshape(TOKENS, D_MODEL)
    row = lambda a: a.reshape(1, -1)
    for l in range(DEPTH):
        w = _layer_weights(l, mla_w_uq, mla_w_ukv, w_proj_a, w_proj_b, w_proj_c, w_out)
        g_pre = row(pre_norm_g[l])
        p_bm = _proj(x2d, g_pre, row(gm_ln_g[l]), row(gm_ln_b[l]), w_in_t, l)

        cc = _lru_mixer(x2d.reshape(BATCH, SEQ, D_MODEL), g_pre, w_in_t, l,
                        lru_conv_w[l], row(lru_conv_b[l]), wax[l], row(lru_b_a[l]),
                        row(lru_b_x[l]), row(lru_lambda[l]), w["wpc"]).reshape(TOKENS, D_MODEL)

        qt, k, vt = _mla_prep(p_bm, cos128t, sin128t, csk, row(mla_q_norm_g[l]),
                              row(mla_kv_norm_g[l]), w["wq"], w["wkn"], w["wvt"])
        x2d = _attention_merge(qt, k, vt, p_bm, x2d, cc, w["wpb"], gm_ws[l],
                               gm_bs[l][:, :, None], w["wpa"], w["wo"], row(post_norm_g[l]))
    return x2d.reshape(BATCH, SEQ, D_MODEL)
```

```python
import math

import jax
import jax.numpy as jnp
from jax import lax
from jax.experimental import pallas as pl
from jax.experimental.pallas import tpu as pltpu

D_MODEL = 1024
BATCH = 8
SEQ = 2048
DEPTH = 2
TOKENS = BATCH * SEQ
CHUNK = 64
EPS = 1e-6

GM_WIDTH = 1024
GM_GROUPS = 4
GM_BLOCK = 128
GM_GW = GM_WIDTH // GM_GROUPS

MLA_HEADS = 8
MLA_NOPE = 128
MLA_ROPE = 64
MLA_VDIM = 128
MLA_QK_DIM = MLA_NOPE + MLA_ROPE
MLA_Q_RANK = 384
MLA_KV_RANK = 256
MLA_WIDTH = MLA_HEADS * MLA_VDIM
ROPE_THETA = 10000.0
ROPE_GROUP = 2 * MLA_ROPE

LRU_WIDTH = 1280
LRU_BLOCKS = 16
LRU_BW = LRU_WIDTH // LRU_BLOCKS
LRU_C = 8.0
CONV_W = 4

IN_SIZES = (GM_WIDTH, GM_WIDTH, GM_WIDTH, MLA_Q_RANK, MLA_KV_RANK, MLA_ROPE, MLA_WIDTH,
            LRU_WIDTH, LRU_WIDTH, D_MODEL, D_MODEL, D_MODEL)
IN_CUTS = tuple(sum(IN_SIZES[:k]) for k in range(len(IN_SIZES) + 1))
(IN_XC, IN_ZC, IN_GC) = (7, 8, 11)

BM_SEG = 1024
(COL_U, COL_V, COL_ZA, COL_ZB, COL_GA, COL_GB) = range(6)
LATENT_W = MLA_Q_RANK + MLA_KV_RANK + 2 * MLA_ROPE
N_BM = 6 * BM_SEG + LATENT_W
COL_LATENT = (6 * BM_SEG) // LATENT_W

VMEM_LIMIT_BYTES = 56 * 1024 * 1024
LANES = 128

PROJ_TM = 1024
PROJ_TN = 2304
LRU_TS = 128
GATE_TN = 256
PREP_TM = 2048
ATT_TQ = 256
ATT_TK = 256
ATT_D = 256
ATT_QROWS = MLA_HEADS * MLA_NOPE + (MLA_HEADS // 2) * ROPE_GROUP
ATT_KCOLS = MLA_HEADS * MLA_NOPE + 2 * ROPE_GROUP
ATT_VROWS = MLA_VDIM + 16

_F32 = jnp.float32
_BF16 = jnp.bfloat16


def _params(*sem):
    return pltpu.CompilerParams(dimension_semantics=sem, vmem_limit_bytes=VMEM_LIMIT_BYTES)


def _sigmoid(x):
    return 0.5 * jnp.tanh(0.5 * x) + 0.5


def _silu(x):
    h = 0.5 * x
    return h * jnp.tanh(h) + h


def _rmsnorm_bf16(x, g):
    ms = jnp.mean(x * x, axis=-1, keepdims=True)
    return (x * lax.rsqrt(ms + EPS) * g).astype(_BF16)


def _dot_nt(a, b_t):
    return lax.dot_general(a, b_t, (((1,), (1,)), ((), ())), preferred_element_type=_F32)


BM_SOURCES = ((0, GM_WIDTH, "raw"), (1, GM_WIDTH, "ln"), (2, GM_WIDTH, "silu"), (6, MLA_WIDTH, "silu"),
              (9, D_MODEL, "sigmoid"), (10, D_MODEL, "sigmoid"), (3, LATENT_W, "latent"))
PROJ_SLOT_ROWS = (1024, 1024, 768)


def _proj_plan(step):
    lo, hi = step * PROJ_TN, (step + 1) * PROJ_TN
    plan, c0 = [], 0
    for k, width, kind in BM_SOURCES:
        a, b = max(c0, lo), min(c0 + width, hi)
        if a < b:
            assert kind != "ln" or b - a == width, "layernorm needs its whole segment in one step"
            plan.append((a - lo, b - lo, kind, IN_CUTS[k] + a - c0))
        c0 += width
    assert len(plan) == len(PROJ_SLOT_ROWS)
    assert all(c1 - c0 <= rows for (c0, c1, _, _), rows in zip(plan, PROJ_SLOT_ROWS))
    return plan


def _proj_kernel(x_ref, g_ref, lng_ref, lnb_ref, w0_ref, w1_ref, w2_ref, o_ref, h_scr):
    j = pl.program_id(1)

    @pl.when(j == 0)
    def _():
        h_scr[...] = _rmsnorm_bf16(x_ref[...], g_ref[...])

    def piece(c0, c1, kind, w_rows):
        r = _dot_nt(h_scr[...], w_rows)
        if kind == "ln":
            mu = jnp.mean(r, axis=-1, keepdims=True)
            rc = r - mu
            var = jnp.mean(rc * rc, axis=-1, keepdims=True)
            r = rc * lax.rsqrt(var + EPS) * lng_ref[...] + lnb_ref[...]
        elif kind == "silu":
            r = _silu(r)
        elif kind == "sigmoid":
            r = _sigmoid(r)
        elif kind == "latent":
            t = r[:, c1 - c0 - ROPE_GROUP:]
            lane = lax.broadcasted_iota(jnp.int32, t.shape, 1)
            half = MLA_ROPE // 2
            t = jnp.where(lane < MLA_ROPE, t,
                          jnp.where(lane < MLA_ROPE + half, -pltpu.roll(t, half, 1),
                                    pltpu.roll(t, MLA_ROPE + half, 1)))
            r = jnp.concatenate([r[:, :c1 - c0 - ROPE_GROUP], t], axis=1)
        o_ref[:, c0:c1] = r.astype(o_ref.dtype)

    for step in range(N_BM // PROJ_TN):
        @pl.when(j == step)
        def _(step=step):
            for (c0, c1, kind, _), w_ref in zip(_proj_plan(step), (w0_ref, w1_ref, w2_ref)):
                piece(c0, c1, kind, w_ref[0, 0:c1 - c0, :])


def _proj(x2d, g, lng, lnb, w_in_t, layer):
    vec = pl.BlockSpec((1, D_MODEL), lambda i, j: (0, 0))
    n_steps = N_BM // PROJ_TN
    plans = [_proj_plan(step) for step in range(n_steps)]

    def w_rows(slot):
        starts = [plans[step][slot][3] for step in range(n_steps)]

        def index_map(i, j):
            row = starts[-1]
            for step in range(n_steps - 2, -1, -1):
                row = jnp.where(j == step, starts[step], row)
            return (layer, row, 0)

        return pl.BlockSpec((pl.Element(1), pl.Element(PROJ_SLOT_ROWS[slot]), pl.Element(D_MODEL)),
                            index_map)

    return pl.pallas_call(
        _proj_kernel,
        grid=(TOKENS // PROJ_TM, n_steps),
        in_specs=[
            pl.BlockSpec((PROJ_TM, D_MODEL), lambda i, j: (i, 0)),
            vec, vec, vec,
            w_rows(0), w_rows(1), w_rows(2),
        ],
        out_specs=pl.BlockSpec((PROJ_TM, PROJ_TN), lambda i, j: (i, j)),
        out_shape=jax.ShapeDtypeStruct((TOKENS, N_BM), _BF16),
        scratch_shapes=[pltpu.VMEM((PROJ_TM, D_MODEL), _BF16)],
        compiler_params=_params("arbitrary", "arbitrary"),
        name="proj_bm",
    )(x2d, g, lng, lnb, w_in_t, w_in_t, w_in_t)


def _gate_k_range(c):
    first_row = (c * GATE_TN) // LRU_BW * LRU_BW
    last_row = (((c + 1) * GATE_TN - 1) // LRU_BW + 1) * LRU_BW
    k0 = first_row // LANES * LANES
    width = -(-(last_row - k0) // GATE_TN) * GATE_TN
    k0 = min(k0, LRU_WIDTH - width)
    return k0, k0 + width


GATE_K_MAX = max(k1 - k0 for k0, k1 in map(_gate_k_range, range(LRU_WIDTH // GATE_TN)))


def _gate_weights_kernel(wa_ref, wx_ref, o_ref, stage):
    for c in range(LRU_WIDTH // GATE_TN):
        k0, _ = _gate_k_range(c)
        stage[...] = jnp.zeros(stage.shape, _F32)
        for h in range(LRU_BLOCKS):
            lo = max(LRU_BW * h, GATE_TN * c)
            hi = min(LRU_BW * (h + 1), GATE_TN * (c + 1))
            if lo >= hi:
                continue
            r0 = LRU_BW * h - k0
            src = slice(lo - LRU_BW * h, hi - LRU_BW * h)
            dst = slice(lo - GATE_TN * c, hi - GATE_TN * c)
            stage[r0:r0 + LRU_BW, dst] = wa_ref[h][:, src]
            stage[r0:r0 + LRU_BW, GATE_TN + dst.start:GATE_TN + dst.stop] = wx_ref[h][:, src]
        o_ref[c] = stage[...].astype(o_ref.dtype)


def _gate_weights(w_a, w_x):
    n_tiles = LRU_WIDTH // GATE_TN
    blocks = pl.BlockSpec((None, LRU_BLOCKS, LRU_BW, LRU_BW), lambda l: (l, 0, 0, 0))
    return pl.pallas_call(
        _gate_weights_kernel,
        grid=(DEPTH,),
        in_specs=[blocks, blocks],
        out_specs=pl.BlockSpec((None, n_tiles, GATE_K_MAX, 2 * GATE_TN), lambda l: (l, 0, 0, 0)),
        out_shape=jax.ShapeDtypeStruct((DEPTH, n_tiles, GATE_K_MAX, 2 * GATE_TN), _BF16),
        scratch_shapes=[pltpu.VMEM((GATE_K_MAX, 2 * GATE_TN), _F32)],
        compiler_params=_params("arbitrary"),
        name="gate_weights",
    )(w_a, w_x)


def _lru_kernel(x_ref, pg_ref, wxc_ref, wz_ref, wg_ref, cw_ref, cb_ref, wax_ref, ba_ref, bx_ref,
                lam_ref, wp_ref, o_ref, xext, a_scr, b_scr, h_scr, sz_scr, sg_scr):
    rows = LRU_TS * BATCH
    halo = (CONV_W - 1) * BATCH

    @pl.when(pl.program_id(0) == 0)
    def _():
        xext[0:halo, :] = jnp.zeros((halo, LRU_WIDTH), _F32)
        h_scr[...] = jnp.zeros_like(h_scr)

    xt = jnp.swapaxes(x_ref[...], 0, 1).reshape(rows, D_MODEL)
    hn = _rmsnorm_bf16(xt, pg_ref[...])

    xext[halo:halo + rows, :] = _dot_nt(hn, wxc_ref[0])
    xc = cb_ref[...] + cw_ref[CONV_W - 1:CONV_W, :] * xext[halo:halo + rows, :]
    for k in range(CONV_W - 1):
        xc = xc + cw_ref[k:k + 1, :] * xext[k * BATCH:k * BATCH + rows, :]
    xext[0:halo, :] = xext[rows:rows + halo, :]

    xcb = xc.astype(_BF16)
    lam = lam_ref[...]
    softplus_neg_lam = jnp.maximum(-lam, 0.0) + jnp.log(1.0 + jnp.exp(-jnp.abs(lam)))
    for c in range(LRU_WIDTH // GATE_TN):
        cs = slice(c * GATE_TN, (c + 1) * GATE_TN)
        k0, k1 = _gate_k_range(c)
        ri = jnp.dot(xcb[:, k0:k1], wax_ref[c, 0:k1 - k0, :], preferred_element_type=_F32)
        half_log = (-0.5 * LRU_C) * softplus_neg_lam[:, cs]
        a = jnp.exp(half_log * jnp.tanh(0.5 * (ri[:, 0:GATE_TN] + ba_ref[:, cs])) + half_log)
        i = _sigmoid(ri[:, GATE_TN:2 * GATE_TN] + bx_ref[:, cs])
        gap = 1.0 - a * a
        mult = jnp.where(gap > 0.0, gap * lax.rsqrt(gap), 0.0)
        a_scr[:, cs] = a
        b_scr[:, cs] = mult * (i * xc[:, cs])

    sz_scr[...] = _silu(_dot_nt(hn, wz_ref[0]))
    sg_scr[...] = _sigmoid(_dot_nt(hn, wg_ref[0]))

    h = h_scr[...]
    for s in range(LRU_TS):
        rs = slice(s * BATCH, (s + 1) * BATCH)
        h = a_scr[rs, :] * h + b_scr[rs, :]
        b_scr[rs, :] = h
    h_scr[...] = h

    y = (b_scr[...] * sz_scr[...]).astype(_BF16)
    cc = sg_scr[...] * jnp.dot(y, wp_ref[...], preferred_element_type=_F32)
    o_ref[...] = jnp.swapaxes(cc.reshape(LRU_TS, BATCH, D_MODEL), 0, 1).astype(o_ref.dtype)


def _lru_mixer(x3d, pre_g, w_in_t, layer, conv_w, conv_b, wax, ba, bx, lam, wp):
    rows = LRU_TS * BATCH
    halo = (CONV_W - 1) * BATCH

    def resident(a):
        return pl.BlockSpec(a.shape, lambda t: (0,) * a.ndim, pipeline_mode=pl.Buffered(1))

    def w_rows(k):
        return pl.BlockSpec((pl.Element(1), pl.Element(IN_SIZES[k]), pl.Element(D_MODEL)),
                            lambda t: (layer, IN_CUTS[k], 0), pipeline_mode=pl.Buffered(1))

    return pl.pallas_call(
        _lru_kernel,
        grid=(SEQ // LRU_TS,),
        in_specs=[pl.BlockSpec((BATCH, LRU_TS, D_MODEL), lambda t: (0, t, 0)), resident(pre_g),
                  w_rows(IN_XC), w_rows(IN_ZC), w_rows(IN_GC)]
        + [resident(a) for a in (conv_w, conv_b, wax, ba, bx, lam, wp)],
        out_specs=pl.BlockSpec((BATCH, LRU_TS, D_MODEL), lambda t: (0, t, 0)),
        out_shape=jax.ShapeDtypeStruct((BATCH, SEQ, D_MODEL), _BF16),
        scratch_shapes=[
            pltpu.VMEM((rows + halo, LRU_WIDTH), _F32),
            pltpu.VMEM((rows, LRU_WIDTH), _F32),
            pltpu.VMEM((rows, LRU_WIDTH), _F32),
            pltpu.VMEM((BATCH, LRU_WIDTH), _F32),
            pltpu.VMEM((rows, LRU_WIDTH), _F32),
            pltpu.VMEM((rows, D_MODEL), _F32),
        ],
        compiler_params=_params("arbitrary"),
        name="lru_mixer",
    )(x3d, pre_g, w_in_t, w_in_t, w_in_t, conv_w, conv_b, wax, ba, bx, lam, wp)


def _mla_prep_kernel(c_ref, cost_ref, sint_ref, csk_ref, qg_ref, kvg_ref, wqt_ref, wkn_ref, wvt_ref,
                     qt_ref, k_ref, vt_ref):
    qscale = math.log2(math.e) / math.sqrt(MLA_QK_DIM)
    c = c_ref[...].astype(_F32)
    cq = c[:, 0:MLA_Q_RANK]
    ckv = c[:, MLA_Q_RANK:MLA_Q_RANK + MLA_KV_RANK]
    krk = c[:, MLA_Q_RANK + MLA_KV_RANK:LATENT_W]

    hq = (cq * lax.rsqrt(jnp.mean(cq * cq, axis=-1, keepdims=True) + EPS)
          * qg_ref[...]).astype(_BF16)
    qt = lax.dot_general(wqt_ref[...], hq, (((1,), (1,)), ((), ())), preferred_element_type=_F32)
    nope_w = MLA_HEADS * MLA_NOPE
    rope_w = MLA_HEADS * MLA_ROPE
    cost = cost_ref[...]
    sint = sint_ref[...]
    for p in range(rope_w // ROPE_GROUP):
        qr = qt[nope_w + ROPE_GROUP * p:nope_w + ROPE_GROUP * (p + 1), :]
        qrr = qt[nope_w + rope_w + ROPE_GROUP * p:nope_w + rope_w + ROPE_GROUP * (p + 1), :]
        qt_ref[nope_w + ROPE_GROUP * p:nope_w + ROPE_GROUP * (p + 1), :] = (
            (qr * cost + qrr * sint) * qscale).astype(_BF16)
    qt_ref[0:nope_w, :] = (qt[0:nope_w, :] * qscale).astype(_BF16)

    hkv = (ckv * lax.rsqrt(jnp.mean(ckv * ckv, axis=-1, keepdims=True) + EPS)
           * kvg_ref[...]).astype(_BF16)
    kn = jnp.dot(hkv, wkn_ref[...], preferred_element_type=_F32)
    vt = lax.dot_general(wvt_ref[...], hkv, (((1,), (1,)), ((), ())),
                         preferred_element_type=_F32).astype(_BF16)
    for h in range(MLA_HEADS):
        vt_ref[ATT_VROWS * h:ATT_VROWS * h + MLA_VDIM, :] = vt[MLA_VDIM * h:MLA_VDIM * (h + 1), :]
        vt_ref[ATT_VROWS * h + MLA_VDIM:ATT_VROWS * (h + 1), :] = jnp.ones(
            (ATT_VROWS - MLA_VDIM, vt.shape[1]), _BF16)

    t = krk * csk_ref[...]
    kf2 = t + pltpu.roll(t, MLA_ROPE, 1)
    lane = lax.broadcasted_iota(jnp.int32, kf2.shape, 1)
    k_ref[:, 0:nope_w] = kn.astype(_BF16)
    k_ref[:, nope_w:nope_w + ROPE_GROUP] = jnp.where(lane < MLA_ROPE, kf2, 0.0).astype(_BF16)
    k_ref[:, nope_w + ROPE_GROUP:nope_w + 2 * ROPE_GROUP] = jnp.where(
        lane >= MLA_ROPE, kf2, 0.0).astype(_BF16)


def _mla_prep(p_bm, cos128t, sin128t, csk, qg, kvg, wqt, wkn, wvt):
    tiles_per_seq = SEQ // PREP_TM
    const = lambda i: (0, 0)
    pos = lambda i: (i % tiles_per_seq, 0)
    pos_t = lambda i: (0, i % tiles_per_seq)
    return pl.pallas_call(
        _mla_prep_kernel,
        grid=(TOKENS // PREP_TM,),
        in_specs=[
            pl.BlockSpec((PREP_TM, LATENT_W), lambda i: (i, COL_LATENT)),
            pl.BlockSpec((ROPE_GROUP, PREP_TM), pos_t),
            pl.BlockSpec((ROPE_GROUP, PREP_TM), pos_t),
            pl.BlockSpec((PREP_TM, ROPE_GROUP), pos),
            pl.BlockSpec((1, MLA_Q_RANK), const),
            pl.BlockSpec((1, MLA_KV_RANK), const),
            pl.BlockSpec(wqt.shape, const),
            pl.BlockSpec(wkn.shape, const),
            pl.BlockSpec(wvt.shape, const),
        ],
        out_specs=[
            pl.BlockSpec((ATT_QROWS, PREP_TM), lambda i: (0, i)),
            pl.BlockSpec((PREP_TM, ATT_KCOLS), lambda i: (i, 0)),
            pl.BlockSpec((MLA_HEADS * ATT_VROWS, PREP_TM), lambda i: (0, i)),
        ],
        out_shape=[
            jax.ShapeDtypeStruct((ATT_QROWS, TOKENS), _BF16),
            jax.ShapeDtypeStruct((TOKENS, ATT_KCOLS), _BF16),
            jax.ShapeDtypeStruct((MLA_HEADS * ATT_VROWS, TOKENS), _BF16),
        ],
        compiler_params=_params("arbitrary"),
        name="mla_prep",
    )(p_bm, cos128t, sin128t, csk, qg, kvg, wqt, wkn, wvt)


def _attn_kernel(qt_ref, k_ref, vt_ref, qt_next_ref, k_next_ref, szb_ref, sgb_ref, wp_ref,
                 x_ref, u_ref, vln_ref, sza_ref, sga_ref, cc_ref, ws_ref, bs_ref,
                 wpa_ref, wo_ref, pg_ref, o_ref,
                 st_scr, mt_scr, m_scr, l_scr, acc_scr, y_scr, ya_scr):
    qi = pl.program_id(1)
    key_chunk = lax.broadcasted_iota(jnp.int32, (ATT_TK, ATT_TQ), 0) // CHUNK
    query_chunk = lax.broadcasted_iota(jnp.int32, (ATT_TK, ATT_TQ), 1) // CHUNK
    diag_mask = key_chunk <= query_chunk

    m_scr[...] = jnp.full(m_scr.shape, -1e30, _F32)
    l_scr[...] = jnp.zeros(l_scr.shape, _F32)
    acc_scr[...] = jnp.zeros(acc_scr.shape, _F32)

    def store_scores(k_tile, q_t, h):
        st = jnp.dot(k_tile, q_t, preferred_element_type=_F32)
        st_scr[h] = st
        mt_scr[h] = jnp.max(st, axis=0, keepdims=True)

    nope_w = MLA_HEADS * MLA_NOPE

    def head_q(q_ref_, h):
        pair = nope_w + ROPE_GROUP * (h // 2)
        return jnp.concatenate([q_ref_[MLA_NOPE * h:MLA_NOPE * (h + 1), :],
                                q_ref_[pair:pair + ROPE_GROUP, :]], axis=0)

    def head_k(k_rows, h):
        rope = nope_w + ROPE_GROUP * (h % 2)
        return jnp.concatenate([k_rows[:, MLA_NOPE * h:MLA_NOPE * (h + 1)],
                                k_rows[:, rope:rope + ROPE_GROUP]], axis=1)

    def scores(j, h):
        off = pl.multiple_of(j * ATT_TK, ATT_TK)
        store_scores(head_k(k_ref.at[pl.ds(off, ATT_TK), :], h), head_q(qt_ref, h), h)

    def accumulate(j, h, masked):
        off = pl.multiple_of(j * ATT_TK, ATT_TK)
        st = st_scr[h]
        if masked:
            st = jnp.where(diag_mask, st, -1e30)
            mt = jnp.max(st, axis=0, keepdims=True)
        else:
            mt = mt_scr[h]
        m_prev = m_scr[h]
        m_new = jnp.maximum(m_prev, mt)
        alpha = jnp.exp2(m_prev - m_new)
        p = jnp.exp2(st - m_new)
        vt = vt_ref[ATT_VROWS * h:ATT_VROWS * (h + 1), pl.ds(off, ATT_TK)]
        pv = jnp.dot(vt, p.astype(_BF16), preferred_element_type=_F32)
        acc_scr[h] = alpha * acc_scr[h] + pv[0:MLA_VDIM, :]
        l_scr[h] = alpha * l_scr[h] + pv[MLA_VDIM:MLA_VDIM + 1, :]
        m_scr[h] = m_new

    @pl.when((pl.program_id(0) == 0) & (qi == 0))
    def _():
        for h in range(MLA_HEADS):
            scores(0, h)

    def body(j, carry):
        for h in range(MLA_HEADS):
            accumulate(j, h, False)
            scores(j + 1, h)
        return carry

    lax.fori_loop(0, qi, body, 0)
    for h in range(MLA_HEADS):
        accumulate(qi, h, True)
        store_scores(head_k(k_next_ref, h), head_q(qt_next_ref, h), h)

    for h in range(MLA_HEADS):
        hs = slice(MLA_VDIM * h, MLA_VDIM * (h + 1))
        o = (acc_scr[h] * (1.0 / l_scr[h])).T
        y_scr[:, hs] = (o * szb_ref[:, hs].astype(_F32)).astype(_BF16)

    cb = sgb_ref[...].astype(_F32) * jnp.dot(y_scr[...], wp_ref[...], preferred_element_type=_F32)

    idx_r = lax.broadcasted_iota(jnp.int32, (GM_BLOCK, GM_BLOCK), 0) // CHUNK
    idx_c = lax.broadcasted_iota(jnp.int32, (GM_BLOCK, GM_BLOCK), 1) // CHUNK
    causal = idx_c <= idx_r

    for g in range(GM_GROUPS):
        ws = jnp.where(causal, ws_ref[g], 0.0).astype(_BF16)
        cs = slice(GM_GW * g, GM_GW * (g + 1))
        for r in range(ATT_TQ // GM_BLOCK):
            rs = slice(GM_BLOCK * r, GM_BLOCK * (r + 1))
            sv = jnp.dot(ws, vln_ref[rs, cs], preferred_element_type=_F32) + bs_ref[g]
            y = u_ref[rs, cs].astype(_F32) * sv * sza_ref[rs, cs].astype(_F32)
            ya_scr[rs, cs] = y.astype(_BF16)

    ca = sga_ref[...].astype(_F32) * jnp.dot(ya_scr[...], wpa_ref[...], preferred_element_type=_F32)
    merged = ca + cb + cc_ref[...].astype(_F32)
    o = jnp.dot(merged.astype(_BF16), wo_ref[...], preferred_element_type=_F32)
    o = o * lax.rsqrt(jnp.mean(o * o, axis=-1, keepdims=True) + EPS) * pg_ref[...]
    o_ref[...] = x_ref[...] + o


def _attention_merge(qt, k, vt, p_bm, x2d, cc, wpb, ws, bs_col, wpa, wo, pg):
    nq = SEQ // ATT_TQ
    rows = lambda b, i: (b * nq + i, 0)
    next_step = lambda b, i: jnp.minimum(b * nq + i + 1, BATCH * nq - 1)
    seg = lambda c: pl.BlockSpec((ATT_TQ, BM_SEG), lambda b, i: (b * nq + i, c))
    tile = pl.BlockSpec((ATT_TQ, D_MODEL), rows)

    def resident(a):
        return pl.BlockSpec(a.shape, lambda b, i: (0,) * a.ndim, pipeline_mode=pl.Buffered(1))

    return pl.pallas_call(
        _attn_kernel,
        grid=(BATCH, nq),
        in_specs=[
            pl.BlockSpec((ATT_QROWS, ATT_TQ), lambda b, i: (0, b * nq + i)),
            pl.BlockSpec((SEQ, ATT_KCOLS), lambda b, i: (b, 0)),
            pl.BlockSpec((MLA_HEADS * ATT_VROWS, SEQ), lambda b, i: (0, b)),
            pl.BlockSpec((ATT_QROWS, ATT_TQ), lambda b, i: (0, next_step(b, i))),
            pl.BlockSpec((ATT_TK, ATT_KCOLS),
                         lambda b, i: ((next_step(b, i) // nq) * (SEQ // ATT_TK), 0)),
            seg(COL_ZB), seg(COL_GB), resident(wpb),
            tile, seg(COL_U), seg(COL_V), seg(COL_ZA), seg(COL_GA), tile,
            resident(ws), resident(bs_col),
            resident(wpa), resident(wo), resident(pg),
        ],
        out_specs=tile,
        out_shape=jax.ShapeDtypeStruct((TOKENS, D_MODEL), _F32),
        scratch_shapes=[
            pltpu.VMEM((MLA_HEADS, ATT_TK, ATT_TQ), _F32),
            pltpu.VMEM((MLA_HEADS, 1, ATT_TQ), _F32),
            pltpu.VMEM((MLA_HEADS, 1, ATT_TQ), _F32),
            pltpu.VMEM((MLA_HEADS, 1, ATT_TQ), _F32),
            pltpu.VMEM((MLA_HEADS, MLA_VDIM, ATT_TQ), _F32),
            pltpu.VMEM((ATT_TQ, MLA_WIDTH), _BF16),
            pltpu.VMEM((ATT_TQ, GM_WIDTH), _BF16),
        ],
        compiler_params=_params("arbitrary", "arbitrary"),
        name="mla_attn_merge",
    )(qt, k, vt, qt, k, p_bm, p_bm, wpb, x2d, p_bm, p_bm, p_bm, p_bm, cc,
      ws, bs_col, wpa, wo, pg)


def _layer_weights(l, mla_w_uq, mla_w_ukv, w_proj_a, w_proj_b, w_proj_c, w_out):
    half = MLA_ROPE // 2
    wq = mla_w_uq[l].reshape(MLA_Q_RANK, MLA_HEADS, MLA_QK_DIM)
    wq_nope = wq[:, :, :MLA_NOPE].reshape(MLA_Q_RANK, -1)
    wq_rope = wq[:, :, MLA_NOPE:].reshape(MLA_Q_RANK, -1)
    wq_rope_rot = jnp.concatenate([-wq[:, :, MLA_NOPE + half:], wq[:, :, MLA_NOPE:MLA_NOPE + half]],
                                  axis=2).reshape(MLA_Q_RANK, -1)
    wq_all = jnp.concatenate([wq_nope, wq_rope, wq_rope_rot], axis=1).T.astype(_BF16)

    wkv = mla_w_ukv[l].reshape(MLA_KV_RANK, MLA_HEADS, MLA_NOPE + MLA_VDIM)
    wkn = wkv[:, :, :MLA_NOPE].reshape(MLA_KV_RANK, -1).astype(_BF16)
    wvt = wkv[:, :, MLA_NOPE:].reshape(MLA_KV_RANK, -1).T.astype(_BF16)

    return dict(wq=wq_all, wkn=wkn, wvt=wvt,
                wpa=w_proj_a[l].astype(_BF16), wpb=w_proj_b[l].astype(_BF16),
                wpc=w_proj_c[l].astype(_BF16), wo=w_out[l].astype(_BF16))


def _rope_tables():
    pos = jnp.arange(SEQ, dtype=_F32)
    inv_freq = ROPE_THETA ** (-jnp.arange(0, MLA_ROPE, 2, dtype=_F32) / MLA_ROPE)
    ang = pos[:, None] * inv_freq[None, :]
    cos = jnp.cos(ang)
    sin = jnp.sin(ang)
    cos128t = jnp.tile(cos, (1, 4)).T
    sin128t = jnp.tile(sin, (1, 4)).T
    csk = jnp.concatenate([cos, cos, sin, sin], axis=1)
    return cos128t, sin128t, csk


def kernel(x, pre_norm_g, w_in, gm_ln_g, gm_ln_b, gm_ws, gm_bs, mla_q_norm_g, mla_w_uq,
           mla_kv_norm_g, mla_w_ukv, lru_conv_w, lru_conv_b, lru_w_a, lru_b_a, lru_w_x,
           lru_b_x, lru_lambda, w_proj_a, w_proj_b, w_proj_c, w_out, post_norm_g):
    cos128t, sin128t, csk = _rope_tables()
    wax = _gate_weights(lru_w_a, lru_w_x)
    w_in_t = jnp.swapaxes(w_in, 1, 2).astype(_BF16)
    x2d = x.reshape(TOKENS, D_MODEL)
    row = lambda a: a.reshape(1, -1)
    for l in range(DEPTH):
        w = _layer_weights(l, mla_w_uq, mla_w_ukv, w_proj_a, w_proj_b, w_proj_c, w_out)
        g_pre = row(pre_norm_g[l])
        p_bm = _proj(x2d, g_pre, row(gm_ln_g[l]), row(gm_ln_b[l]), w_in_t, l)

        cc = _lru_mixer(x2d.reshape(BATCH, SEQ, D_MODEL), g_pre, w_in_t, l,
                        lru_conv_w[l], row(lru_conv_b[l]), wax[l], row(lru_b_a[l]),
                        row(lru_b_x[l]), row(lru_lambda[l]), w["wpc"]).reshape(TOKENS, D_MODEL)

        qt, k, vt = _mla_prep(p_bm, cos128t, sin128t, csk, row(mla_q_norm_g[l]),
                              row(mla_kv_norm_g[l]), w["wq"], w["wkn"], w["wvt"])
        x2d = _attention_merge(qt, k, vt, p_bm, x2d, cc, w["wpb"], gm_ws[l],
                               gm_bs[l][:, :, None], w["wpa"], w["wo"], row(post_norm_g[l]))
    return x2d.reshape(BATCH, SEQ, D_MODEL)
```
